```python
import math
import jax, jax.numpy as jnp
from jax import lax
import numpy as np

D_MODEL = 1024
BATCH = 32
SEQ = 2048
DEPTH = 4

CHUNK = 64
Q_BLOCK = 128
N_MEM = 256
EPS = 1e-6

DA_HEADS = 4
DA_DIM = 64
DA_VDIM = 2 * DA_DIM
SB_HEADS = 4
SB_DIM = 64
MLA_HEADS = 4
MLA_NOPE = 64
MLA_ROPE = 32
MLA_V = 64
MLA_Q_RANK = 256
MLA_KV_RANK = 128
ROPE_THETA = 10000.0
NUM_BUCKETS = 32
MAX_DISTANCE = 128
MEM_HEADS = 4
MEM_DIM = 64
D_FF = 4 * D_MODEL

MIX_WIDTH = DA_HEADS * DA_VDIM + SB_HEADS * SB_DIM + MLA_HEADS * MLA_V
IN_SIZES = (DA_HEADS * 2 * DA_DIM, DA_HEADS * 2 * DA_DIM, DA_HEADS * DA_VDIM,
            SB_HEADS * SB_DIM, SB_HEADS * SB_DIM, SB_HEADS * SB_DIM,
            MLA_Q_RANK, MLA_KV_RANK, MLA_ROPE)
IN_COLS = sum(IN_SIZES)

kernel_name = "hybrid_chunk_causal_diff_sb_mla_trunk"


def rms_norm(x, g):
    xf = x.astype(jnp.float32)
    y = xf * lax.rsqrt(jnp.mean(xf * xf, axis=-1, keepdims=True) + EPS)
    return (y * g.astype(jnp.float32)).astype(x.dtype)


def split_points():
    pts, acc = [], 0
    for n in IN_SIZES[:-1]:
        acc += n
        pts.append(acc)
    return pts


def t5_bucket(rel):
    nb = NUM_BUCKETS // 2
    bucket = (rel > 0).astype(jnp.int32) * nb
    n = jnp.abs(rel)
    max_exact = nb // 2
    is_small = n < max_exact
    large = max_exact + (jnp.log(jnp.maximum(n, 1).astype(jnp.float32) / max_exact)
                         / math.log(MAX_DISTANCE / max_exact) * (nb - max_exact)).astype(jnp.int32)
    large = jnp.minimum(large, nb - 1)
    return bucket + jnp.where(is_small, n, large)


def chunk_mask(q_pos, k_pos):
    return (k_pos[None, :] // CHUNK) <= (q_pos[:, None] // CHUNK)


def rope(x, pos):
    half = MLA_ROPE // 2
    freqs = ROPE_THETA ** (-jnp.arange(half, dtype=jnp.float32) / half)
    ang = pos.astype(jnp.float32)[:, None] * freqs[None, :]
    cos = jnp.cos(ang)[None, :, None, :]
    sin = jnp.sin(ang)[None, :, None, :]
    xf = x.astype(jnp.float32)
    x1, x2 = xf[..., :half], xf[..., half:]
    return jnp.concatenate([x1 * cos - x2 * sin, x2 * cos + x1 * sin], axis=-1).astype(x.dtype)


def over_query_blocks(block_fn, q):
    b, s = q.shape[0], q.shape[1]
    nb = s // Q_BLOCK
    qs = jnp.moveaxis(q.reshape((b, nb, Q_BLOCK) + q.shape[2:]), 1, 0)
    out = lax.map(lambda a: block_fn(a[0], a[1]), (jnp.arange(nb, dtype=jnp.int32), qs))
    out = jnp.moveaxis(out, 0, 1)
    return out.reshape((b, s) + out.shape[3:])


def diff_attention(q, k, v, rel_bias, q_g, k_g, lam_p, subln_g, layer):
    b, s, _ = q.shape
    q = rms_norm(q.reshape(b, s, DA_HEADS, 2, DA_DIM), q_g)
    k = rms_norm(k.reshape(b, s, DA_HEADS, 2, DA_DIM), k_g)
    v = v.reshape(b, s, DA_HEADS, DA_VDIM)
    lam_init = 0.8 - 0.6 * math.exp(-0.3 * layer)
    lp = lam_p.astype(jnp.float32)
    lam = jnp.exp(jnp.sum(lp[0] * lp[1])) - jnp.exp(jnp.sum(lp[2] * lp[3])) + lam_init
    k_pos = jnp.arange(s, dtype=jnp.int32)

    def block(i, qb):
        q_pos = i * Q_BLOCK + jnp.arange(Q_BLOCK, dtype=jnp.int32)
        logits = jnp.einsum('bqhmd,bkhmd->bhmqk', qb, k,
                            preferred_element_type=jnp.float32) * (DA_DIM ** -0.5)
        bias = rel_bias[t5_bucket(k_pos[None, :] - q_pos[:, None])].astype(jnp.float32)
        bias = bias.reshape(Q_BLOCK, s, DA_HEADS, 2).transpose(2, 3, 0, 1)
        logits = jnp.where(chunk_mask(q_pos, k_pos), logits + bias, -jnp.inf)
        p = jax.nn.softmax(logits, axis=-1)
        a = p[:, :, 0] - lam * p[:, :, 1]
        return jnp.einsum('bhqk,bkhe->bqhe', a.astype(v.dtype), v)

    o = over_query_blocks(block, q)
    o = rms_norm(o, subln_g) * (1.0 - lam_init)
    return o.reshape(b, s, DA_HEADS * DA_VDIM)


def stick_breaking(q, k, v, out_g):
    b, s, _ = q.shape
    q = q.reshape(b, s, SB_HEADS, SB_DIM)
    k = k.reshape(b, s, SB_HEADS, SB_DIM)
    v = v.reshape(b, s, SB_HEADS, SB_DIM)
    k_pos = jnp.arange(s, dtype=jnp.int32)

    def block(i, qb):
        q_pos = i * Q_BLOCK + jnp.arange(Q_BLOCK, dtype=jnp.int32)
        z = jnp.einsum('bqhd,bkhd->bhqk', qb, k,
                       preferred_element_type=jnp.float32) * (SB_DIM ** -0.5)
        earlier = k_pos[None, :] < q_pos[:, None]
        log_beta = jax.nn.log_sigmoid(z)
        log_1m_beta = jnp.where(earlier, jax.nn.log_sigmoid(-z), 0.0)
        later = lax.cumsum(log_1m_beta, axis=3, reverse=True) - log_1m_beta
        a = jnp.where(earlier, jnp.exp(log_beta + later), 0.0)
        return jnp.einsum('bhqk,bkhd->bqhd', a.astype(v.dtype), v)

    o = over_query_blocks(block, q)
    return rms_norm(o, out_g).reshape(b, s, SB_HEADS * SB_DIM)


def latent_attention(c_q, c_kv, k_r, pos, cq_g, ckv_g, w_uq, w_ukv, q_g, k_g, out_g):
    b, s, _ = c_q.shape
    q = (rms_norm(c_q, cq_g) @ w_uq).reshape(b, s, MLA_HEADS, MLA_NOPE + MLA_ROPE)
    kv = (rms_norm(c_kv, ckv_g) @ w_ukv).reshape(b, s, MLA_HEADS, MLA_NOPE + MLA_V)
    k_nope, v = kv[..., :MLA_NOPE], kv[..., MLA_NOPE:]
    k = jnp.concatenate([k_nope, jnp.broadcast_to(k_r[:, :, None, :], (b, s, MLA_HEADS, MLA_ROPE))], axis=-1)
    q = rms_norm(q, q_g)
    k = rms_norm(k, k_g)
    q = jnp.concatenate([q[..., :MLA_NOPE], rope(q[..., MLA_NOPE:], pos)], axis=-1)
    k = jnp.concatenate([k[..., :MLA_NOPE], rope(k[..., MLA_NOPE:], pos)], axis=-1)
    scale = (MLA_NOPE + MLA_ROPE) ** -0.5
    k_pos = jnp.arange(s, dtype=jnp.int32)

    def block(i, qb):
        q_pos = i * Q_BLOCK + jnp.arange(Q_BLOCK, dtype=jnp.int32)
        logits = jnp.einsum('bqhd,bkhd->bhqk', qb, k, preferred_element_type=jnp.float32) * scale
        logits = jnp.where(chunk_mask(q_pos, k_pos), logits, -jnp.inf)
        p = jax.nn.softmax(logits, axis=-1)
        return jnp.einsum('bhqk,bkhd->bqhd', p.astype(v.dtype), v)

    o = over_query_blocks(block, q)
    return rms_norm(o, out_g).reshape(b, s, MLA_HEADS * MLA_V)


def memory_attention(h, mem, mem_g, w_q, w_kv, q_g, k_g, w_o):
    b, s, _ = h.shape
    n = mem.shape[1]
    m = rms_norm(mem, mem_g)
    q = rms_norm((h @ w_q).reshape(b, s, MEM_HEADS, MEM_DIM), q_g)
    kv = (m @ w_kv).reshape(b, n, 2, MEM_HEADS, MEM_DIM)
    k = rms_norm(kv[:, :, 0], k_g)
    v = kv[:, :, 1]
    logits = jnp.einsum('bqhd,bkhd->bhqk', q, k, preferred_element_type=jnp.float32) * (MEM_DIM ** -0.5)
    p = jax.nn.softmax(logits, axis=-1)
    o = jnp.einsum('bhqk,bkhd->bqhd', p.astype(v.dtype), v).reshape(b, s, MEM_HEADS * MEM_DIM)
    return o @ w_o


def setup_inputs(seed: int = 0) -> dict:
    key = jax.random.key(seed)
    keys = iter(jax.random.split(key, 40))

    def nrm(shape, scale):
        return jax.random.normal(next(keys), shape, jnp.float32) * scale

    def gain(shape):
        return 1.0 + nrm(shape, 0.02)

    L = DEPTH
    return {
        "x": nrm((BATCH, SEQ, D_MODEL), 1.0),
        "mem": nrm((BATCH, N_MEM, D_MODEL), 1.0),
        "rel_bias": nrm((NUM_BUCKETS, DA_HEADS * 2), 0.2),
        "mix_norm_g": gain((L, D_MODEL)),
        "w_in": nrm((L, D_MODEL, IN_COLS), D_MODEL ** -0.5),
        "da_q_norm_g": gain((L, DA_DIM)),
        "da_k_norm_g": gain((L, DA_DIM)),
        "da_lambda": nrm((L, 4, DA_DIM), 0.1),
        "da_subln_g": gain((L, DA_VDIM)),
        "sb_out_g": gain((L, SB_DIM)),
        "mla_cq_norm_g": gain((L, MLA_Q_RANK)),
        "mla_ckv_norm_g": gain((L, MLA_KV_RANK)),
        "w_mla_uq": nrm((L, MLA_Q_RANK, MLA_HEADS * (MLA_NOPE + MLA_ROPE)), MLA_Q_RANK ** -0.5),
        "w_mla_ukv": nrm((L, MLA_KV_RANK, MLA_HEADS * (MLA_NOPE + MLA_V)), MLA_KV_RANK ** -0.5),
        "mla_q_norm_g": gain((L, MLA_NOPE + MLA_ROPE)),
        "mla_k_norm_g": gain((L, MLA_NOPE + MLA_ROPE)),
        "mla_out_g": gain((L, MLA_V)),
        "w_out": nrm((L, MIX_WIDTH, D_MODEL), 0.5 * MIX_WIDTH ** -0.5),
        "memx_norm_g": gain((L, D_MODEL)),
        "mem_norm_g": gain((L, D_MODEL)),
        "w_mem_q": nrm((L, D_MODEL, MEM_HEADS * MEM_DIM), D_MODEL ** -0.5),
        "w_mem_kv": nrm((L, D_MODEL, 2 * MEM_HEADS * MEM_DIM), D_MODEL ** -0.5),
        "mem_q_norm_g": gain((L, MEM_DIM)),
        "mem_k_norm_g": gain((L, MEM_DIM)),
        "w_mem_o": nrm((L, MEM_HEADS * MEM_DIM, D_MODEL), 0.5 * (MEM_HEADS * MEM_DIM) ** -0.5),
        "ffn_norm_g": gain((L, D_MODEL)),
        "w_ff1": nrm((L, D_MODEL, D_FF), D_MODEL ** -0.5),
        "w_ff2": nrm((L, D_FF, D_MODEL), 0.5 * D_FF ** -0.5),
    }


def reference(x, mem, rel_bias, mix_norm_g, w_in, da_q_norm_g, da_k_norm_g, da_lambda, da_subln_g,
              sb_out_g, mla_cq_norm_g, mla_ckv_norm_g, w_mla_uq, w_mla_ukv, mla_q_norm_g, mla_k_norm_g,
              mla_out_g, w_out, memx_norm_g, mem_norm_g, w_mem_q, w_mem_kv, mem_q_norm_g, mem_k_norm_g,
              w_mem_o, ffn_norm_g, w_ff1, w_ff2):
    s = x.shape[1]
    pos = jnp.arange(s, dtype=jnp.int32)
    pts = split_points()
    for layer in range(DEPTH):
        h = rms_norm(x, mix_norm_g[layer])
        proj = h @ w_in[layer]
        da_q, da_k, da_v, sb_q, sb_k, sb_v, c_q, c_kv, k_r = jnp.split(proj, pts, axis=-1)
        y_a = diff_attention(da_q, da_k, da_v, rel_bias, da_q_norm_g[layer], da_k_norm_g[layer],
                             da_lambda[layer], da_subln_g[layer], layer)
        y_b = stick_breaking(sb_q, sb_k, sb_v, sb_out_g[layer])
        y_c = latent_attention(c_q, c_kv, k_r, pos, mla_cq_norm_g[layer], mla_ckv_norm_g[layer],
                               w_mla_uq[layer], w_mla_ukv[layer], mla_q_norm_g[layer],
                               mla_k_norm_g[layer], mla_out_g[layer])
        y = jnp.concatenate([y_a, y_b, y_c], axis=-1)
        x = x + y @ w_out[layer]
        h = rms_norm(x, memx_norm_g[layer])
        x = x + memory_attention(h, mem, mem_norm_g[layer], w_mem_q[layer], w_mem_kv[layer],
                                 mem_q_norm_g[layer], mem_k_norm_g[layer], w_mem_o[layer])
        h = rms_norm(x, ffn_norm_g[layer])
        x = x + jnp.square(jax.nn.relu(h @ w_ff1[layer])) @ w_ff2[layer]
    return x
```

```python
import functools
import math

import numpy as np
import jax
import jax.numpy as jnp
from jax import lax
from jax.experimental import pallas as pl
from jax.experimental.pallas import tpu as pltpu

_F32 = jnp.float32
_BF16 = jnp.bfloat16
_EPS = 1e-6

_CHUNK = 64
_DA_HEADS, _DA_DIM = 4, 64
_DA_VDIM = 2 * _DA_DIM
_SB_HEADS, _SB_DIM = 4, 64
_MLA_HEADS, _MLA_NOPE, _MLA_ROPE, _MLA_V = 4, 64, 32, 64
_MLA_QK = _MLA_NOPE + _MLA_ROPE
_MLA_Q_RANK, _MLA_KV_RANK = 256, 128
_ROPE_THETA = 10000.0
_NUM_BUCKETS, _MAX_DISTANCE = 32, 128
_MEM_HEADS, _MEM_DIM = 4, 64

_LANES = 128
_MXU = 256
_VMEM_LIMIT = 52 * 1024 * 1024

_ATT_TILE = 256
_ROW_TILE = 512

_IN_SIZES = (512, 512, 512, 256, 256, 256, _MLA_Q_RANK, _MLA_KV_RANK, _MLA_ROPE)
_IN_OFFS = tuple(int(v) for v in np.cumsum((0,) + _IN_SIZES))
_NT = (((1,), (1,)), ((), ()))


def _const_spec(shape):
    zeros = (0,) * len(shape)
    return pl.BlockSpec(shape, lambda *_: zeros, pipeline_mode=pl.Buffered(1))


def _params(*sem):
    return pltpu.CompilerParams(dimension_semantics=sem, vmem_limit_bytes=_VMEM_LIMIT)


def _rms(x, g):
    return x * lax.rsqrt(jnp.mean(x * x, axis=-1, keepdims=True) + _EPS) * g


def _group_mean_sq(y, gmat, group):
    sq = (y * y).astype(_BF16)
    cols = y.shape[1]
    parts = [jnp.dot(sq[:, c:c + _MXU], gmat, preferred_element_type=_F32)
             for c in range(0, cols, _MXU)]
    ss = parts[0] if len(parts) == 1 else jnp.concatenate(parts, axis=1)
    return ss * (1.0 / group)


def _inproj_kernel(x_ref, gmix_ref, w_ref, g64_ref, gq_ref, gk_ref, cqg_ref, ckvg_ref,
                   wuq_ref, wukv_ref, qg_ref, kg_ref, cos_ref, s1_ref, s2_ref,
                   daq_ref, dak_ref, dav_ref, sbq_ref, sbk_ref, sbv_ref,
                   mq_ref, mk_ref, mv_ref):
    x = x_ref[...]
    h = _rms(x, gmix_ref[...]).astype(_BF16)
    o = _IN_OFFS

    def proj(seg, width=None):
        hi = o[seg + 1] if width is None else o[seg] + width
        return jnp.dot(h, w_ref[:, o[seg]:hi], preferred_element_type=_F32)

    g64 = g64_ref[...]

    def norm64(y, g):
        return y * lax.rsqrt(_group_mean_sq(y, g64, _DA_DIM) + _EPS) * g

    daq_ref[...] = norm64(proj(0), gq_ref[...]).astype(_BF16)
    dak_ref[...] = norm64(proj(1), gk_ref[...]).astype(_BF16)
    dav_ref[...] = proj(2).astype(_BF16)
    sbq_ref[...] = (proj(3) * (_SB_DIM ** -0.5)).astype(_BF16)
    sbk_ref[...] = proj(4).astype(_BF16)
    sbv_ref[...] = proj(5).astype(_BF16)

    cos, s1, s2 = cos_ref[...], s1_ref[...], s2_ref[...]

    def head_norm_rope(y, g):
        ms = jnp.sum(y * y, axis=-1, keepdims=True) * (1.0 / _MLA_QK)
        yn = y * lax.rsqrt(ms + _EPS) * g
        half = _MLA_ROPE // 2
        return (yn * cos + pltpu.roll(yn, half, 1) * s1
                + pltpu.roll(yn, _LANES - half, 1) * s2)

    cq = _rms(proj(6), cqg_ref[...]).astype(_BF16)
    q_all = jnp.dot(cq, wuq_ref[...], preferred_element_type=_F32)
    ckv = _rms(proj(7), ckvg_ref[...]).astype(_BF16)
    kv_all = jnp.dot(ckv, wukv_ref[...], preferred_element_type=_F32)
    k_rope = proj(8, _LANES)
    for hd in range(_MLA_HEADS):
        sl = slice(hd * _LANES, (hd + 1) * _LANES)
        mq_ref[:, sl] = head_norm_rope(q_all[:, sl], qg_ref[:, sl]).astype(_BF16)
        mk_ref[:, sl] = head_norm_rope(kv_all[:, sl] + k_rope, kg_ref[:, sl]).astype(_BF16)
    mv_ref[...] = kv_all[:, _MLA_HEADS * _LANES:].astype(_BF16)


def _inproj(x2d, lp, seq):
    m, d = x2d.shape
    tm = _ROW_TILE
    pos_blocks = seq // tm
    row = lambda c: pl.BlockSpec((tm, c), lambda i: (i, 0))
    tab = pl.BlockSpec((tm, _LANES), lambda i: (i % pos_blocks, 0))
    widths = (512, 512, 512, 256, 256, 256, 512, 512, 256)
    return pl.pallas_call(
        _inproj_kernel,
        grid=(m // tm,),
        in_specs=[row(d), _const_spec((1, d)), _const_spec(lp["w_in"].shape),
                  _const_spec((_MXU, _MXU)), _const_spec((1, 512)), _const_spec((1, 512)),
                  _const_spec((1, _MLA_Q_RANK)), _const_spec((1, _MLA_KV_RANK)),
                  _const_spec(lp["w_uq"].shape), _const_spec(lp["w_ukv"].shape),
                  _const_spec((1, 512)), _const_spec((1, 512)), tab, tab, tab],
        out_specs=[row(c) for c in widths],
        out_shape=[jax.ShapeDtypeStruct((m, c), _BF16) for c in widths],
        compiler_params=_params("parallel"),
        name="inproj",
    )(x2d, lp["mix_g"], lp["w_in"], lp["g64"], lp["da_qg"], lp["da_kg"], lp["cq_g"],
      lp["ckv_g"], lp["w_uq"], lp["w_ukv"], lp["mla_qg"], lp["mla_kg"],
      lp["rope_cos"], lp["rope_s1"], lp["rope_s2"])


def _softmax_step(s, v, state):
    m_old, l_old, acc = state
    m_new = jnp.maximum(m_old, jnp.max(s, axis=-1, keepdims=True))
    alpha = jnp.exp(m_old - m_new)
    p = jnp.exp(s - m_new)
    l_new = alpha * l_old + jnp.sum(p, axis=-1, keepdims=True)
    acc = alpha * acc + jnp.dot(p.astype(_BF16), v, preferred_element_type=_F32)
    return m_new, l_new, acc


def _softmax_init(t, width):
    return (jnp.full((t, 1), -jnp.inf, _F32), jnp.zeros((t, 1), _F32),
            jnp.zeros((t, width), _F32))


def _rows(i, t):
    return pl.ds(pl.multiple_of(i * t, t), t)


def _half_lane_norm(o, first, g):
    sq = o * o
    lo = jnp.sum(jnp.where(first, sq, 0.0), axis=-1, keepdims=True)
    hi = jnp.sum(sq, axis=-1, keepdims=True) - lo
    ms = jnp.where(first, lo, hi) * (1.0 / 64)
    return o * lax.rsqrt(ms + _EPS) * g


def _da_kernel(lam_ref, q_ref, k_ref, v_ref, bias_ref, g_ref, o_ref, *, t):
    nq = q_ref.shape[0] // t
    lam = lam_ref[0]
    first = lax.broadcasted_iota(jnp.int32, (1, _LANES), 1) < _DA_DIM

    def q_tile(qi, carry):
        q = q_ref[_rows(qi, t), :]
        zero = jnp.zeros_like(q)
        qm = (jnp.where(first, q, zero), jnp.where(first, zero, q))

        def kv_step(ki, st):
            k = k_ref[_rows(ki, t), :]
            v = v_ref[_rows(ki, t), :]
            d = jnp.minimum(qi - ki, 2)
            out = []
            for mi in range(2):
                s = lax.dot_general(qm[mi], k, _NT, preferred_element_type=_F32) + bias_ref[d, mi]
                out.append(_softmax_step(s, v, st[mi]))
            return tuple(out)

        init = (_softmax_init(t, _DA_VDIM), _softmax_init(t, _DA_VDIM))
        (_, l0, a0), (_, l1, a1) = lax.fori_loop(0, qi + 1, kv_step, init)
        o = a0 / l0 - lam * (a1 / l1)
        o_ref[_rows(qi, t), :] = _rms(o, g_ref[...]).astype(o_ref.dtype)
        return carry

    lax.fori_loop(0, nq, q_tile, 0)


def _da_attention(q, k, v, bias, lam, g, batch, seq):
    t = _ATT_TILE
    blk = pl.BlockSpec((seq, _LANES), lambda hd, b: (b, hd))
    return pl.pallas_call(
        functools.partial(_da_kernel, t=t),
        grid=(_DA_HEADS, batch),
        in_specs=[pl.BlockSpec(memory_space=pltpu.SMEM), blk, blk, blk,
                  pl.BlockSpec((None, 3, 2, t, t), lambda hd, b: (hd, 0, 0, 0, 0)),
                  _const_spec((1, _LANES))],
        out_specs=blk,
        out_shape=jax.ShapeDtypeStruct(q.shape, _BF16),
        compiler_params=_params("parallel", "parallel"),
        name="diff_attention",
    )(lam, q, k, v, bias, g)


def _mla_kernel(q_ref, k_ref, v_ref, mask_ref, g_ref, o_ref, *, t):
    nq = q_ref.shape[0] // t
    first = lax.broadcasted_iota(jnp.int32, (1, _LANES), 1) < _MLA_V

    def q_tile(qi, carry):
        qs = [q_ref[_rows(qi, t), hh * _LANES:(hh + 1) * _LANES] for hh in range(2)]

        def kv_step(ki, st, mask):
            v = v_ref[_rows(ki, t), :]
            out = []
            for hh in range(2):
                k = k_ref[_rows(ki, t), hh * _LANES:(hh + 1) * _LANES]
                s = lax.dot_general(qs[hh], k, _NT, preferred_element_type=_F32)
                if mask is not None:
                    s = s + mask
                out.append(_softmax_step(s, v, st[hh]))
            return tuple(out)

        init = (_softmax_init(t, _LANES), _softmax_init(t, _LANES))
        st = lax.fori_loop(0, qi, lambda ki, st: kv_step(ki, st, None), init)
        (_, l0, a0), (_, l1, a1) = kv_step(qi, st, mask_ref[...])
        o = jnp.where(first, a0 / l0, a1 / l1)
        o_ref[_rows(qi, t), :] = _half_lane_norm(o, first, g_ref[...]).astype(o_ref.dtype)
        return carry

    lax.fori_loop(0, nq, q_tile, 0)


def _mla_attention(q, k, v, mask, g, batch, seq):
    t = _ATT_TILE
    pairs = _MLA_HEADS // 2
    qk = pl.BlockSpec((seq, 2 * _LANES), lambda p, b: (b, p))
    vo = pl.BlockSpec((seq, _LANES), lambda p, b: (b, p))
    return pl.pallas_call(
        functools.partial(_mla_kernel, t=t),
        grid=(pairs, batch),
        in_specs=[qk, qk, vo, _const_spec((t, t)), _const_spec((1, _LANES))],
        out_specs=vo,
        out_shape=jax.ShapeDtypeStruct(v.shape, _BF16),
        compiler_params=_params("parallel", "parallel"),
        name="latent_attention",
    )(q, k, v, mask, g)


def _sb_kernel(q_ref, k_ref, v_ref, g_ref, o_ref, *, t):
    nq = q_ref.shape[0] // t
    first = lax.broadcasted_iota(jnp.int32, (1, _LANES), 1) < _SB_DIM
    r_idx = lax.broadcasted_iota(jnp.int32, (t, t), 0)
    c_idx = lax.broadcasted_iota(jnp.int32, (t, t), 1)
    earlier = c_idx < r_idx
    tri = jnp.where(earlier, 1.0, 0.0).astype(_BF16)

    def q_tile(qi, carry):
        q = q_ref[_rows(qi, t), :]
        zero = jnp.zeros_like(q)
        qh = (jnp.where(first, q, zero), jnp.where(first, zero, q))

        def kv_step(ki, st, diag):
            k = k_ref[_rows(ki, t), :]
            v = v_ref[_rows(ki, t), :]
            out = []
            for hh in range(2):
                run, acc = st[hh]
                z = lax.dot_general(qh[hh], k, _NT, preferred_element_type=_F32)
                lp = jnp.log1p(jnp.exp(-jnp.abs(z)))
                log_beta = jnp.minimum(z, 0.0) - lp
                log_1m = -jnp.maximum(z, 0.0) - lp
                if diag:
                    log_1m = jnp.where(earlier, log_1m, 0.0)
                l_hi = log_1m.astype(_BF16)
                l_lo = (log_1m - l_hi.astype(_F32)).astype(_BF16)
                later = (jnp.dot(l_hi, tri, preferred_element_type=_F32)
                         + jnp.dot(l_lo, tri, preferred_element_type=_F32))
                a = jnp.exp(log_beta + later + run)
                if diag:
                    a = jnp.where(earlier, a, 0.0)
                acc = acc + jnp.dot(a.astype(_BF16), v, preferred_element_type=_F32)
                run = run + later[:, 0:1] + log_1m[:, 0:1]
                out.append((run, acc))
            return tuple(out)

        init = tuple((jnp.zeros((t, 1), _F32), jnp.zeros((t, _LANES), _F32)) for _ in range(2))
        st = kv_step(qi, init, True)
        st = lax.fori_loop(0, qi, lambda j, st: kv_step(qi - 1 - j, st, False), st)
        o = jnp.where(first, st[0][1], st[1][1])
        o_ref[_rows(qi, t), :] = _half_lane_norm(o, first, g_ref[...]).astype(o_ref.dtype)
        return carry

    lax.fori_loop(0, nq, q_tile, 0)


def _sb_attention(q, k, v, g, batch, seq):
    t = _ATT_TILE
    pairs = _SB_HEADS // 2
    blk = pl.BlockSpec((seq, _LANES), lambda p, b: (b, p))
    return pl.pallas_call(
        functools.partial(_sb_kernel, t=t),
        grid=(pairs, batch),
        in_specs=[blk, blk, blk, _const_spec((1, _LANES))],
        out_specs=blk,
        out_shape=jax.ShapeDtypeStruct(q.shape, _BF16),
        compiler_params=_params("parallel", "parallel"),
        name="stick_breaking",
    )(q, k, v, g)


def _memkv_kernel(mem_ref, g_ref, w_ref, g64_ref, kg_ref, k_ref, v_ref):
    m = _rms(mem_ref[...], g_ref[...]).astype(_BF16)
    kv = jnp.dot(m, w_ref[...], preferred_element_type=_F32)
    width = _MEM_HEADS * _MEM_DIM
    k = kv[:, :width]
    k = k * lax.rsqrt(_group_mean_sq(k, g64_ref[...], _MEM_DIM) + _EPS) * kg_ref[...]
    k_ref[...] = k.astype(_BF16)
    v_ref[...] = kv[:, width:].astype(_BF16)


def _memkv(mem2d, g, w, g64, kg):
    depth = w.shape[0]
    rows, d = mem2d.shape
    tm = _ROW_TILE
    width = _MEM_HEADS * _MEM_DIM
    out = pl.BlockSpec((None, tm, width), lambda l, i: (l, i, 0))
    return pl.pallas_call(
        _memkv_kernel,
        grid=(depth, rows // tm),
        in_specs=[pl.BlockSpec((tm, d), lambda l, i: (i, 0)),
                  pl.BlockSpec((None, 1, d), lambda l, i: (l, 0, 0)),
                  pl.BlockSpec((None, d, 2 * width), lambda l, i: (l, 0, 0)),
                  _const_spec((_MXU, _MXU)),
                  pl.BlockSpec((None, 1, width), lambda l, i: (l, 0, 0))],
        out_specs=[out, out],
        out_shape=[jax.ShapeDtypeStruct((depth, rows, width), _BF16)] * 2,
        compiler_params=_params("parallel", "parallel"),
        name="memory_kv",
    )(mem2d, g, w, g64, kg)


def _mix_mem_kernel(x_ref, ya_ref, yb_ref, yc_ref, wo_ref, gx_ref, wq_ref, g64_ref, qg_ref,
                    km_ref, vm_ref, wmo_ref, o_ref):
    wa = ya_ref.shape[1]
    wb = wa + yb_ref.shape[1]
    x = (x_ref[...]
         + jnp.dot(ya_ref[...], wo_ref[:wa, :], preferred_element_type=_F32)
         + jnp.dot(yb_ref[...], wo_ref[wa:wb, :], preferred_element_type=_F32)
         + jnp.dot(yc_ref[...], wo_ref[wb:, :], preferred_element_type=_F32))
    h = _rms(x, gx_ref[...]).astype(_BF16)
    q = jnp.dot(h, wq_ref[...], preferred_element_type=_F32)
    q = (q * lax.rsqrt(_group_mean_sq(q, g64_ref[...], _MEM_DIM) + _EPS) * qg_ref[...]).astype(_BF16)
    km = km_ref[...]
    vm = vm_ref[...]
    head_of_lane = lax.broadcasted_iota(jnp.int32, (1, q.shape[1]), 1) // _MEM_DIM
    zero = jnp.zeros_like(q)
    o = jnp.zeros(q.shape, _F32)
    for hd in range(_MEM_HEADS):
        sel = head_of_lane == hd
        s = lax.dot_general(jnp.where(sel, q, zero), km, _NT, preferred_element_type=_F32)
        p = jnp.exp(s - jnp.max(s, axis=-1, keepdims=True))
        oh = jnp.dot(p.astype(_BF16), vm, preferred_element_type=_F32)
        o = jnp.where(sel, oh / jnp.sum(p, axis=-1, keepdims=True), o)
    o_ref[...] = x + jnp.dot(o.astype(_BF16), wmo_ref[...], preferred_element_type=_F32)


def _mix_mem(x2d, ya, yb, yc, lp, km, vm, seq):
    m, d = x2d.shape
    tm = _ROW_TILE
    per_seq = seq // tm
    n_mem, width = km.shape[1], km.shape[2]
    row = lambda c: pl.BlockSpec((tm, c), lambda i: (i, 0))
    mem = pl.BlockSpec((None, n_mem, width), lambda i: (i // per_seq, 0, 0))
    return pl.pallas_call(
        _mix_mem_kernel,
        grid=(m // tm,),
        in_specs=[row(d), row(ya.shape[1]), row(yb.shape[1]), row(yc.shape[1]),
                  _const_spec(lp["w_out"].shape), _const_spec((1, d)),
                  _const_spec(lp["w_mem_q"].shape), _const_spec((_MXU, _MXU)),
                  _const_spec((1, width)), mem, mem, _const_spec(lp["w_mem_o"].shape)],
        out_specs=row(d),
        out_shape=jax.ShapeDtypeStruct((m, d), _F32),
        compiler_params=_params("parallel"),
        name="mix_and_memory",
    )(x2d, ya, yb, yc, lp["w_out"], lp["memx_g"], lp["w_mem_q"], lp["g64"], lp["mem_qg"],
      km, vm, lp["w_mem_o"])


_FF_CHUNK = 1024


def _ffn_kernel(x_ref, g_ref, w1_ref, w2_ref, o_ref):
    x = x_ref[...]
    h = _rms(x, g_ref[...]).astype(_BF16)
    acc = x
    for c in range(0, w1_ref.shape[1], _FF_CHUNK):
        u = jnp.dot(h, w1_ref[:, c:c + _FF_CHUNK], preferred_element_type=_F32)
        r = jnp.maximum(u, 0.0)
        acc = acc + jnp.dot((r * r).astype(_BF16), w2_ref[c:c + _FF_CHUNK, :],
                            preferred_element_type=_F32)
    o_ref[...] = acc


def _ffn(x2d, g, w1, w2):
    m, d = x2d.shape
    tm = _ROW_TILE
    row = pl.BlockSpec((tm, d), lambda i: (i, 0))
    return pl.pallas_call(
        _ffn_kernel,
        grid=(m // tm,),
        in_specs=[row, _const_spec((1, d)), _const_spec(w1.shape), _const_spec(w2.shape)],
        out_specs=row,
        out_shape=jax.ShapeDtypeStruct((m, d), _F32),
        compiler_params=_params("parallel"),
        name="ffn",
    )(x2d, g, w1, w2)


def _t5_bucket(rel):
    nb = _NUM_BUCKETS // 2
    bucket = (rel > 0).astype(jnp.int32) * nb
    n = jnp.abs(rel)
    max_exact = nb // 2
    is_small = n < max_exact
    large = max_exact + (jnp.log(jnp.maximum(n, 1).astype(jnp.float32) / max_exact)
                         / math.log(_MAX_DISTANCE / max_exact) * (nb - max_exact)).astype(jnp.int32)
    large = jnp.minimum(large, nb - 1)
    return bucket + jnp.where(is_small, n, large)


def _da_bias_tables(rel_bias, t):
    assert t + 1 >= _MAX_DISTANCE and t % _CHUNK == 0
    i = jnp.arange(t, dtype=jnp.int32)[:, None]
    j = jnp.arange(t, dtype=jnp.int32)[None, :]
    rb = rel_bias.astype(_F32)
    b0 = rb[_t5_bucket(j - i)]
    b0 = jnp.where(((j // _CHUNK) <= (i // _CHUNK))[:, :, None], b0, -jnp.inf)
    b1 = rb[_t5_bucket(j - i - t)]
    far = jnp.broadcast_to(rb[_t5_bucket(jnp.full((1, 1), -(t + 1), jnp.int32))], b1.shape)
    tab = jnp.stack([b0, b1, far]).reshape(3, t, t, _DA_HEADS, 2)
    return tab.transpose(3, 0, 4, 1, 2)


def _rope_tables(seq):
    half = _MLA_ROPE // 2
    freqs = _ROPE_THETA ** (-jnp.arange(half, dtype=jnp.float32) / half)
    ang = jnp.arange(seq, dtype=jnp.int32).astype(jnp.float32)[:, None] * freqs[None, :]
    cos, sin = jnp.cos(ang), jnp.sin(ang)
    ones = jnp.ones((seq, _MLA_NOPE), _F32)
    z = lambda w: jnp.zeros((seq, w), _F32)
    tail = _LANES - _MLA_QK
    c = jnp.concatenate([ones, cos, cos, z(tail)], axis=1)
    s1 = jnp.concatenate([z(_MLA_NOPE + half), sin, z(tail)], axis=1)
    s2 = jnp.concatenate([z(_MLA_NOPE), -sin, z(half + tail)], axis=1)
    return c, s1, s2


def _group_ones(group):
    idx = np.arange(_MXU) // group
    return jnp.asarray(idx[:, None] == idx[None, :], dtype=_BF16)


def _layer_params(l, p, rope, g64):
    d = p["w_in"].shape[1]
    o = _IN_OFFS
    w_in = p["w_in"][l]
    kr = jnp.zeros((d, _LANES), _F32).at[:, _MLA_NOPE:_MLA_QK].set(w_in[:, o[8]:o[9]])
    w_uq = jnp.pad(p["w_mla_uq"][l].reshape(_MLA_Q_RANK, _MLA_HEADS, _MLA_QK),
                   ((0, 0), (0, 0), (0, _LANES - _MLA_QK))).reshape(_MLA_Q_RANK, -1)
    w_ukv = p["w_mla_ukv"][l].reshape(_MLA_KV_RANK, _MLA_HEADS, _MLA_NOPE + _MLA_V)
    w_k = jnp.pad(w_ukv[:, :, :_MLA_NOPE], ((0, 0), (0, 0), (0, _LANES - _MLA_NOPE)))
    w_v = w_ukv[:, :, _MLA_NOPE:]
    pad_g = lambda g: jnp.tile(jnp.pad(g, (0, _LANES - _MLA_QK)), _MLA_HEADS)[None]
    lam_init = 0.8 - 0.6 * math.exp(-0.3 * l)
    lp = p["da_lambda"][l].astype(_F32)
    lam = jnp.exp(jnp.sum(lp[0] * lp[1])) - jnp.exp(jnp.sum(lp[2] * lp[3])) + lam_init
    return {
        "mix_g": p["mix_norm_g"][l][None],
        "w_in": jnp.concatenate([w_in[:, :o[8]], kr], axis=1).astype(_BF16),
        "g64": g64,
        "da_qg": jnp.tile(p["da_q_norm_g"][l], 2 * _DA_HEADS)[None] * (_DA_DIM ** -0.5),
        "da_kg": jnp.tile(p["da_k_norm_g"][l], 2 * _DA_HEADS)[None],
        "cq_g": p["mla_cq_norm_g"][l][None],
        "ckv_g": p["mla_ckv_norm_g"][l][None],
        "w_uq": w_uq.astype(_BF16),
        "w_ukv": jnp.concatenate([w_k.reshape(_MLA_KV_RANK, -1), w_v.reshape(_MLA_KV_RANK, -1)],
                                 axis=1).astype(_BF16),
        "mla_qg": pad_g(p["mla_q_norm_g"][l]) * (_MLA_QK ** -0.5),
        "mla_kg": pad_g(p["mla_k_norm_g"][l]),
        "rope_cos": rope[0], "rope_s1": rope[1], "rope_s2": rope[2],
        "lam": jnp.reshape(lam, (1,)).astype(_F32),
        "da_og": p["da_subln_g"][l][None] * (1.0 - lam_init),
        "sb_og": jnp.tile(p["sb_out_g"][l], 2)[None],
        "mla_og": jnp.tile(p["mla_out_g"][l], 2)[None],
        "w_out": p["w_out"][l].astype(_BF16),
        "memx_g": p["memx_norm_g"][l][None],
        "w_mem_q": p["w_mem_q"][l].astype(_BF16),
        "mem_qg": jnp.tile(p["mem_q_norm_g"][l], _MEM_HEADS)[None] * (_MEM_DIM ** -0.5),
        "w_mem_o": p["w_mem_o"][l].astype(_BF16),
        "ffn_g": p["ffn_norm_g"][l][None],
        "w_ff1": p["w_ff1"][l].astype(_BF16),
        "w_ff2": p["w_ff2"][l].astype(_BF16),
    }


def kernel(x, mem, rel_bias, mix_norm_g, w_in, da_q_norm_g, da_k_norm_g, da_lambda, da_subln_g,
           sb_out_g, mla_cq_norm_g, mla_ckv_norm_g, w_mla_uq, w_mla_ukv, mla_q_norm_g, mla_k_norm_g,
           mla_out_g, w_out, memx_norm_g, mem_norm_g, w_mem_q, w_mem_kv, mem_q_norm_g, mem_k_norm_g,
           w_mem_o, ffn_norm_g, w_ff1, w_ff2):
    p = dict(mix_norm_g=mix_norm_g, w_in=w_in, da_q_norm_g=da_q_norm_g, da_k_norm_g=da_k_norm_g,
             da_lambda=da_lambda, da_subln_g=da_subln_g, sb_out_g=sb_out_g,
             mla_cq_norm_g=mla_cq_norm_g, mla_ckv_norm_g=mla_ckv_norm_g, w_mla_uq=w_mla_uq,
             w_mla_ukv=w_mla_ukv, mla_q_norm_g=mla_q_norm_g, mla_k_norm_g=mla_k_norm_g,
             mla_out_g=mla_out_g, w_out=w_out, memx_norm_g=memx_norm_g, w_mem_q=w_mem_q,
             mem_q_norm_g=mem_q_norm_g, w_mem_o=w_mem_o, ffn_norm_g=ffn_norm_g, w_ff1=w_ff1,
             w_ff2=w_ff2)
    batch, seq, d = x.shape
    depth = w_in.shape[0]
    n_mem = mem.shape[1]
    t = _ATT_TILE
    assert seq % _ROW_TILE == 0 and seq % t == 0 and (batch * n_mem) % _ROW_TILE == 0
    assert w_in.shape[2] == _IN_OFFS[-1]

    g64 = _group_ones(_MEM_DIM)
    rope = _rope_tables(seq)
    bias = _da_bias_tables(rel_bias, t)
    i = jnp.arange(t, dtype=jnp.int32)
    chunk_mask = jnp.where((i[None, :] // _CHUNK) <= (i[:, None] // _CHUNK), 0.0, -jnp.inf).astype(_F32)

    width = _MEM_HEADS * _MEM_DIM
    km, vm = _memkv(mem.reshape(batch * n_mem, d), mem_norm_g[:, None, :], w_mem_kv.astype(_BF16), g64,
                    jnp.tile(mem_k_norm_g, (1, _MEM_HEADS))[:, None, :])
    km = km.reshape(depth, batch, n_mem, width)
    vm = vm.reshape(depth, batch, n_mem, width)

    x2d = x.reshape(batch * seq, d)
    for l in range(depth):
        lp = _layer_params(l, p, rope, g64)
        daq, dak, dav, sbq, sbk, sbv, mq, mk, mv = _inproj(x2d, lp, seq)
        ya = _da_attention(daq, dak, dav, bias, lp["lam"], lp["da_og"], batch, seq)
        yb = _sb_attention(sbq, sbk, sbv, lp["sb_og"], batch, seq)
        yc = _mla_attention(mq, mk, mv, chunk_mask, lp["mla_og"], batch, seq)
        x2d = _mix_mem(x2d, ya, yb, yc, lp, km[l], vm[l], seq)
        x2d = _ffn(x2d, lp["ffn_g"], lp["w_ff1"], lp["w_ff2"])
    return x2d.reshape(batch, seq, d)
```

```python
import functools
import math

import numpy as np
import jax
import jax.numpy as jnp
from jax import lax
from jax.experimental import pallas as pl
from jax.experimental.pallas import tpu as pltpu

_F32 = jnp.float32
_BF16 = jnp.bfloat16
_EPS = 1e-6

_CHUNK = 64
_DA_HEADS, _DA_DIM = 4, 64
_DA_VDIM = 2 * _DA_DIM
_SB_HEADS, _SB_DIM = 4, 64
_MLA_HEADS, _MLA_NOPE, _MLA_ROPE, _MLA_V = 4, 64, 32, 64
_MLA_QK = _MLA_NOPE + _MLA_ROPE
_MLA_Q_RANK, _MLA_KV_RANK = 256, 128
_ROPE_THETA = 10000.0
_NUM_BUCKETS, _MAX_DISTANCE = 32, 128
_MEM_HEADS, _MEM_DIM = 4, 64

_LANES = 128
_MXU = 256
_VMEM_LIMIT = 52 * 1024 * 1024

_ATT_TILE = 256
_ROW_TILE = 512

_IN_SIZES = (512, 512, 512, 256, 256, 256, _MLA_Q_RANK, _MLA_KV_RANK, _MLA_ROPE)
_IN_OFFS = tuple(int(v) for v in np.cumsum((0,) + _IN_SIZES))
_NT = (((1,), (1,)), ((), ()))
_LOG2E = math.log2(math.e)


def _const_spec(shape):
    zeros = (0,) * len(shape)
    return pl.BlockSpec(shape, lambda *_: zeros, pipeline_mode=pl.Buffered(1))


def _params(*sem):
    return pltpu.CompilerParams(dimension_semantics=sem, vmem_limit_bytes=_VMEM_LIMIT)


def _rms(x, g):
    return x * lax.rsqrt(jnp.mean(x * x, axis=-1, keepdims=True) + _EPS) * g


def _group_mean_sq(y, gmat, group):
    sq = (y * y).astype(_BF16)
    cols = y.shape[1]
    parts = [jnp.dot(sq[:, c:c + _MXU], gmat, preferred_element_type=_F32)
             for c in range(0, cols, _MXU)]
    ss = parts[0] if len(parts) == 1 else jnp.concatenate(parts, axis=1)
    return ss * (1.0 / group)


def _inproj_kernel(x_ref, gmix_ref, w_ref, g64_ref, gq_ref, gk_ref, cqg_ref, ckvg_ref,
                   wuq_ref, wukv_ref, qg_ref, kg_ref, cos_ref, s1_ref, s2_ref,
                   daq_ref, dak_ref, dav_ref, sbq_ref, sbk_ref, sbv_ref,
                   mq_ref, mk_ref, mv_ref):
    x = x_ref[...]
    h = _rms(x, gmix_ref[...]).astype(_BF16)
    o = _IN_OFFS

    def proj(seg, width=None):
        hi = o[seg + 1] if width is None else o[seg] + width
        return jnp.dot(h, w_ref[:, o[seg]:hi], preferred_element_type=_F32)

    g64 = g64_ref[...]
    t = daq_ref.shape[2]

    def norm64(y, g):
        return y * lax.rsqrt(_group_mean_sq(y, g64, _DA_DIM) + _EPS) * g

    def store_t(ref, y, row0=0):
        for r in range(y.shape[0] // t):
            ref[r, row0:row0 + y.shape[1], :] = y[r * t:(r + 1) * t, :].T.astype(_BF16)

    store_t(daq_ref, norm64(proj(0), gq_ref[...]))
    dak_ref[...] = norm64(proj(1), gk_ref[...]).astype(_BF16)
    store_t(dav_ref, proj(2))
    store_t(sbq_ref, proj(3) * (_SB_DIM ** -0.5))
    sbk_ref[...] = proj(4).astype(_BF16)
    store_t(sbv_ref, proj(5))

    cos, s1, s2 = cos_ref[...], s1_ref[...], s2_ref[...]

    def head_norm_rope(y, g):
        ms = jnp.sum(y * y, axis=-1, keepdims=True) * (1.0 / _MLA_QK)
        yn = y * lax.rsqrt(ms + _EPS) * g
        half = _MLA_ROPE // 2
        return (yn * cos + pltpu.roll(yn, half, 1) * s1
                + pltpu.roll(yn, _LANES - half, 1) * s2)

    cq = _rms(proj(6), cqg_ref[...]).astype(_BF16)
    q_all = jnp.dot(cq, wuq_ref[...], preferred_element_type=_F32)
    ckv = _rms(proj(7), ckvg_ref[...]).astype(_BF16)
    kv_all = jnp.dot(ckv, wukv_ref[...], preferred_element_type=_F32)
    k_rope = proj(8, _LANES)
    for hd in range(_MLA_HEADS):
        sl = slice(hd * _LANES, (hd + 1) * _LANES)
        store_t(mq_ref, head_norm_rope(q_all[:, sl], qg_ref[:, sl]), hd * _LANES)
        mk_ref[:, sl] = head_norm_rope(kv_all[:, sl] + k_rope, kg_ref[:, sl]).astype(_BF16)
    store_t(mv_ref, kv_all[:, _MLA_HEADS * _LANES:])


def _inproj(x2d, lp, seq):
    m, d = x2d.shape
    tm = _ROW_TILE
    t = _ATT_TILE
    pos_blocks = seq // tm
    tab = pl.BlockSpec((tm, _LANES), lambda i: (i % pos_blocks, 0))
    widths = (512, 512, 512, 256, 256, 256, 512, 512, 256)
    transposed = (True, False, True, True, False, True, True, False, True)
    out_specs, out_shape = [], []
    for c, tr in zip(widths, transposed):
        if tr:
            out_specs.append(pl.BlockSpec((tm // t, c, t), lambda i: (i, 0, 0)))
            out_shape.append(jax.ShapeDtypeStruct((m // t, c, t), _BF16))
        else:
            out_specs.append(pl.BlockSpec((tm, c), lambda i: (i, 0)))
            out_shape.append(jax.ShapeDtypeStruct((m, c), _BF16))
    row = lambda c: pl.BlockSpec((tm, c), lambda i: (i, 0))
    return pl.pallas_call(
        _inproj_kernel,
        grid=(m // tm,),
        in_specs=[row(d), _const_spec((1, d)), _const_spec(lp["w_in"].shape),
                  _const_spec((_MXU, _MXU)), _const_spec((1, 512)), _const_spec((1, 512)),
                  _const_spec((1, _MLA_Q_RANK)), _const_spec((1, _MLA_KV_RANK)),
                  _const_spec(lp["w_uq"].shape), _const_spec(lp["w_ukv"].shape),
                  _const_spec((1, 512)), _const_spec((1, 512)), tab, tab, tab],
        out_specs=out_specs,
        out_shape=out_shape,
        compiler_params=_params("parallel"),
        name="inproj",
    )(x2d, lp["mix_g"], lp["w_in"], lp["g64"], lp["da_qg"], lp["da_kg"], lp["cq_g"],
      lp["ckv_g"], lp["w_uq"], lp["w_ukv"], lp["mla_qg"], lp["mla_kg"],
      lp["rope_cos"], lp["rope_s1"], lp["rope_s2"])


def _softmax_steps(scores, values, states):
    m_new = [jnp.maximum(st[0], jnp.max(s, axis=0, keepdims=True)) for s, st in zip(scores, states)]
    p = [jnp.exp2(s - m) for s, m in zip(scores, m_new)]
    pv = [jnp.dot(v, pi.astype(_BF16), preferred_element_type=_F32) for v, pi in zip(values, p)]
    out = []
    for (m_old, l_old, acc), m, pi, pvi in zip(states, m_new, p, pv):
        alpha = jnp.exp2(m_old - m)
        out.append((m, alpha * l_old + jnp.sum(pi, axis=0, keepdims=True), alpha * acc + pvi))
    return tuple(out)


def _softmax_init(t, width):
    return (jnp.full((1, t), -jnp.inf, _F32), jnp.zeros((1, t), _F32),
            jnp.zeros((width, t), _F32))


def _rows(i, t):
    return pl.ds(pl.multiple_of(i * t, t), t)


def _split_rows(x_t, half):
    zero = jnp.zeros((half, x_t.shape[1]), x_t.dtype)
    return (jnp.concatenate([x_t[:half], zero], axis=0),
            jnp.concatenate([zero, x_t[half:]], axis=0))


def _half_row_norm(o_t, half, g_t):
    sq = o_t * o_t
    lo = lax.rsqrt(jnp.mean(sq[:half], axis=0, keepdims=True) + _EPS)
    hi = lax.rsqrt(jnp.mean(sq[half:], axis=0, keepdims=True) + _EPS)
    return jnp.concatenate([o_t[:half] * lo, o_t[half:] * hi], axis=0) * g_t


def _da_kernel(lam_ref, q_ref, k_ref, v_ref, bias_ref, g_ref, o_ref, *, t):
    nq = q_ref.shape[0]
    lam = lam_ref[0]

    heads = range(_DA_HEADS)
    head = lambda hd: slice(hd * _LANES, (hd + 1) * _LANES)

    def q_tile(qi, carry):
        qm = [qh for hd in heads for qh in _split_rows(q_ref[qi, head(hd), :], _DA_DIM)]

        def kv_step(ki, st):
            d = jnp.minimum(qi - ki, 2)
            scores = [jnp.dot(k_ref[_rows(ki, t), head(hd)], qm[2 * hd + mi],
                              preferred_element_type=_F32) + bias_ref[hd, d, mi]
                      for hd in heads for mi in range(2)]
            values = [v_ref[ki, head(hd), :] for hd in heads for _ in range(2)]
            return _softmax_steps(scores, values, st)

        init = tuple(_softmax_init(t, _DA_VDIM) for _ in range(2 * _DA_HEADS))
        st = lax.fori_loop(0, qi + 1, kv_step, init)
        for hd in heads:
            (_, l0, a0), (_, l1, a1) = st[2 * hd], st[2 * hd + 1]
            o_t = a0 * (1.0 / l0) - lam * (a1 * (1.0 / l1))
            o_t = o_t * lax.rsqrt(jnp.mean(o_t * o_t, axis=0, keepdims=True) + _EPS) * g_ref[...]
            o_ref[_rows(qi, t), head(hd)] = o_t.T.astype(o_ref.dtype)
        return carry

    lax.fori_loop(0, nq, q_tile, 0)


def _da_attention(q_t, k, v_t, bias, lam, g_t, batch, seq):
    t = _ATT_TILE
    nq = seq // t
    width = _DA_HEADS * _LANES
    tiles = pl.BlockSpec((nq, width, t), lambda b: (b, 0, 0))
    rows = pl.BlockSpec((seq, width), lambda b: (b, 0))
    return pl.pallas_call(
        functools.partial(_da_kernel, t=t),
        grid=(batch,),
        in_specs=[pl.BlockSpec(memory_space=pltpu.SMEM), tiles, rows, tiles,
                  _const_spec(bias.shape), _const_spec((_LANES, t))],
        out_specs=rows,
        out_shape=jax.ShapeDtypeStruct(k.shape, _BF16),
        compiler_params=_params("parallel"),
        name="diff_attention",
    )(lam, q_t, k, v_t, bias, g_t)


def _mla_kernel(q_ref, k_ref, v_ref, mask_ref, g_ref, o_ref, *, t):
    nq = q_ref.shape[0]

    heads = range(_MLA_HEADS)
    head = lambda hd: slice(hd * _LANES, (hd + 1) * _LANES)
    pair = lambda hd: slice((hd // 2) * _LANES, (hd // 2 + 1) * _LANES)

    def q_tile(qi, carry):
        qs = [q_ref[qi, head(hd), :] for hd in heads]

        def kv_step(ki, st, mask):
            scores = [jnp.dot(k_ref[_rows(ki, t), head(hd)], qs[hd], preferred_element_type=_F32)
                      for hd in heads]
            if mask is not None:
                scores = [s + mask for s in scores]
            values = [v_ref[ki, pair(hd), :] for hd in heads]
            return _softmax_steps(scores, values, st)

        init = tuple(_softmax_init(t, _LANES) for _ in heads)
        st = lax.fori_loop(0, qi, lambda ki, st: kv_step(ki, st, None), init)
        st = kv_step(qi, st, mask_ref[...])
        for p in range(_MLA_HEADS // 2):
            (_, l0, a0), (_, l1, a1) = st[2 * p], st[2 * p + 1]
            o_t = jnp.concatenate([a0[:_MLA_V] * (1.0 / l0), a1[_MLA_V:] * (1.0 / l1)], axis=0)
            o_ref[_rows(qi, t), head(p)] = _half_row_norm(o_t, _MLA_V, g_ref[...]).T.astype(o_ref.dtype)
        return carry

    lax.fori_loop(0, nq, q_tile, 0)


def _mla_attention(q_t, k, v_t, mask_t, g_t, batch, seq):
    t = _ATT_TILE
    nq = seq // t
    qk_width = _MLA_HEADS * _LANES
    v_width = _MLA_HEADS * _MLA_V
    return pl.pallas_call(
        functools.partial(_mla_kernel, t=t),
        grid=(batch,),
        in_specs=[pl.BlockSpec((nq, qk_width, t), lambda b: (b, 0, 0)),
                  pl.BlockSpec((seq, qk_width), lambda b: (b, 0)),
                  pl.BlockSpec((nq, v_width, t), lambda b: (b, 0, 0)),
                  _const_spec((t, t)), _const_spec((_LANES, t))],
        out_specs=pl.BlockSpec((seq, v_width), lambda b: (b, 0)),
        out_shape=jax.ShapeDtypeStruct((batch * seq, v_width), _BF16),
        compiler_params=_params("parallel"),
        name="latent_attention",
    )(q_t, k, v_t, mask_t, g_t)


def _sb_kernel(q_ref, k_ref, v_ref, g_ref, o_ref, *, t):
    nq = q_ref.shape[0]
    key_idx = lax.broadcasted_iota(jnp.int32, (t, t), 0)
    query_idx = lax.broadcasted_iota(jnp.int32, (t, t), 1)
    earlier = key_idx < query_idx
    tri = jnp.where(earlier, 1.0, 0.0).astype(_BF16)

    heads = range(_SB_HEADS)
    pair = lambda hd: slice((hd // 2) * _LANES, (hd // 2 + 1) * _LANES)
    dot = functools.partial(jnp.dot, preferred_element_type=_F32)

    def q_tile(qi, carry):
        qh = [q for p in range(_SB_HEADS // 2) for q in _split_rows(q_ref[qi, pair(2 * p), :], _SB_DIM)]

        def kv_step(ki, st, diag):
            z = [dot(k_ref[_rows(ki, t), pair(hd)], qh[hd]) for hd in heads]
            lp = [jnp.log1p(jnp.exp(-jnp.abs(zi))) for zi in z]
            log_beta = [jnp.minimum(zi, 0.0) - li for zi, li in zip(z, lp)]
            log_1m = [-jnp.maximum(zi, 0.0) - li for zi, li in zip(z, lp)]
            if diag:
                log_1m = [jnp.where(earlier, li, 0.0) for li in log_1m]
            l_hi = [li.astype(_BF16) for li in log_1m]
            l_lo = [(li - hi.astype(_F32)).astype(_BF16) for li, hi in zip(log_1m, l_hi)]
            later = [dot(tri, hi) + dot(tri, lo) for hi, lo in zip(l_hi, l_lo)]
            a = [jnp.exp(lb + la + run) for lb, la, (run, _) in zip(log_beta, later, st)]
            if diag:
                a = [jnp.where(earlier, ai, 0.0) for ai in a]
            pv = [dot(v_ref[ki, pair(hd), :], a[hd].astype(_BF16)) for hd in heads]
            return tuple((run + la[0:1, :] + li[0:1, :], acc + pvi)
                         for (run, acc), la, li, pvi in zip(st, later, log_1m, pv))

        init = tuple((jnp.zeros((1, t), _F32), jnp.zeros((_LANES, t), _F32)) for _ in heads)
        st = kv_step(qi, init, True)
        st = lax.fori_loop(0, qi, lambda j, st: kv_step(qi - 1 - j, st, False), st)
        for p in range(_SB_HEADS // 2):
            o_t = jnp.concatenate([st[2 * p][1][:_SB_DIM], st[2 * p + 1][1][_SB_DIM:]], axis=0)
            o_ref[_rows(qi, t), pair(2 * p)] = _half_row_norm(o_t, _SB_DIM, g_ref[...]).T.astype(o_ref.dtype)
        return carry

    lax.fori_loop(0, nq, q_tile, 0)


def _sb_attention(q_t, k, v_t, g_t, batch, seq):
    t = _ATT_TILE
    nq = seq // t
    width = _SB_HEADS * _SB_DIM
    tiles = pl.BlockSpec((nq, width, t), lambda b: (b, 0, 0))
    rows = pl.BlockSpec((seq, width), lambda b: (b, 0))
    return pl.pallas_call(
        functools.partial(_sb_kernel, t=t),
        grid=(batch,),
        in_specs=[tiles, rows, tiles, _const_spec((_LANES, t))],
        out_specs=rows,
        out_shape=jax.ShapeDtypeStruct(k.shape, _BF16),
        compiler_params=_params("parallel"),
        name="stick_breaking",
    )(q_t, k, v_t, g_t)


def _memkv_kernel(mem_ref, g_ref, w_ref, g64_ref, kg_ref, k_ref, v_ref):
    m = _rms(mem_ref[...], g_ref[...]).astype(_BF16)
    kv = jnp.dot(m, w_ref[...], preferred_element_type=_F32)
    width = _MEM_HEADS * _MEM_DIM
    k = kv[:, :width]
    k = k * lax.rsqrt(_group_mean_sq(k, g64_ref[...], _MEM_DIM) + _EPS) * kg_ref[...]
    k_ref[...] = k.astype(_BF16)
    v_ref[...] = kv[:, width:].astype(_BF16)


def _memkv(mem2d, g, w, g64, kg):
    depth = w.shape[0]
    rows, d = mem2d.shape
    tm = _ROW_TILE
    width = _MEM_HEADS * _MEM_DIM
    out = pl.BlockSpec((None, tm, width), lambda l, i: (l, i, 0))
    return pl.pallas_call(
        _memkv_kernel,
        grid=(depth, rows // tm),
        in_specs=[pl.BlockSpec((tm, d), lambda l, i: (i, 0)),
                  pl.BlockSpec((None, 1, d), lambda l, i: (l, 0, 0)),
                  pl.BlockSpec((None, d, 2 * width), lambda l, i: (l, 0, 0)),
                  _const_spec((_MXU, _MXU)),
                  pl.BlockSpec((None, 1, width), lambda l, i: (l, 0, 0))],
        out_specs=[out, out],
        out_shape=[jax.ShapeDtypeStruct((depth, rows, width), _BF16)] * 2,
        compiler_params=_params("parallel", "parallel"),
        name="memory_kv",
    )(mem2d, g, w, g64, kg)


def _mix_mem_kernel(x_ref, ya_ref, yb_ref, yc_ref, wo_ref, gx_ref, wq_ref, g64_ref, qg_ref,
                    km_ref, vm_ref, wmo_ref, o_ref):
    wa = ya_ref.shape[1]
    wb = wa + yb_ref.shape[1]
    x = (x_ref[...]
         + jnp.dot(ya_ref[...], wo_ref[:wa, :], preferred_element_type=_F32)
         + jnp.dot(yb_ref[...], wo_ref[wa:wb, :], preferred_element_type=_F32)
         + jnp.dot(yc_ref[...], wo_ref[wb:, :], preferred_element_type=_F32))
    h = _rms(x, gx_ref[...]).astype(_BF16)
    q = jnp.dot(h, wq_ref[...], preferred_element_type=_F32)
    q = (q * lax.rsqrt(_group_mean_sq(q, g64_ref[...], _MEM_DIM) + _EPS) * qg_ref[...]).astype(_BF16)
    km = km_ref[...]
    vm = vm_ref[...]
    head_of_lane = lax.broadcasted_iota(jnp.int32, (1, q.shape[1]), 1) // _MEM_DIM
    zero = jnp.zeros_like(q)
    o = jnp.zeros(q.shape, _F32)
    for hd in range(_MEM_HEADS):
        sel = head_of_lane == hd
        s = lax.dot_general(jnp.where(sel, q, zero), km, _NT, preferred_element_type=_F32)
        p = jnp.exp(s - jnp.max(s, axis=-1, keepdims=True))
        oh = jnp.dot(p.astype(_BF16), vm, preferred_element_type=_F32)
        o = jnp.where(sel, oh / jnp.sum(p, axis=-1, keepdims=True), o)
    o_ref[...] = x + jnp.dot(o.astype(_BF16), wmo_ref[...], preferred_element_type=_F32)


def _mix_mem(x2d, ya, yb, yc, lp, km, vm, seq):
    m, d = x2d.shape
    tm = _ROW_TILE
    per_seq = seq // tm
    n_mem, width = km.shape[1], km.shape[2]
    row = lambda c: pl.BlockSpec((tm, c), lambda i: (i, 0))
    mem = pl.BlockSpec((None, n_mem, width), lambda i: (i // per_seq, 0, 0))
    return pl.pallas_call(
        _mix_mem_kernel,
        grid=(m // tm,),
        in_specs=[row(d), row(ya.shape[1]), row(yb.shape[1]), row(yc.shape[1]),
                  _const_spec(lp["w_out"].shape), _const_spec((1, d)),
                  _const_spec(lp["w_mem_q"].shape), _const_spec((_MXU, _MXU)),
                  _const_spec((1, width)), mem, mem, _const_spec(lp["w_mem_o"].shape)],
        out_specs=row(d),
        out_shape=jax.ShapeDtypeStruct((m, d), _F32),
        compiler_params=_params("parallel"),
        name="mix_and_memory",
    )(x2d, ya, yb, yc, lp["w_out"], lp["memx_g"], lp["w_mem_q"], lp["g64"], lp["mem_qg"],
      km, vm, lp["w_mem_o"])


_FF_CHUNK = 1024


def _ffn_kernel(x_ref, g_ref, w1_ref, w2_ref, o_ref):
    x = x_ref[...]
    h = _rms(x, g_ref[...]).astype(_BF16)
    acc = x
    for c in range(0, w1_ref.shape[1], _FF_CHUNK):
        u = jnp.dot(h, w1_ref[:, c:c + _FF_CHUNK], preferred_element_type=_F32)
        r = jnp.maximum(u, 0.0)
        acc = acc + jnp.dot((r * r).astype(_BF16), w2_ref[c:c + _FF_CHUNK, :],
                            preferred_element_type=_F32)
    o_ref[...] = acc


def _ffn(x2d, g, w1, w2):
    m, d = x2d.shape
    tm = _ROW_TILE
    row = pl.BlockSpec((tm, d), lambda i: (i, 0))
    return pl.pallas_call(
        _ffn_kernel,
        grid=(m // tm,),
        in_specs=[row, _const_spec((1, d)), _const_spec(w1.shape), _const_spec(w2.shape)],
        out_specs=row,
        out_shape=jax.ShapeDtypeStruct((m, d), _F32),
        compiler_params=_params("parallel"),
        name="ffn",
    )(x2d, g, w1, w2)


def _t5_bucket(rel):
    nb = _NUM_BUCKETS // 2
    bucket = (rel > 0).astype(jnp.int32) * nb
    n = jnp.abs(rel)
    max_exact = nb // 2
    is_small = n < max_exact
    large = max_exact + (jnp.log(jnp.maximum(n, 1).astype(jnp.float32) / max_exact)
                         / math.log(_MAX_DISTANCE / max_exact) * (nb - max_exact)).astype(jnp.int32)
    large = jnp.minimum(large, nb - 1)
    return bucket + jnp.where(is_small, n, large)


def _da_bias_tables(rel_bias, t):
    assert t + 1 >= _MAX_DISTANCE and t % _CHUNK == 0
    j = jnp.arange(t, dtype=jnp.int32)[:, None]
    i = jnp.arange(t, dtype=jnp.int32)[None, :]
    rb = rel_bias.astype(_F32) * _LOG2E
    b0 = rb[_t5_bucket(j - i)]
    b0 = jnp.where(((j // _CHUNK) <= (i // _CHUNK))[:, :, None], b0, -jnp.inf)
    b1 = rb[_t5_bucket(j - i - t)]
    far = jnp.broadcast_to(rb[_t5_bucket(jnp.full((1, 1), -(t + 1), jnp.int32))], b1.shape)
    tab = jnp.stack([b0, b1, far]).reshape(3, t, t, _DA_HEADS, 2)
    return tab.transpose(3, 0, 4, 1, 2)


def _rope_tables(seq):
    half = _MLA_ROPE // 2
    freqs = _ROPE_THETA ** (-jnp.arange(half, dtype=jnp.float32) / half)
    ang = jnp.arange(seq, dtype=jnp.int32).astype(jnp.float32)[:, None] * freqs[None, :]
    cos, sin = jnp.cos(ang), jnp.sin(ang)
    ones = jnp.ones((seq, _MLA_NOPE), _F32)
    z = lambda w: jnp.zeros((seq, w), _F32)
    tail = _LANES - _MLA_QK
    c = jnp.concatenate([ones, cos, cos, z(tail)], axis=1)
    s1 = jnp.concatenate([z(_MLA_NOPE + half), sin, z(tail)], axis=1)
    s2 = jnp.concatenate([z(_MLA_NOPE), -sin, z(half + tail)], axis=1)
    return c, s1, s2


def _group_ones(group):
    idx = np.arange(_MXU) // group
    return jnp.asarray(idx[:, None] == idx[None, :], dtype=_BF16)


def _layer_params(l, p, rope, g64):
    d = p["w_in"].shape[1]
    o = _IN_OFFS
    w_in = p["w_in"][l]
    kr = jnp.zeros((d, _LANES), _F32).at[:, _MLA_NOPE:_MLA_QK].set(w_in[:, o[8]:o[9]])
    w_uq = jnp.pad(p["w_mla_uq"][l].reshape(_MLA_Q_RANK, _MLA_HEADS, _MLA_QK),
                   ((0, 0), (0, 0), (0, _LANES - _MLA_QK))).reshape(_MLA_Q_RANK, -1)
    w_ukv = p["w_mla_ukv"][l].reshape(_MLA_KV_RANK, _MLA_HEADS, _MLA_NOPE + _MLA_V)
    w_k = jnp.pad(w_ukv[:, :, :_MLA_NOPE], ((0, 0), (0, 0), (0, _LANES - _MLA_NOPE)))
    w_v = w_ukv[:, :, _MLA_NOPE:]
    pad_g = lambda g: jnp.tile(jnp.pad(g, (0, _LANES - _MLA_QK)), _MLA_HEADS)[None]
    col = lambda g: jnp.broadcast_to(g[:, None], (g.shape[0], _ATT_TILE))
    lam_init = 0.8 - 0.6 * math.exp(-0.3 * l)
    lp = p["da_lambda"][l].astype(_F32)
    lam = jnp.exp(jnp.sum(lp[0] * lp[1])) - jnp.exp(jnp.sum(lp[2] * lp[3])) + lam_init
    return {
        "mix_g": p["mix_norm_g"][l][None],
        "w_in": jnp.concatenate([w_in[:, :o[8]], kr], axis=1).astype(_BF16),
        "g64": g64,
        "da_qg": jnp.tile(p["da_q_norm_g"][l], 2 * _DA_HEADS)[None] * (_DA_DIM ** -0.5 * _LOG2E),
        "da_kg": jnp.tile(p["da_k_norm_g"][l], 2 * _DA_HEADS)[None],
        "cq_g": p["mla_cq_norm_g"][l][None],
        "ckv_g": p["mla_ckv_norm_g"][l][None],
        "w_uq": w_uq.astype(_BF16),
        "w_ukv": jnp.concatenate([w_k.reshape(_MLA_KV_RANK, -1), w_v.reshape(_MLA_KV_RANK, -1)],
                                 axis=1).astype(_BF16),
        "mla_qg": pad_g(p["mla_q_norm_g"][l]) * (_MLA_QK ** -0.5 * _LOG2E),
        "mla_kg": pad_g(p["mla_k_norm_g"][l]),
        "rope_cos": rope[0], "rope_s1": rope[1], "rope_s2": rope[2],
        "lam": jnp.reshape(lam, (1,)).astype(_F32),
        "da_og": col(p["da_subln_g"][l] * (1.0 - lam_init)),
        "sb_og": col(jnp.tile(p["sb_out_g"][l], 2)),
        "mla_og": col(jnp.tile(p["mla_out_g"][l], 2)),
        "w_out": p["w_out"][l].astype(_BF16),
        "memx_g": p["memx_norm_g"][l][None],
        "w_mem_q": p["w_mem_q"][l].astype(_BF16),
        "mem_qg": jnp.tile(p["mem_q_norm_g"][l], _MEM_HEADS)[None] * (_MEM_DIM ** -0.5),
        "w_mem_o": p["w_mem_o"][l].astype(_BF16),
        "ffn_g": p["ffn_norm_g"][l][None],
        "w_ff1": p["w_ff1"][l].astype(_BF16),
        "w_ff2": p["w_ff2"][l].astype(_BF16),
    }


def kernel(x, mem, rel_bias, mix_norm_g, w_in, da_q_norm_g, da_k_norm_g, da_lambda, da_subln_g,
           sb_out_g, mla_cq_norm_g, mla_ckv_norm_g, w_mla_uq, w_mla_ukv, mla_q_norm_g, mla_k_norm_g,
           mla_out_g, w_out, memx_norm_g, mem_norm_g, w_mem_q, w_mem_kv, mem_q_norm_g, mem_k_norm_g,
           w_mem_o, ffn_norm_g, w_ff1, w_ff2):
    p = dict(mix_norm_g=mix_norm_g, w_in=w_in, da_q_norm_g=da_q_norm_g, da_k_norm_g=da_k_norm_g,
             da_lambda=da_lambda, da_subln_g=da_subln_g, sb_out_g=sb_out_g,
             mla_cq_norm_g=mla_cq_norm_g, mla_ckv_norm_g=mla_ckv_norm_g, w_mla_uq=w_mla_uq,
             w_mla_ukv=w_mla_ukv, mla_q_norm_g=mla_q_norm_g, mla_k_norm_g=mla_k_norm_g,
             mla_out_g=mla_out_g, w_out=w_out, memx_norm_g=memx_norm_g, w_mem_q=w_mem_q,
             mem_q_norm_g=mem_q_norm_g, w_mem_o=w_mem_o, ffn_norm_g=ffn_norm_g, w_ff1=w_ff1,
             w_ff2=w_ff2)
    batch, seq, d = x.shape
    depth = w_in.shape[0]
    n_mem = mem.shape[1]
    t = _ATT_TILE
    assert seq % _ROW_TILE == 0 and seq % t == 0 and (batch * n_mem) % _ROW_TILE == 0
    assert w_in.shape[2] == _IN_OFFS[-1]

    g64 = _group_ones(_MEM_DIM)
    rope = _rope_tables(seq)
    bias = _da_bias_tables(rel_bias, t)
    i = jnp.arange(t, dtype=jnp.int32)
    chunk_mask = jnp.where((i[:, None] // _CHUNK) <= (i[None, :] // _CHUNK), 0.0, -jnp.inf).astype(_F32)

    width = _MEM_HEADS * _MEM_DIM
    km, vm = _memkv(mem.reshape(batch * n_mem, d), mem_norm_g[:, None, :], w_mem_kv.astype(_BF16), g64,
                    jnp.tile(mem_k_norm_g, (1, _MEM_HEADS))[:, None, :])
    km = km.reshape(depth, batch, n_mem, width)
    vm = vm.reshape(depth, batch, n_mem, width)

    x2d = x.reshape(batch * seq, d)
    for l in range(depth):
        lp = _layer_params(l, p, rope, g64)
        daq, dak, dav, sbq, sbk, sbv, mq, mk, mv = _inproj(x2d, lp, seq)
        ya = _da_attention(daq, dak, dav, bias, lp["lam"], lp["da_og"], batch, seq)
        yb = _sb_attention(sbq, sbk, sbv, lp["sb_og"], batch, seq)
        yc = _mla_attention(mq, mk, mv, chunk_mask, lp["mla_og"], batch, seq)
        x2d = _mix_mem(x2d, ya, yb, yc, lp, km[l], vm[l], seq)
        x2d = _ffn(x2d, lp["ffn_g"], lp["w_ff1"], lp["w_ff2"])
    return x2d.reshape(batch, seq, d)
```

```python
import functools
import math

import numpy as np
import jax
import jax.numpy as jnp
from jax import lax
from jax.experimental import pallas as pl
from jax.experimental.pallas import tpu as pltpu

_F32 = jnp.float32
_BF16 = jnp.bfloat16
_EPS = 1e-6

_CHUNK = 64
_DA_HEADS, _DA_DIM = 4, 64
_DA_VDIM = 2 * _DA_DIM
_SB_HEADS, _SB_DIM = 4, 64
_MLA_HEADS, _MLA_NOPE, _MLA_ROPE, _MLA_V = 4, 64, 32, 64
_MLA_QK = _MLA_NOPE + _MLA_ROPE
_MLA_Q_RANK, _MLA_KV_RANK = 256, 128
_ROPE_THETA = 10000.0
_NUM_BUCKETS, _MAX_DISTANCE = 32, 128
_MEM_HEADS, _MEM_DIM = 4, 64

_LANES = 128
_MXU = 256
_VMEM_LIMIT = 52 * 1024 * 1024

_ATT_TILE = 256
_ROW_TILE = 512

_IN_SIZES = (512, 512, 512, 256, 256, 256, _MLA_Q_RANK, _MLA_KV_RANK, _MLA_ROPE)
_IN_OFFS = tuple(int(v) for v in np.cumsum((0,) + _IN_SIZES))
_NT = (((1,), (1,)), ((), ()))
_LOG2E = math.log2(math.e)


def _const_spec(shape):
    zeros = (0,) * len(shape)
    return pl.BlockSpec(shape, lambda *_: zeros, pipeline_mode=pl.Buffered(1))


def _params(*sem):
    return pltpu.CompilerParams(dimension_semantics=sem, vmem_limit_bytes=_VMEM_LIMIT)


def _rms(x, g):
    return x * lax.rsqrt(jnp.mean(x * x, axis=-1, keepdims=True) + _EPS) * g


def _group_mean_sq(y, gmat, group):
    sq = (y * y).astype(_BF16)
    cols = y.shape[1]
    parts = [jnp.dot(sq[:, c:c + _MXU], gmat, preferred_element_type=_F32)
             for c in range(0, cols, _MXU)]
    ss = parts[0] if len(parts) == 1 else jnp.concatenate(parts, axis=1)
    return ss * (1.0 / group)


def _inproj_kernel(x_ref, gmix_ref, w_ref, g64_ref, gq_ref, gk_ref, cqg_ref, ckvg_ref,
                   wuq_ref, wukv_ref, qg_ref, kg_ref, cos_ref, s1_ref, s2_ref,
                   daq_ref, dak_ref, dav_ref, sbq_ref, sbk_ref, sbv_ref,
                   mq_ref, mk_ref, mv_ref):
    x = x_ref[...]
    h = _rms(x, gmix_ref[...]).astype(_BF16)
    o = _IN_OFFS

    def proj(seg, width=None):
        hi = o[seg + 1] if width is None else o[seg] + width
        return jnp.dot(h, w_ref[:, o[seg]:hi], preferred_element_type=_F32)

    g64 = g64_ref[...]
    t = daq_ref.shape[2]

    def norm64(y, g):
        return y * lax.rsqrt(_group_mean_sq(y, g64, _DA_DIM) + _EPS) * g

    def store_t(ref, y, row0=0):
        for r in range(y.shape[0] // t):
            ref[r, row0:row0 + y.shape[1], :] = y[r * t:(r + 1) * t, :].T.astype(_BF16)

    cos, s1, s2 = cos_ref[...], s1_ref[...], s2_ref[...]

    def head_norm_rope(y, g):
        ms = jnp.sum(y * y, axis=-1, keepdims=True) * (1.0 / _MLA_QK)
        yn = y * lax.rsqrt(ms + _EPS) * g
        half = _MLA_ROPE // 2
        return (yn * cos + pltpu.roll(yn, half, 1) * s1
                + pltpu.roll(yn, _LANES - half, 1) * s2)

    cq = _rms(proj(6), cqg_ref[...]).astype(_BF16)
    q_all = jnp.dot(cq, wuq_ref[...], preferred_element_type=_F32)
    ckv = _rms(proj(7), ckvg_ref[...]).astype(_BF16)
    kv_all = jnp.dot(ckv, wukv_ref[...], preferred_element_type=_F32)
    k_rope = proj(8, _LANES)

    def latent_head(hd):
        sl = slice(hd * _LANES, (hd + 1) * _LANES)
        store_t(mq_ref, head_norm_rope(q_all[:, sl], qg_ref[:, sl]), hd * _LANES)
        mk_ref[:, sl] = head_norm_rope(kv_all[:, sl] + k_rope, kg_ref[:, sl]).astype(_BF16)

    store_t(daq_ref, norm64(proj(0), gq_ref[...]))
    latent_head(0)
    dak_ref[...] = norm64(proj(1), gk_ref[...]).astype(_BF16)
    latent_head(1)
    store_t(dav_ref, proj(2))
    latent_head(2)
    store_t(sbq_ref, proj(3) * (_SB_DIM ** -0.5 * _LOG2E))
    latent_head(3)
    sbk_ref[...] = proj(4).astype(_BF16)
    store_t(mv_ref, kv_all[:, _MLA_HEADS * _LANES:])
    store_t(sbv_ref, proj(5))


def _inproj(x2d, lp, seq):
    m, d = x2d.shape
    tm = _ROW_TILE
    t = _ATT_TILE
    pos_blocks = seq // tm
    tab = pl.BlockSpec((tm, _LANES), lambda i: (i % pos_blocks, 0))
    widths = (512, 512, 512, 256, 256, 256, 512, 512, 256)
    transposed = (True, False, True, True, False, True, True, False, True)
    out_specs, out_shape = [], []
    for c, tr in zip(widths, transposed):
        if tr:
            out_specs.append(pl.BlockSpec((tm // t, c, t), lambda i: (i, 0, 0)))
            out_shape.append(jax.ShapeDtypeStruct((m // t, c, t), _BF16))
        else:
            out_specs.append(pl.BlockSpec((tm, c), lambda i: (i, 0)))
            out_shape.append(jax.ShapeDtypeStruct((m, c), _BF16))
    row = lambda c: pl.BlockSpec((tm, c), lambda i: (i, 0))
    return pl.pallas_call(
        _inproj_kernel,
        grid=(m // tm,),
        in_specs=[row(d), _const_spec((1, d)), _const_spec(lp["w_in"].shape),
                  _const_spec((_MXU, _MXU)), _const_spec((1, 512)), _const_spec((1, 512)),
                  _const_spec((1, _MLA_Q_RANK)), _const_spec((1, _MLA_KV_RANK)),
                  _const_spec(lp["w_uq"].shape), _const_spec(lp["w_ukv"].shape),
                  _const_spec((1, 512)), _const_spec((1, 512)), tab, tab, tab],
        out_specs=out_specs,
        out_shape=out_shape,
        compiler_params=_params("parallel"),
        name="inproj",
    )(x2d, lp["mix_g"], lp["w_in"], lp["g64"], lp["da_qg"], lp["da_kg"], lp["cq_g"],
      lp["ckv_g"], lp["w_uq"], lp["w_ukv"], lp["mla_qg"], lp["mla_kg"],
      lp["rope_cos"], lp["rope_s1"], lp["rope_s2"])


_SUM_ROWS = 16


def _softmax_steps(scores, values, ones, states):
    out = []
    for s, v, (m_old, acc) in zip(scores, values, states):
        m_new = jnp.maximum(m_old, jnp.max(s, axis=0, keepdims=True))
        alpha = jnp.exp2(m_old - m_new)
        p = jnp.exp2(s - m_new).astype(_BF16)
        v_ones = jnp.concatenate([v, ones], axis=0)
        out.append((m_new, alpha * acc + jnp.dot(v_ones, p, preferred_element_type=_F32)))
    return tuple(out)


def _softmax_init(t, width):
    return jnp.full((1, t), -jnp.inf, _F32), jnp.zeros((width + _SUM_ROWS, t), _F32)


def _softmax_finish(state, rows):
    _, acc = state
    width = acc.shape[0] - _SUM_ROWS
    return acc[rows] * (1.0 / acc[width:width + 1])


def _rows(i, t):
    return pl.ds(pl.multiple_of(i * t, t), t)


def _split_rows(x_t, half):
    zero = jnp.zeros((half, x_t.shape[1]), x_t.dtype)
    return (jnp.concatenate([x_t[:half], zero], axis=0),
            jnp.concatenate([zero, x_t[half:]], axis=0))


def _half_row_norm(o_t, half, g_t):
    sq = o_t * o_t
    lo = lax.rsqrt(jnp.mean(sq[:half], axis=0, keepdims=True) + _EPS)
    hi = lax.rsqrt(jnp.mean(sq[half:], axis=0, keepdims=True) + _EPS)
    return jnp.concatenate([o_t[:half] * lo, o_t[half:] * hi], axis=0) * g_t


def _da_kernel(lam_ref, q_ref, k_ref, v_ref, bias_ref, g_ref, o_ref, *, t):
    nq = q_ref.shape[0]
    lam = lam_ref[0]

    heads = range(_DA_HEADS)
    head = lambda hd: slice(hd * _LANES, (hd + 1) * _LANES)
    ones = jnp.ones((_SUM_ROWS, t), _BF16)
    all_rows = slice(0, _DA_VDIM)

    def q_tile(qi, carry):
        qm = [qh for hd in heads for qh in _split_rows(q_ref[qi, head(hd), :], _DA_DIM)]

        def kv_step(ki, st):
            d = jnp.minimum(qi - ki, 2)
            scores = [jnp.dot(k_ref[_rows(ki, t), head(hd)], qm[2 * hd + mi],
                              preferred_element_type=_F32) + bias_ref[hd, d, mi]
                      for hd in heads for mi in range(2)]
            values = [v_ref[ki, head(hd), :] for hd in heads for _ in range(2)]
            return _softmax_steps(scores, values, ones, st)

        init = tuple(_softmax_init(t, _DA_VDIM) for _ in range(2 * _DA_HEADS))
        st = lax.fori_loop(0, qi + 1, kv_step, init)
        for hd in heads:
            o_t = (_softmax_finish(st[2 * hd], all_rows)
                   - lam * _softmax_finish(st[2 * hd + 1], all_rows))
            o_t = o_t * lax.rsqrt(jnp.mean(o_t * o_t, axis=0, keepdims=True) + _EPS) * g_ref[...]
            o_ref[_rows(qi, t), head(hd)] = o_t.T.astype(o_ref.dtype)
        return carry

    lax.fori_loop(0, nq, q_tile, 0)


def _da_attention(q_t, k, v_t, bias, lam, g_t, batch, seq):
    t = _ATT_TILE
    nq = seq // t
    width = _DA_HEADS * _LANES
    tiles = pl.BlockSpec((nq, width, t), lambda b: (b, 0, 0))
    rows = pl.BlockSpec((seq, width), lambda b: (b, 0))
    return pl.pallas_call(
        functools.partial(_da_kernel, t=t),
        grid=(batch,),
        in_specs=[pl.BlockSpec(memory_space=pltpu.SMEM), tiles, rows, tiles,
                  _const_spec(bias.shape), _const_spec((_LANES, t))],
        out_specs=rows,
        out_shape=jax.ShapeDtypeStruct(k.shape, _BF16),
        compiler_params=_params("parallel"),
        name="diff_attention",
    )(lam, q_t, k, v_t, bias, g_t)


def _mla_kernel(q_ref, k_ref, v_ref, mask_ref, g_ref, o_ref, *, t):
    nq = q_ref.shape[0]

    heads = range(_MLA_HEADS)
    head = lambda hd: slice(hd * _LANES, (hd + 1) * _LANES)
    pair = lambda hd: slice((hd // 2) * _LANES, (hd // 2 + 1) * _LANES)
    ones = jnp.ones((_SUM_ROWS, t), _BF16)

    def q_tile(qi, carry):
        qs = [q_ref[qi, head(hd), :] for hd in heads]

        def kv_step(ki, st, mask):
            scores = [jnp.dot(k_ref[_rows(ki, t), head(hd)], qs[hd], preferred_element_type=_F32)
                      for hd in heads]
            if mask is not None:
                scores = [s + mask for s in scores]
            values = [v_ref[ki, pair(hd), :] for hd in heads]
            return _softmax_steps(scores, values, ones, st)

        init = tuple(_softmax_init(t, _LANES) for _ in heads)
        st = lax.fori_loop(0, qi, lambda ki, st: kv_step(ki, st, None), init)
        st = kv_step(qi, st, mask_ref[...])
        for p in range(_MLA_HEADS // 2):
            o_t = jnp.concatenate([_softmax_finish(st[2 * p], slice(0, _MLA_V)),
                                   _softmax_finish(st[2 * p + 1], slice(_MLA_V, _LANES))], axis=0)
            o_ref[_rows(qi, t), head(p)] = _half_row_norm(o_t, _MLA_V, g_ref[...]).T.astype(o_ref.dtype)
        return carry

    lax.fori_loop(0, nq, q_tile, 0)


def _mla_attention(q_t, k, v_t, mask_t, g_t, batch, seq):
    t = _ATT_TILE
    nq = seq // t
    qk_width = _MLA_HEADS * _LANES
    v_width = _MLA_HEADS * _MLA_V
    return pl.pallas_call(
        functools.partial(_mla_kernel, t=t),
        grid=(batch,),
        in_specs=[pl.BlockSpec((nq, qk_width, t), lambda b: (b, 0, 0)),
                  pl.BlockSpec((seq, qk_width), lambda b: (b, 0)),
                  pl.BlockSpec((nq, v_width, t), lambda b: (b, 0, 0)),
                  _const_spec((t, t)), _const_spec((_LANES, t))],
        out_specs=pl.BlockSpec((seq, v_width), lambda b: (b, 0)),
        out_shape=jax.ShapeDtypeStruct((batch * seq, v_width), _BF16),
        compiler_params=_params("parallel"),
        name="latent_attention",
    )(q_t, k, v_t, mask_t, g_t)


def _sb_kernel(q_ref, k_ref, v_ref, g_ref, o_ref, *, t):
    nq = q_ref.shape[0]
    key_idx = lax.broadcasted_iota(jnp.int32, (t, t), 0)
    query_idx = lax.broadcasted_iota(jnp.int32, (t, t), 1)
    earlier = key_idx < query_idx
    tri = jnp.where(earlier, 1.0, 0.0).astype(_BF16)

    heads = range(_SB_HEADS)
    pair = lambda hd: slice((hd // 2) * _LANES, (hd // 2 + 1) * _LANES)
    dot = functools.partial(jnp.dot, preferred_element_type=_F32)

    def q_tile(qi, carry):
        qh = [q for p in range(_SB_HEADS // 2) for q in _split_rows(q_ref[qi, pair(2 * p), :], _SB_DIM)]

        def kv_step(ki, st, diag):
            z = [dot(k_ref[_rows(ki, t), pair(hd)], qh[hd]) for hd in heads]
            lp = [jnp.log2(1.0 + jnp.exp2(jnp.minimum(zi, -zi))) for zi in z]
            log_beta = [jnp.minimum(zi, 0.0) - li for zi, li in zip(z, lp)]
            log_1m = [lb - zi for lb, zi in zip(log_beta, z)]
            if diag:
                log_1m = [jnp.where(earlier, li, 0.0) for li in log_1m]
            l_hi = [li.astype(_BF16) for li in log_1m]
            l_lo = [(li - hi.astype(_F32)).astype(_BF16) for li, hi in zip(log_1m, l_hi)]
            later = [dot(tri, hi) + dot(tri, lo) for hi, lo in zip(l_hi, l_lo)]
            a = [jnp.exp2(lb + la + run) for lb, la, (run, _) in zip(log_beta, later, st)]
            if diag:
                a = [jnp.where(earlier, ai, 0.0) for ai in a]
            pv = [dot(v_ref[ki, pair(hd), :], a[hd].astype(_BF16)) for hd in heads]
            return tuple((run + la[0:1, :] + li[0:1, :], acc + pvi)
                         for (run, acc), la, li, pvi in zip(st, later, log_1m, pv))

        init = tuple((jnp.zeros((1, t), _F32), jnp.zeros((_LANES, t), _F32)) for _ in heads)
        st = kv_step(qi, init, True)
        st = lax.fori_loop(0, qi, lambda j, st: kv_step(qi - 1 - j, st, False), st)
        for p in range(_SB_HEADS // 2):
            o_t = jnp.concatenate([st[2 * p][1][:_SB_DIM], st[2 * p + 1][1][_SB_DIM:]], axis=0)
            o_ref[_rows(qi, t), pair(2 * p)] = _half_row_norm(o_t, _SB_DIM, g_ref[...]).T.astype(o_ref.dtype)
        return carry

    lax.fori_loop(0, nq, q_tile, 0)


def _sb_attention(q_t, k, v_t, g_t, batch, seq):
    t = _ATT_TILE
    nq = seq // t
    width = _SB_HEADS * _SB_DIM
    tiles = pl.BlockSpec((nq, width, t), lambda b: (b, 0, 0))
    rows = pl.BlockSpec((seq, width), lambda b: (b, 0))
    return pl.pallas_call(
        functools.partial(_sb_kernel, t=t),
        grid=(batch,),
        in_specs=[tiles, rows, tiles, _const_spec((_LANES, t))],
        out_specs=rows,
        out_shape=jax.ShapeDtypeStruct(k.shape, _BF16),
        compiler_params=_params("parallel"),
        name="stick_breaking",
    )(q_t, k, v_t, g_t)


def _memkv_kernel(mem_ref, g_ref, w_ref, g64_ref, kg_ref, k_ref, v_ref):
    m = _rms(mem_ref[...], g_ref[...]).astype(_BF16)
    kv = jnp.dot(m, w_ref[...], preferred_element_type=_F32)
    width = _MEM_HEADS * _MEM_DIM
    k = kv[:, :width]
    k = k * lax.rsqrt(_group_mean_sq(k, g64_ref[...], _MEM_DIM) + _EPS) * kg_ref[...]
    k_ref[...] = k.astype(_BF16)
    v_ref[...] = kv[:, width:].astype(_BF16)


def _memkv(mem2d, g, w, g64, kg):
    depth = w.shape[0]
    rows, d = mem2d.shape
    tm = _ROW_TILE
    width = _MEM_HEADS * _MEM_DIM
    out = pl.BlockSpec((None, tm, width), lambda l, i: (l, i, 0))
    return pl.pallas_call(
        _memkv_kernel,
        grid=(depth, rows // tm),
        in_specs=[pl.BlockSpec((tm, d), lambda l, i: (i, 0)),
                  pl.BlockSpec((None, 1, d), lambda l, i: (l, 0, 0)),
                  pl.BlockSpec((None, d, 2 * width), lambda l, i: (l, 0, 0)),
                  _const_spec((_MXU, _MXU)),
                  pl.BlockSpec((None, 1, width), lambda l, i: (l, 0, 0))],
        out_specs=[out, out],
        out_shape=[jax.ShapeDtypeStruct((depth, rows, width), _BF16)] * 2,
        compiler_params=_params("parallel", "parallel"),
        name="memory_kv",
    )(mem2d, g, w, g64, kg)


def _mix_mem_kernel(x_ref, ya_ref, yb_ref, yc_ref, wo_ref, gx_ref, wq_ref, g64_ref, qg_ref,
                    km_ref, vm_ref, wmo_ref, o_ref):
    wa = ya_ref.shape[1]
    wb = wa + yb_ref.shape[1]
    x = (x_ref[...]
         + jnp.dot(ya_ref[...], wo_ref[:wa, :], preferred_element_type=_F32)
         + jnp.dot(yb_ref[...], wo_ref[wa:wb, :], preferred_element_type=_F32)
         + jnp.dot(yc_ref[...], wo_ref[wb:, :], preferred_element_type=_F32))
    h = _rms(x, gx_ref[...]).astype(_BF16)
    q = jnp.dot(h, wq_ref[...], preferred_element_type=_F32)
    q = (q * lax.rsqrt(_group_mean_sq(q, g64_ref[...], _MEM_DIM) + _EPS) * qg_ref[...]).astype(_BF16)
    km = km_ref[...]
    vm = vm_ref[...]
    head_of_lane = lax.broadcasted_iota(jnp.int32, (1, q.shape[1]), 1) // _MEM_DIM
    zero = jnp.zeros_like(q)
    o = jnp.zeros(q.shape, _F32)
    for hd in range(_MEM_HEADS):
        sel = head_of_lane == hd
        s = lax.dot_general(jnp.where(sel, q, zero), km, _NT, preferred_element_type=_F32)
        p = jnp.exp(s - jnp.max(s, axis=-1, keepdims=True))
        oh = jnp.dot(p.astype(_BF16), vm, preferred_element_type=_F32)
        o = jnp.where(sel, oh / jnp.sum(p, axis=-1, keepdims=True), o)
    o_ref[...] = x + jnp.dot(o.astype(_BF16), wmo_ref[...], preferred_element_type=_F32)


def _mix_mem(x2d, ya, yb, yc, lp, km, vm, seq):
    m, d = x2d.shape
    tm = _ROW_TILE
    per_seq = seq // tm
    n_mem, width = km.shape[1], km.shape[2]
    row = lambda c: pl.BlockSpec((tm, c), lambda i: (i, 0))
    mem = pl.BlockSpec((None, n_mem, width), lambda i: (i // per_seq, 0, 0))
    return pl.pallas_call(
        _mix_mem_kernel,
        grid=(m // tm,),
        in_specs=[row(d), row(ya.shape[1]), row(yb.shape[1]), row(yc.shape[1]),
                  _const_spec(lp["w_out"].shape), _const_spec((1, d)),
                  _const_spec(lp["w_mem_q"].shape), _const_spec((_MXU, _MXU)),
                  _const_spec((1, width)), mem, mem, _const_spec(lp["w_mem_o"].shape)],
        out_specs=row(d),
        out_shape=jax.ShapeDtypeStruct((m, d), _F32),
        compiler_params=_params("parallel"),
        name="mix_and_memory",
    )(x2d, ya, yb, yc, lp["w_out"], lp["memx_g"], lp["w_mem_q"], lp["g64"], lp["mem_qg"],
      km, vm, lp["w_mem_o"])


_FF_CHUNK = 1024


def _ffn_kernel(x_ref, g_ref, w1_ref, w2_ref, o_ref):
    x = x_ref[...]
    h = _rms(x, g_ref[...]).astype(_BF16)
    acc = x
    for c in range(0, w1_ref.shape[1], _FF_CHUNK):
        u = jnp.dot(h, w1_ref[:, c:c + _FF_CHUNK], preferred_element_type=_F32)
        r = jnp.maximum(u, 0.0)
        acc = acc + jnp.dot((r * r).astype(_BF16), w2_ref[c:c + _FF_CHUNK, :],
                            preferred_element_type=_F32)
    o_ref[...] = acc


def _ffn(x2d, g, w1, w2):
    m, d = x2d.shape
    tm = _ROW_TILE
    row = pl.BlockSpec((tm, d), lambda i: (i, 0))
    return pl.pallas_call(
        _ffn_kernel,
        grid=(m // tm,),
        in_specs=[row, _const_spec((1, d)), _const_spec(w1.shape), _const_spec(w2.shape)],
        out_specs=row,
        out_shape=jax.ShapeDtypeStruct((m, d), _F32),
        compiler_params=_params("parallel"),
        name="ffn",
    )(x2d, g, w1, w2)


def _t5_bucket(rel):
    nb = _NUM_BUCKETS // 2
    bucket = (rel > 0).astype(jnp.int32) * nb
    n = jnp.abs(rel)
    max_exact = nb // 2
    is_small = n < max_exact
    large = max_exact + (jnp.log(jnp.maximum(n, 1).astype(jnp.float32) / max_exact)
                         / math.log(_MAX_DISTANCE / max_exact) * (nb - max_exact)).astype(jnp.int32)
    large = jnp.minimum(large, nb - 1)
    return bucket + jnp.where(is_small, n, large)


def _da_bias_tables(rel_bias, t):
    assert t + 1 >= _MAX_DISTANCE and t % _CHUNK == 0
    j = jnp.arange(t, dtype=jnp.int32)[:, None]
    i = jnp.arange(t, dtype=jnp.int32)[None, :]
    rb = rel_bias.astype(_F32) * _LOG2E

    def lookup(bucket):
        hit = bucket[:, :, None, None] == jnp.arange(_NUM_BUCKETS, dtype=jnp.int32)[:, None]
        return jnp.sum(jnp.where(hit, rb[None, None], 0.0), axis=2)

    b0 = lookup(_t5_bucket(j - i))
    b0 = jnp.where(((j // _CHUNK) <= (i // _CHUNK))[:, :, None], b0, -jnp.inf)
    b1 = lookup(_t5_bucket(j - i - t))
    far = jnp.broadcast_to(lookup(_t5_bucket(jnp.full((1, 1), -(t + 1), jnp.int32))), b1.shape)
    tab = jnp.stack([b0, b1, far]).reshape(3, t, t, _DA_HEADS, 2)
    return tab.transpose(3, 0, 4, 1, 2)


def _rope_tables(seq):
    half = _MLA_ROPE // 2
    freqs = _ROPE_THETA ** (-jnp.arange(half, dtype=jnp.float32) / half)
    ang = jnp.arange(seq, dtype=jnp.int32).astype(jnp.float32)[:, None] * freqs[None, :]
    cos, sin = jnp.cos(ang), jnp.sin(ang)
    ones = jnp.ones((seq, _MLA_NOPE), _F32)
    z = lambda w: jnp.zeros((seq, w), _F32)
    tail = _LANES - _MLA_QK
    c = jnp.concatenate([ones, cos, cos, z(tail)], axis=1)
    s1 = jnp.concatenate([z(_MLA_NOPE + half), sin, z(tail)], axis=1)
    s2 = jnp.concatenate([z(_MLA_NOPE), -sin, z(half + tail)], axis=1)
    return c, s1, s2


def _group_ones(group):
    idx = np.arange(_MXU) // group
    return jnp.asarray(idx[:, None] == idx[None, :], dtype=_BF16)


def _layer_params(l, p, rope, g64):
    d = p["w_in"].shape[1]
    o = _IN_OFFS
    w_in = p["w_in"][l]
    kr = jnp.zeros((d, _LANES), _F32).at[:, _MLA_NOPE:_MLA_QK].set(w_in[:, o[8]:o[9]])
    w_uq = jnp.pad(p["w_mla_uq"][l].reshape(_MLA_Q_RANK, _MLA_HEADS, _MLA_QK),
                   ((0, 0), (0, 0), (0, _LANES - _MLA_QK))).reshape(_MLA_Q_RANK, -1)
    w_ukv = p["w_mla_ukv"][l].reshape(_MLA_KV_RANK, _MLA_HEADS, _MLA_NOPE + _MLA_V)
    w_k = jnp.pad(w_ukv[:, :, :_MLA_NOPE], ((0, 0), (0, 0), (0, _LANES - _MLA_NOPE)))
    w_v = w_ukv[:, :, _MLA_NOPE:]
    pad_g = lambda g: jnp.tile(jnp.pad(g, (0, _LANES - _MLA_QK)), _MLA_HEADS)[None]
    col = lambda g: jnp.broadcast_to(g[:, None], (g.shape[0], _ATT_TILE))
    lam_init = 0.8 - 0.6 * math.exp(-0.3 * l)
    lp = p["da_lambda"][l].astype(_F32)
    lam = jnp.exp(jnp.sum(lp[0] * lp[1])) - jnp.exp(jnp.sum(lp[2] * lp[3])) + lam_init
    return {
        "mix_g": p["mix_norm_g"][l][None],
        "w_in": jnp.concatenate([w_in[:, :o[8]], kr], axis=1).astype(_BF16),
        "g64": g64,
        "da_qg": jnp.tile(p["da_q_norm_g"][l], 2 * _DA_HEADS)[None] * (_DA_DIM ** -0.5 * _LOG2E),
        "da_kg": jnp.tile(p["da_k_norm_g"][l], 2 * _DA_HEADS)[None],
        "cq_g": p["mla_cq_norm_g"][l][None],
        "ckv_g": p["mla_ckv_norm_g"][l][None],
        "w_uq": w_uq.astype(_BF16),
        "w_ukv": jnp.concatenate([w_k.reshape(_MLA_KV_RANK, -1), w_v.reshape(_MLA_KV_RANK, -1)],
                                 axis=1).astype(_BF16),
        "mla_qg": pad_g(p["mla_q_norm_g"][l]) * (_MLA_QK ** -0.5 * _LOG2E),
        "mla_kg": pad_g(p["mla_k_norm_g"][l]),
        "rope_cos": rope[0], "rope_s1": rope[1], "rope_s2": rope[2],
        "lam": jnp.reshape(lam, (1,)).astype(_F32),
        "da_og": col(p["da_subln_g"][l] * (1.0 - lam_init)),
        "sb_og": col(jnp.tile(p["sb_out_g"][l], 2)),
        "mla_og": col(jnp.tile(p["mla_out_g"][l], 2)),
        "w_out": p["w_out"][l].astype(_BF16),
        "memx_g": p["memx_norm_g"][l][None],
        "w_mem_q": p["w_mem_q"][l].astype(_BF16),
        "mem_qg": jnp.tile(p["mem_q_norm_g"][l], _MEM_HEADS)[None] * (_MEM_DIM ** -0.5),
        "w_mem_o": p["w_mem_o"][l].astype(_BF16),
        "ffn_g": p["ffn_norm_g"][l][None],
        "w_ff1": p["w_ff1"][l].astype(_BF16),
        "w_ff2": p["w_ff2"][l].astype(_BF16),
    }


def kernel(x, mem, rel_bias, mix_norm_g, w_in, da_q_norm_g, da_k_norm_g, da_lambda, da_subln_g,
           sb_out_g, mla_cq_norm_g, mla_ckv_norm_g, w_mla_uq, w_mla_ukv, mla_q_norm_g, mla_k_norm_g,
           mla_out_g, w_out, memx_norm_g, mem_norm_g, w_mem_q, w_mem_kv, mem_q_norm_g, mem_k_norm_g,
           w_mem_o, ffn_norm_g, w_ff1, w_ff2):
    p = dict(mix_norm_g=mix_norm_g, w_in=w_in, da_q_norm_g=da_q_norm_g, da_k_norm_g=da_k_norm_g,
             da_lambda=da_lambda, da_subln_g=da_subln_g, sb_out_g=sb_out_g,
             mla_cq_norm_g=mla_cq_norm_g, mla_ckv_norm_g=mla_ckv_norm_g, w_mla_uq=w_mla_uq,
             w_mla_ukv=w_mla_ukv, mla_q_norm_g=mla_q_norm_g, mla_k_norm_g=mla_k_norm_g,
             mla_out_g=mla_out_g, w_out=w_out, memx_norm_g=memx_norm_g, w_mem_q=w_mem_q,
             mem_q_norm_g=mem_q_norm_g, w_mem_o=w_mem_o, ffn_norm_g=ffn_norm_g, w_ff1=w_ff1,
             w_ff2=w_ff2)
    batch, seq, d = x.shape
    depth = w_in.shape[0]
    n_mem = mem.shape[1]
    t = _ATT_TILE
    assert seq % _ROW_TILE == 0 and seq % t == 0 and (batch * n_mem) % _ROW_TILE == 0
    assert w_in.shape[2] == _IN_OFFS[-1]

    g64 = _group_ones(_MEM_DIM)
    rope = _rope_tables(seq)
    bias = _da_bias_tables(rel_bias, t)
    i = jnp.arange(t, dtype=jnp.int32)
    chunk_mask = jnp.where((i[:, None] // _CHUNK) <= (i[None, :] // _CHUNK), 0.0, -jnp.inf).astype(_F32)

    width = _MEM_HEADS * _MEM_DIM
    km, vm = _memkv(mem.reshape(batch * n_mem, d), mem_norm_g[:, None, :], w_mem_kv.astype(_BF16), g64,
                    jnp.tile(mem_k_norm_g, (1, _MEM_HEADS))[:, None, :])
    km = km.reshape(depth, batch, n_mem, width)
    vm = vm.reshape(depth, batch, n_mem, width)

    x2d = x.reshape(batch * seq, d)
    for l in range(depth):
        lp = _layer_params(l, p, rope, g64)
        daq, dak, dav, sbq, sbk, sbv, mq, mk, mv = _inproj(x2d, lp, seq)
        ya = _da_attention(daq, dak, dav, bias, lp["lam"], lp["da_og"], batch, seq)
        yb = _sb_attention(sbq, sbk, sbv, lp["sb_og"], batch, seq)
        yc = _mla_attention(mq, mk, mv, chunk_mask, lp["mla_og"], batch, seq)
        x2d = _mix_mem(x2d, ya, yb, yc, lp, km[l], vm[l], seq)
        x2d = _ffn(x2d, lp["ffn_g"], lp["w_ff1"], lp["w_ff2"])
    return x2d.reshape(batch, seq, d)
```

```python
import functools
import math

import numpy as np
import jax
import jax.numpy as jnp
from jax import lax
from jax.experimental import pallas as pl
from jax.experimental.pallas import tpu as pltpu

_F32 = jnp.float32
_BF16 = jnp.bfloat16
_EPS = 1e-6

_CHUNK = 64
_DA_HEADS, _DA_DIM = 4, 64
_DA_VDIM = 2 * _DA_DIM
_SB_HEADS, _SB_DIM = 4, 64
_MLA_HEADS, _MLA_NOPE, _MLA_ROPE, _MLA_V = 4, 64, 32, 64
_MLA_QK = _MLA_NOPE + _MLA_ROPE
_MLA_Q_RANK, _MLA_KV_RANK = 256, 128
_ROPE_THETA = 10000.0
_NUM_BUCKETS, _MAX_DISTANCE = 32, 128
_MEM_HEADS, _MEM_DIM = 4, 64

_LANES = 128
_MXU = 256
_VMEM_LIMIT = 52 * 1024 * 1024

_ATT_TILE = 256
_ROW_TILE = 512

_IN_SIZES = (512, 512, 512, 256, 256, 256, _MLA_Q_RANK, _MLA_KV_RANK, _MLA_ROPE)
_IN_OFFS = tuple(int(v) for v in np.cumsum((0,) + _IN_SIZES))
_NT = (((1,), (1,)), ((), ()))
_LOG2E = math.log2(math.e)


def _const_spec(shape):
    zeros = (0,) * len(shape)
    return pl.BlockSpec(shape, lambda *_: zeros, pipeline_mode=pl.Buffered(1))


def _params(*sem):
    return pltpu.CompilerParams(dimension_semantics=sem, vmem_limit_bytes=_VMEM_LIMIT)


def _rms(x, g):
    return x * lax.rsqrt(jnp.mean(x * x, axis=-1, keepdims=True) + _EPS) * g


def _group_mean_sq(y, gmat, group):
    sq = (y * y).astype(_BF16)
    cols = y.shape[1]
    parts = [jnp.dot(sq[:, c:c + _MXU], gmat, preferred_element_type=_F32)
             for c in range(0, cols, _MXU)]
    ss = parts[0] if len(parts) == 1 else jnp.concatenate(parts, axis=1)
    return ss * (1.0 / group)


def _inproj_kernel(x_ref, gmix_ref, w_ref, g64_ref, gq_ref, gk_ref, cqg_ref, ckvg_ref,
                   wuq_ref, wukv_ref, qg_ref, kg_ref, cos_ref, s1_ref, s2_ref,
                   daq_ref, dak_ref, dav_ref, sbq_ref, sbk_ref, sbv_ref,
                   mq_ref, mk_ref, mv_ref):
    x = x_ref[...]
    h = _rms(x, gmix_ref[...]).astype(_BF16)
    o = _IN_OFFS

    def proj(seg, width=None):
        hi = o[seg + 1] if width is None else o[seg] + width
        return jnp.dot(h, w_ref[:, o[seg]:hi], preferred_element_type=_F32)

    g64 = g64_ref[...]
    t = daq_ref.shape[2]

    def norm64(y, g):
        return y * lax.rsqrt(_group_mean_sq(y, g64, _DA_DIM) + _EPS) * g

    def store_t(ref, y, row0=0):
        for r in range(y.shape[0] // t):
            ref[r, row0:row0 + y.shape[1], :] = y[r * t:(r + 1) * t, :].T.astype(_BF16)

    cos, s1, s2 = cos_ref[...], s1_ref[...], s2_ref[...]

    def head_norm_rope(y, g):
        ms = jnp.sum(y * y, axis=-1, keepdims=True) * (1.0 / _MLA_QK)
        yn = y * lax.rsqrt(ms + _EPS) * g
        half = _MLA_ROPE // 2
        return (yn * cos + pltpu.roll(yn, half, 1) * s1
                + pltpu.roll(yn, _LANES - half, 1) * s2)

    cq = _rms(proj(6), cqg_ref[...]).astype(_BF16)
    q_all = jnp.dot(cq, wuq_ref[...], preferred_element_type=_F32)
    ckv = _rms(proj(7), ckvg_ref[...]).astype(_BF16)
    kv_all = jnp.dot(ckv, wukv_ref[...], preferred_element_type=_F32)
    k_rope = proj(8, _LANES)

    def latent_head(hd):
        sl = slice(hd * _LANES, (hd + 1) * _LANES)
        store_t(mq_ref, head_norm_rope(q_all[:, sl], qg_ref[:, sl]), hd * _LANES)
        mk_ref[:, sl] = head_norm_rope(kv_all[:, sl] + k_rope, kg_ref[:, sl]).astype(_BF16)

    store_t(daq_ref, norm64(proj(0), gq_ref[...]))
    latent_head(0)
    dak_ref[...] = norm64(proj(1), gk_ref[...]).astype(_BF16)
    latent_head(1)
    store_t(dav_ref, proj(2))
    latent_head(2)
    store_t(sbq_ref, proj(3) * (_SB_DIM ** -0.5 * _LOG2E))
    latent_head(3)
    sbk_ref[...] = proj(4).astype(_BF16)
    store_t(mv_ref, kv_all[:, _MLA_HEADS * _LANES:])
    store_t(sbv_ref, proj(5))


def _inproj(x2d, lp, seq):
    m, d = x2d.shape
    tm = _ROW_TILE
    t = _ATT_TILE
    pos_blocks = seq // tm
    tab = pl.BlockSpec((tm, _LANES), lambda i: (i % pos_blocks, 0))
    widths = (512, 512, 512, 256, 256, 256, 512, 512, 256)
    transposed = (True, False, True, True, False, True, True, False, True)
    out_specs, out_shape = [], []
    for c, tr in zip(widths, transposed):
        if tr:
            out_specs.append(pl.BlockSpec((tm // t, c, t), lambda i: (i, 0, 0)))
            out_shape.append(jax.ShapeDtypeStruct((m // t, c, t), _BF16))
        else:
            out_specs.append(pl.BlockSpec((tm, c), lambda i: (i, 0)))
            out_shape.append(jax.ShapeDtypeStruct((m, c), _BF16))
    row = lambda c: pl.BlockSpec((tm, c), lambda i: (i, 0))
    return pl.pallas_call(
        _inproj_kernel,
        grid=(m // tm,),
        in_specs=[row(d), _const_spec((1, d)), _const_spec(lp["w_in"].shape),
                  _const_spec((_MXU, _MXU)), _const_spec((1, 512)), _const_spec((1, 512)),
                  _const_spec((1, _MLA_Q_RANK)), _const_spec((1, _MLA_KV_RANK)),
                  _const_spec(lp["w_uq"].shape), _const_spec(lp["w_ukv"].shape),
                  _const_spec((1, 512)), _const_spec((1, 512)), tab, tab, tab],
        out_specs=out_specs,
        out_shape=out_shape,
        compiler_params=_params("parallel"),
        name="inproj",
    )(x2d, lp["mix_g"], lp["w_in"], lp["g64"], lp["da_qg"], lp["da_kg"], lp["cq_g"],
      lp["ckv_g"], lp["w_uq"], lp["w_ukv"], lp["mla_qg"], lp["mla_kg"],
      lp["rope_cos"], lp["rope_s1"], lp["rope_s2"])


_SUM_ROWS = 16


def _pipelined_tiles(n_tiles, stage, consume, run, s_even, s_odd, stage_first=None):
    def pair(i, carry):
        run, summary = carry
        run = consume(2 * i, s_even, summary, run)
        summary = stage(2 * i + 1, s_odd)
        run = consume(2 * i + 1, s_odd, summary, run)
        return run, stage(2 * i + 2, s_even)

    def tail_two(_, carry):
        run, summary = carry
        run = consume(n_tiles - 2, s_even, summary, run)
        summary = stage(n_tiles - 1, s_odd)
        return consume(n_tiles - 1, s_odd, summary, run), summary

    def tail_one(_, carry):
        run, summary = carry
        return consume(n_tiles - 1, s_even, summary, run), summary

    carry = (run, (stage_first or stage)(0, s_even))
    last = n_tiles - 1
    odd = jnp.bitwise_and(last, 1)
    carry = lax.fori_loop(0, jnp.right_shift(last, 1), pair, carry)
    carry = lax.fori_loop(0, odd, tail_two, carry)
    lax.fori_loop(0, 1 - odd, tail_one, carry)


def _online_softmax(n_tiles, score, value, s_even, s_odd, acc_scr):
    n_chains, _, t = s_even.shape
    ones = jnp.ones((_SUM_ROWS, t), _BF16)
    chains = range(n_chains)

    def stage(ki, buf):
        col_max = []
        for c in chains:
            s = score(c, ki)
            buf[c] = s
            col_max.append(jnp.max(s, axis=0, keepdims=True))
        return tuple(col_max)

    def consume(ki, buf, col_max, m_run):
        out = []
        for c in chains:
            m_new = jnp.maximum(m_run[c], col_max[c])
            alpha = jnp.exp2(m_run[c] - m_new)
            p = jnp.exp2(buf[c] - m_new).astype(_BF16)
            v_ones = jnp.concatenate([value(c, ki), ones], axis=0)
            acc_scr[c] = alpha * acc_scr[c] + jnp.dot(v_ones, p, preferred_element_type=_F32)
            out.append(m_new)
        return tuple(out)

    for c in chains:
        acc_scr[c] = jnp.zeros(acc_scr.shape[1:], _F32)
    m_init = tuple(jnp.full((1, t), -jnp.inf, _F32) for _ in chains)
    _pipelined_tiles(n_tiles, stage, consume, m_init, s_even, s_odd)


def _softmax_finish(acc, rows):
    width = acc.shape[0] - _SUM_ROWS
    return acc[rows] * (1.0 / acc[width:width + 1])


def _rows(i, t):
    return pl.ds(pl.multiple_of(i * t, t), t)


def _split_rows(x_t, half):
    zero = jnp.zeros((half, x_t.shape[1]), x_t.dtype)
    return (jnp.concatenate([x_t[:half], zero], axis=0),
            jnp.concatenate([zero, x_t[half:]], axis=0))


def _half_row_norm(o_t, half, g_t):
    sq = o_t * o_t
    lo = lax.rsqrt(jnp.mean(sq[:half], axis=0, keepdims=True) + _EPS)
    hi = lax.rsqrt(jnp.mean(sq[half:], axis=0, keepdims=True) + _EPS)
    return jnp.concatenate([o_t[:half] * lo, o_t[half:] * hi], axis=0) * g_t


def _da_kernel(lam_ref, q_ref, k_ref, v_ref, bias_ref, g_ref, o_ref, s_even, s_odd, acc_scr, *, t):
    nq = q_ref.shape[0]
    lam = lam_ref[0]

    heads = range(_DA_HEADS)
    head = lambda hd: slice(hd * _LANES, (hd + 1) * _LANES)
    all_rows = slice(0, _DA_VDIM)

    def q_tile(qi, carry):
        qm = [qh for hd in heads for qh in _split_rows(q_ref[qi, head(hd), :], _DA_DIM)]

        def score(c, ki):
            hd, mi = divmod(c, 2)
            d = jnp.minimum(qi - ki, 2)
            return (jnp.dot(k_ref[_rows(ki, t), head(hd)], qm[c], preferred_element_type=_F32)
                    + bias_ref[hd, d, mi])

        value = lambda c, ki: v_ref[ki, head(c // 2), :]
        _online_softmax(qi + 1, score, value, s_even, s_odd, acc_scr)
        for hd in heads:
            o_t = (_softmax_finish(acc_scr[2 * hd], all_rows)
                   - lam * _softmax_finish(acc_scr[2 * hd + 1], all_rows))
            o_t = o_t * lax.rsqrt(jnp.mean(o_t * o_t, axis=0, keepdims=True) + _EPS) * g_ref[...]
            o_ref[_rows(qi, t), head(hd)] = o_t.T.astype(o_ref.dtype)
        return carry

    lax.fori_loop(0, nq, q_tile, 0)


def _da_attention(q_t, k, v_t, bias, lam, g_t, batch, seq):
    t = _ATT_TILE
    nq = seq // t
    width = _DA_HEADS * _LANES
    tiles = pl.BlockSpec((nq, width, t), lambda b: (b, 0, 0))
    rows = pl.BlockSpec((seq, width), lambda b: (b, 0))
    return pl.pallas_call(
        functools.partial(_da_kernel, t=t),
        grid=(batch,),
        in_specs=[pl.BlockSpec(memory_space=pltpu.SMEM), tiles, rows, tiles,
                  _const_spec(bias.shape), _const_spec((_LANES, t))],
        out_specs=rows,
        out_shape=jax.ShapeDtypeStruct(k.shape, _BF16),
        scratch_shapes=[pltpu.VMEM((2 * _DA_HEADS, t, t), _F32), pltpu.VMEM((2 * _DA_HEADS, t, t), _F32),
                        pltpu.VMEM((2 * _DA_HEADS, _DA_VDIM + _SUM_ROWS, t), _F32)],
        compiler_params=_params("parallel"),
        name="diff_attention",
    )(lam, q_t, k, v_t, bias, g_t)


def _mla_kernel(q_ref, k_ref, v_ref, mask_ref, g_ref, o_ref, s_even, s_odd, acc_scr, *, t):
    nq = q_ref.shape[0]

    heads = range(_MLA_HEADS)
    head = lambda hd: slice(hd * _LANES, (hd + 1) * _LANES)
    pair = lambda hd: slice((hd // 2) * _LANES, (hd // 2 + 1) * _LANES)

    def q_tile(qi, carry):
        qs = [q_ref[qi, head(hd), :] for hd in heads]

        def score(hd, ki):
            d = jnp.minimum(qi - ki, 1)
            return (jnp.dot(k_ref[_rows(ki, t), head(hd)], qs[hd], preferred_element_type=_F32)
                    + mask_ref[d])

        value = lambda hd, ki: v_ref[ki, pair(hd), :]
        _online_softmax(qi + 1, score, value, s_even, s_odd, acc_scr)
        for p in range(_MLA_HEADS // 2):
            o_t = jnp.concatenate([_softmax_finish(acc_scr[2 * p], slice(0, _MLA_V)),
                                   _softmax_finish(acc_scr[2 * p + 1], slice(_MLA_V, _LANES))], axis=0)
            o_ref[_rows(qi, t), head(p)] = _half_row_norm(o_t, _MLA_V, g_ref[...]).T.astype(o_ref.dtype)
        return carry

    lax.fori_loop(0, nq, q_tile, 0)


def _mla_attention(q_t, k, v_t, mask_t, g_t, batch, seq):
    t = _ATT_TILE
    nq = seq // t
    qk_width = _MLA_HEADS * _LANES
    v_width = _MLA_HEADS * _MLA_V
    return pl.pallas_call(
        functools.partial(_mla_kernel, t=t),
        grid=(batch,),
        in_specs=[pl.BlockSpec((nq, qk_width, t), lambda b: (b, 0, 0)),
                  pl.BlockSpec((seq, qk_width), lambda b: (b, 0)),
                  pl.BlockSpec((nq, v_width, t), lambda b: (b, 0, 0)),
                  _const_spec((2, t, t)), _const_spec((_LANES, t))],
        out_specs=pl.BlockSpec((seq, v_width), lambda b: (b, 0)),
        out_shape=jax.ShapeDtypeStruct((batch * seq, v_width), _BF16),
        scratch_shapes=[pltpu.VMEM((_MLA_HEADS, t, t), _F32), pltpu.VMEM((_MLA_HEADS, t, t), _F32),
                        pltpu.VMEM((_MLA_HEADS, _LANES + _SUM_ROWS, t), _F32)],
        compiler_params=_params("parallel"),
        name="latent_attention",
    )(q_t, k, v_t, mask_t, g_t)


def _sb_kernel(q_ref, k_ref, v_ref, g_ref, o_ref, s_even, s_odd, acc_scr, *, t):
    nq = q_ref.shape[0]
    key_idx = lax.broadcasted_iota(jnp.int32, (t, t), 0)
    query_idx = lax.broadcasted_iota(jnp.int32, (t, t), 1)
    earlier = key_idx < query_idx
    tri = jnp.where(earlier, 1.0, 0.0).astype(_BF16)
    tri2 = jnp.concatenate([tri, tri], axis=1)

    heads = range(_SB_HEADS)
    pair = lambda hd: slice((hd // 2) * _LANES, (hd // 2 + 1) * _LANES)
    dot = functools.partial(jnp.dot, preferred_element_type=_F32)

    def q_tile(qi, carry):
        qh = [q for p in range(_SB_HEADS // 2) for q in _split_rows(q_ref[qi, pair(2 * p), :], _SB_DIM)]

        def stage(j, buf, diag=False):
            z = [dot(k_ref[_rows(qi - j, t), pair(hd)], qh[hd]) for hd in heads]
            log_beta, first_row, later = [], [], []
            for zi in z:
                lp = jnp.log2(1.0 + jnp.exp2(jnp.minimum(zi, -zi)))
                lb = jnp.minimum(zi, 0.0) - lp
                log_1m = lb - zi
                if diag:
                    log_1m = jnp.where(earlier, log_1m, 0.0)
                l_hi = log_1m.astype(_BF16)
                l_lo = (log_1m - l_hi.astype(_F32)).astype(_BF16)
                later.append(dot(tri2, jnp.concatenate([l_hi, l_lo], axis=0)))
                log_beta.append(lb)
                first_row.append(log_1m[0:1, :])
            for hd in heads:
                log_w = log_beta[hd] + later[hd]
                buf[hd] = jnp.where(earlier, log_w, -jnp.inf) if diag else log_w
            return tuple(la[0:1, :] + fr for la, fr in zip(later, first_row))

        def consume(j, buf, through, run):
            for hd in heads:
                a = jnp.exp2(buf[hd] + run[hd]).astype(_BF16)
                acc_scr[hd] = acc_scr[hd] + dot(v_ref[qi - j, pair(hd), :], a)
            return tuple(r + th for r, th in zip(run, through))

        for hd in heads:
            acc_scr[hd] = jnp.zeros(acc_scr.shape[1:], _F32)
        run0 = tuple(jnp.zeros((1, t), _F32) for _ in heads)
        _pipelined_tiles(qi + 1, stage, consume, run0, s_even, s_odd,
                         stage_first=functools.partial(stage, diag=True))
        for p in range(_SB_HEADS // 2):
            o_t = jnp.concatenate([acc_scr[2 * p, :_SB_DIM, :], acc_scr[2 * p + 1, _SB_DIM:, :]], axis=0)
            o_ref[_rows(qi, t), pair(2 * p)] = _half_row_norm(o_t, _SB_DIM, g_ref[...]).T.astype(o_ref.dtype)
        return carry

    lax.fori_loop(0, nq, q_tile, 0)


def _sb_attention(q_t, k, v_t, g_t, batch, seq):
    t = _ATT_TILE
    nq = seq // t
    width = _SB_HEADS * _SB_DIM
    tiles = pl.BlockSpec((nq, width, t), lambda b: (b, 0, 0))
    rows = pl.BlockSpec((seq, width), lambda b: (b, 0))
    return pl.pallas_call(
        functools.partial(_sb_kernel, t=t),
        grid=(batch,),
        in_specs=[tiles, rows, tiles, _const_spec((_LANES, t))],
        out_specs=rows,
        out_shape=jax.ShapeDtypeStruct(k.shape, _BF16),
        scratch_shapes=[pltpu.VMEM((_SB_HEADS, t, t), _F32), pltpu.VMEM((_SB_HEADS, t, t), _F32),
                        pltpu.VMEM((_SB_HEADS, _LANES, t), _F32)],
        compiler_params=_params("parallel"),
        name="stick_breaking",
    )(q_t, k, v_t, g_t)


def _memkv_kernel(mem_ref, g_ref, w_ref, g64_ref, kg_ref, k_ref, v_ref):
    m = _rms(mem_ref[...], g_ref[...]).astype(_BF16)
    kv = jnp.dot(m, w_ref[...], preferred_element_type=_F32)
    width = _MEM_HEADS * _MEM_DIM
    k = kv[:, :width]
    k = k * lax.rsqrt(_group_mean_sq(k, g64_ref[...], _MEM_DIM) + _EPS) * kg_ref[...]
    k_ref[...] = k.astype(_BF16)
    v_ref[...] = kv[:, width:].astype(_BF16)


def _memkv(mem2d, g, w, g64, kg):
    depth = w.shape[0]
    rows, d = mem2d.shape
    tm = _ROW_TILE
    width = _MEM_HEADS * _MEM_DIM
    out = pl.BlockSpec((None, tm, width), lambda l, i: (l, i, 0))
    return pl.pallas_call(
        _memkv_kernel,
        grid=(depth, rows // tm),
        in_specs=[pl.BlockSpec((tm, d), lambda l, i: (i, 0)),
                  pl.BlockSpec((None, 1, d), lambda l, i: (l, 0, 0)),
                  pl.BlockSpec((None, d, 2 * width), lambda l, i: (l, 0, 0)),
                  _const_spec((_MXU, _MXU)),
                  pl.BlockSpec((None, 1, width), lambda l, i: (l, 0, 0))],
        out_specs=[out, out],
        out_shape=[jax.ShapeDtypeStruct((depth, rows, width), _BF16)] * 2,
        compiler_params=_params("parallel", "parallel"),
        name="memory_kv",
    )(mem2d, g, w, g64, kg)


def _mix_mem_kernel(x_ref, ya_ref, yb_ref, yc_ref, wo_ref, gx_ref, wq_ref, g64_ref, qg_ref,
                    km_ref, vm_ref, wmo_ref, o_ref):
    wa = ya_ref.shape[1]
    wb = wa + yb_ref.shape[1]
    x = (x_ref[...]
         + jnp.dot(ya_ref[...], wo_ref[:wa, :], preferred_element_type=_F32)
         + jnp.dot(yb_ref[...], wo_ref[wa:wb, :], preferred_element_type=_F32)
         + jnp.dot(yc_ref[...], wo_ref[wb:, :], preferred_element_type=_F32))
    h = _rms(x, gx_ref[...]).astype(_BF16)
    q = jnp.dot(h, wq_ref[...], preferred_element_type=_F32)
    q = (q * lax.rsqrt(_group_mean_sq(q, g64_ref[...], _MEM_DIM) + _EPS) * qg_ref[...]).astype(_BF16)
    km = km_ref[...]
    vm = vm_ref[...]
    head_of_lane = lax.broadcasted_iota(jnp.int32, (1, q.shape[1]), 1) // _MEM_DIM
    zero = jnp.zeros_like(q)
    o = jnp.zeros(q.shape, _F32)
    for hd in range(_MEM_HEADS):
        sel = head_of_lane == hd
        s = lax.dot_general(jnp.where(sel, q, zero), km, _NT, preferred_element_type=_F32)
        p = jnp.exp(s - jnp.max(s, axis=-1, keepdims=True))
        oh = jnp.dot(p.astype(_BF16), vm, preferred_element_type=_F32)
        o = jnp.where(sel, oh / jnp.sum(p, axis=-1, keepdims=True), o)
    o_ref[...] = x + jnp.dot(o.astype(_BF16), wmo_ref[...], preferred_element_type=_F32)


def _mix_mem(x2d, ya, yb, yc, lp, km, vm, seq):
    m, d = x2d.shape
    tm = _ROW_TILE
    per_seq = seq // tm
    n_mem, width = km.shape[1], km.shape[2]
    row = lambda c: pl.BlockSpec((tm, c), lambda i: (i, 0))
    mem = pl.BlockSpec((None, n_mem, width), lambda i: (i // per_seq, 0, 0))
    return pl.pallas_call(
        _mix_mem_kernel,
        grid=(m // tm,),
        in_specs=[row(d), row(ya.shape[1]), row(yb.shape[1]), row(yc.shape[1]),
                  _const_spec(lp["w_out"].shape), _const_spec((1, d)),
                  _const_spec(lp["w_mem_q"].shape), _const_spec((_MXU, _MXU)),
                  _const_spec((1, width)), mem, mem, _const_spec(lp["w_mem_o"].shape)],
        out_specs=row(d),
        out_shape=jax.ShapeDtypeStruct((m, d), _F32),
        compiler_params=_params("parallel"),
        name="mix_and_memory",
    )(x2d, ya, yb, yc, lp["w_out"], lp["memx_g"], lp["w_mem_q"], lp["g64"], lp["mem_qg"],
      km, vm, lp["w_mem_o"])


_FF_CHUNK = 1024


def _ffn_kernel(x_ref, g_ref, w1_ref, w2_ref, o_ref):
    x = x_ref[...]
    h = _rms(x, g_ref[...]).astype(_BF16)
    acc = x
    for c in range(0, w1_ref.shape[1], _FF_CHUNK):
        u = jnp.dot(h, w1_ref[:, c:c + _FF_CHUNK], preferred_element_type=_F32)
        r = jnp.maximum(u, 0.0)
        acc = acc + jnp.dot((r * r).astype(_BF16), w2_ref[c:c + _FF_CHUNK, :],
                            preferred_element_type=_F32)
    o_ref[...] = acc


def _ffn(x2d, g, w1, w2):
    m, d = x2d.shape
    tm = _ROW_TILE
    row = pl.BlockSpec((tm, d), lambda i: (i, 0))
    return pl.pallas_call(
        _ffn_kernel,
        grid=(m // tm,),
        in_specs=[row, _const_spec((1, d)), _const_spec(w1.shape), _const_spec(w2.shape)],
        out_specs=row,
        out_shape=jax.ShapeDtypeStruct((m, d), _F32),
        compiler_params=_params("parallel"),
        name="ffn",
    )(x2d, g, w1, w2)


def _t5_bucket(rel):
    nb = _NUM_BUCKETS // 2
    bucket = (rel > 0).astype(jnp.int32) * nb
    n = jnp.abs(rel)
    max_exact = nb // 2
    is_small = n < max_exact
    large = max_exact + (jnp.log(jnp.maximum(n, 1).astype(jnp.float32) / max_exact)
                         / math.log(_MAX_DISTANCE / max_exact) * (nb - max_exact)).astype(jnp.int32)
    large = jnp.minimum(large, nb - 1)
    return bucket + jnp.where(is_small, n, large)


def _da_bias_tables(rel_bias, t):
    assert t + 1 >= _MAX_DISTANCE and t % _CHUNK == 0
    j = jnp.arange(t, dtype=jnp.int32)[:, None]
    i = jnp.arange(t, dtype=jnp.int32)[None, :]
    rb = rel_bias.astype(_F32) * _LOG2E

    def lookup(bucket):
        hit = bucket[:, :, None, None] == jnp.arange(_NUM_BUCKETS, dtype=jnp.int32)[:, None]
        return jnp.sum(jnp.where(hit, rb[None, None], 0.0), axis=2)

    b0 = lookup(_t5_bucket(j - i))
    b0 = jnp.where(((j // _CHUNK) <= (i // _CHUNK))[:, :, None], b0, -jnp.inf)
    b1 = lookup(_t5_bucket(j - i - t))
    far = jnp.broadcast_to(lookup(_t5_bucket(jnp.full((1, 1), -(t + 1), jnp.int32))), b1.shape)
    tab = jnp.stack([b0, b1, far]).reshape(3, t, t, _DA_HEADS, 2)
    return tab.transpose(3, 0, 4, 1, 2)


def _rope_tables(seq):
    half = _MLA_ROPE // 2
    freqs = _ROPE_THETA ** (-jnp.arange(half, dtype=jnp.float32) / half)
    ang = jnp.arange(seq, dtype=jnp.int32).astype(jnp.float32)[:, None] * freqs[None, :]
    cos, sin = jnp.cos(ang), jnp.sin(ang)
    ones = jnp.ones((seq, _MLA_NOPE), _F32)
    z = lambda w: jnp.zeros((seq, w), _F32)
    tail = _LANES - _MLA_QK
    c = jnp.concatenate([ones, cos, cos, z(tail)], axis=1)
    s1 = jnp.concatenate([z(_MLA_NOPE + half), sin, z(tail)], axis=1)
    s2 = jnp.concatenate([z(_MLA_NOPE), -sin, z(half + tail)], axis=1)
    return c, s1, s2


def _group_ones(group):
    idx = np.arange(_MXU) // group
    return jnp.asarray(idx[:, None] == idx[None, :], dtype=_BF16)


def _layer_params(l, p, rope, g64):
    d = p["w_in"].shape[1]
    o = _IN_OFFS
    w_in = p["w_in"][l]
    kr = jnp.zeros((d, _LANES), _F32).at[:, _MLA_NOPE:_MLA_QK].set(w_in[:, o[8]:o[9]])
    w_uq = jnp.pad(p["w_mla_uq"][l].reshape(_MLA_Q_RANK, _MLA_HEADS, _MLA_QK),
                   ((0, 0), (0, 0), (0, _LANES - _MLA_QK))).reshape(_MLA_Q_RANK, -1)
    w_ukv = p["w_mla_ukv"][l].reshape(_MLA_KV_RANK, _MLA_HEADS, _MLA_NOPE + _MLA_V)
    w_k = jnp.pad(w_ukv[:, :, :_MLA_NOPE], ((0, 0), (0, 0), (0, _LANES - _MLA_NOPE)))
    w_v = w_ukv[:, :, _MLA_NOPE:]
    pad_g = lambda g: jnp.tile(jnp.pad(g, (0, _LANES - _MLA_QK)), _MLA_HEADS)[None]
    col = lambda g: jnp.broadcast_to(g[:, None], (g.shape[0], _ATT_TILE))
    lam_init = 0.8 - 0.6 * math.exp(-0.3 * l)
    lp = p["da_lambda"][l].astype(_F32)
    lam = jnp.exp(jnp.sum(lp[0] * lp[1])) - jnp.exp(jnp.sum(lp[2] * lp[3])) + lam_init
    return {
        "mix_g": p["mix_norm_g"][l][None],
        "w_in": jnp.concatenate([w_in[:, :o[8]], kr], axis=1).astype(_BF16),
        "g64": g64,
        "da_qg": jnp.tile(p["da_q_norm_g"][l], 2 * _DA_HEADS)[None] * (_DA_DIM ** -0.5 * _LOG2E),
        "da_kg": jnp.tile(p["da_k_norm_g"][l], 2 * _DA_HEADS)[None],
        "cq_g": p["mla_cq_norm_g"][l][None],
        "ckv_g": p["mla_ckv_norm_g"][l][None],
        "w_uq": w_uq.astype(_BF16),
        "w_ukv": jnp.concatenate([w_k.reshape(_MLA_KV_RANK, -1), w_v.reshape(_MLA_KV_RANK, -1)],
                                 axis=1).astype(_BF16),
        "mla_qg": pad_g(p["mla_q_norm_g"][l]) * (_MLA_QK ** -0.5 * _LOG2E),
        "mla_kg": pad_g(p["mla_k_norm_g"][l]),
        "rope_cos": rope[0], "rope_s1": rope[1], "rope_s2": rope[2],
        "lam": jnp.reshape(lam, (1,)).astype(_F32),
        "da_og": col(p["da_subln_g"][l] * (1.0 - lam_init)),
        "sb_og": col(jnp.tile(p["sb_out_g"][l], 2)),
        "mla_og": col(jnp.tile(p["mla_out_g"][l], 2)),
        "w_out": p["w_out"][l].astype(_BF16),
        "memx_g": p["memx_norm_g"][l][None],
        "w_mem_q": p["w_mem_q"][l].astype(_BF16),
        "mem_qg": jnp.tile(p["mem_q_norm_g"][l], _MEM_HEADS)[None] * (_MEM_DIM ** -0.5),
        "w_mem_o": p["w_mem_o"][l].astype(_BF16),
        "ffn_g": p["ffn_norm_g"][l][None],
        "w_ff1": p["w_ff1"][l].astype(_BF16),
        "w_ff2": p["w_ff2"][l].astype(_BF16),
    }


def kernel(x, mem, rel_bias, mix_norm_g, w_in, da_q_norm_g, da_k_norm_g, da_lambda, da_subln_g,
           sb_out_g, mla_cq_norm_g, mla_ckv_norm_g, w_mla_uq, w_mla_ukv, mla_q_norm_g, mla_k_norm_g,
           mla_out_g, w_out, memx_norm_g, mem_norm_g, w_mem_q, w_mem_kv, mem_q_norm_g, mem_k_norm_g,
           w_mem_o, ffn_norm_g, w_ff1, w_ff2):
    p = dict(mix_norm_g=mix_norm_g, w_in=w_in, da_q_norm_g=da_q_norm_g, da_k_norm_g=da_k_norm_g,
             da_lambda=da_lambda, da_subln_g=da_subln_g, sb_out_g=sb_out_g,
             mla_cq_norm_g=mla_cq_norm_g, mla_ckv_norm_g=mla_ckv_norm_g, w_mla_uq=w_mla_uq,
             w_mla_ukv=w_mla_ukv, mla_q_norm_g=mla_q_norm_g, mla_k_norm_g=mla_k_norm_g,
             mla_out_g=mla_out_g, w_out=w_out, memx_norm_g=memx_norm_g, w_mem_q=w_mem_q,
             mem_q_norm_g=mem_q_norm_g, w_mem_o=w_mem_o, ffn_norm_g=ffn_norm_g, w_ff1=w_ff1,
             w_ff2=w_ff2)
    batch, seq, d = x.shape
    depth = w_in.shape[0]
    n_mem = mem.shape[1]
    t = _ATT_TILE
    assert seq % _ROW_TILE == 0 and seq % t == 0 and (batch * n_mem) % _ROW_TILE == 0
    assert w_in.shape[2] == _IN_OFFS[-1]

    g64 = _group_ones(_MEM_DIM)
    rope = _rope_tables(seq)
    bias = _da_bias_tables(rel_bias, t)
    i = jnp.arange(t, dtype=jnp.int32)
    chunk_mask = jnp.where((i[:, None] // _CHUNK) <= (i[None, :] // _CHUNK), 0.0, -jnp.inf).astype(_F32)
    chunk_mask = jnp.stack([chunk_mask, jnp.zeros_like(chunk_mask)])

    width = _MEM_HEADS * _MEM_DIM
    km, vm = _memkv(mem.reshape(batch * n_mem, d), mem_norm_g[:, None, :], w_mem_kv.astype(_BF16), g64,
                    jnp.tile(mem_k_norm_g, (1, _MEM_HEADS))[:, None, :])
    km = km.reshape(depth, batch, n_mem, width)
    vm = vm.reshape(depth, batch, n_mem, width)

    x2d = x.reshape(batch * seq, d)
    for l in range(depth):
        lp = _layer_params(l, p, rope, g64)
        daq, dak, dav, sbq, sbk, sbv, mq, mk, mv = _inproj(x2d, lp, seq)
        ya = _da_attention(daq, dak, dav, bias, lp["lam"], lp["da_og"], batch, seq)
        yb = _sb_attention(sbq, sbk, sbv, lp["sb_og"], batch, seq)
        yc = _mla_attention(mq, mk, mv, chunk_mask, lp["mla_og"], batch, seq)
        x2d = _mix_mem(x2d, ya, yb, yc, lp, km[l], vm[l], seq)
        x2d = _ffn(x2d, lp["ffn_g"], lp["w_ff1"], lp["w_ff2"])
    return x2d.reshape(batch, seq, d)
```

```python
import functools
import math

import numpy as np
import jax
import jax.numpy as jnp
from jax import lax
from jax.experimental import pallas as pl
from jax.experimental.pallas import tpu as pltpu

_F32 = jnp.float32
_BF16 = jnp.bfloat16
_EPS = 1e-6

_CHUNK = 64
_DA_HEADS, _DA_DIM = 4, 64
_DA_VDIM = 2 * _DA_DIM
_SB_HEADS, _SB_DIM = 4, 64
_MLA_HEADS, _MLA_NOPE, _MLA_ROPE, _MLA_V = 4, 64, 32, 64
_MLA_QK = _MLA_NOPE + _MLA_ROPE
_MLA_Q_RANK, _MLA_KV_RANK = 256, 128
_ROPE_THETA = 10000.0
_NUM_BUCKETS, _MAX_DISTANCE = 32, 128
_MEM_HEADS, _MEM_DIM = 4, 64

_LANES = 128
_MXU = 256
_VMEM_LIMIT = 52 * 1024 * 1024

_ATT_TILE = 256
_ROW_TILE = 512

_IN_SIZES = (512, 512, 512, 256, 256, 256, _MLA_Q_RANK, _MLA_KV_RANK, _MLA_ROPE)
_IN_OFFS = tuple(int(v) for v in np.cumsum((0,) + _IN_SIZES))
_NT = (((1,), (1,)), ((), ()))
_LOG2E = math.log2(math.e)


def _const_spec(shape):
    zeros = (0,) * len(shape)
    return pl.BlockSpec(shape, lambda *_: zeros, pipeline_mode=pl.Buffered(1))


def _params(*sem):
    return pltpu.CompilerParams(dimension_semantics=sem, vmem_limit_bytes=_VMEM_LIMIT)


def _rms(x, g):
    return x * lax.rsqrt(jnp.mean(x * x, axis=-1, keepdims=True) + _EPS) * g


def _group_mean_sq(y, gmat, group):
    sq = (y * y).astype(_BF16)
    cols = y.shape[1]
    parts = [jnp.dot(sq[:, c:c + _MXU], gmat, preferred_element_type=_F32)
             for c in range(0, cols, _MXU)]
    ss = parts[0] if len(parts) == 1 else jnp.concatenate(parts, axis=1)
    return ss * (1.0 / group)


def _inproj_kernel(x_ref, gmix_ref, w_ref, g64_ref, gq_ref, gk_ref, cqg_ref, ckvg_ref,
                   wuq_ref, wukv_ref, qg_ref, kg_ref, cos_ref, s1_ref, s2_ref,
                   daq_ref, dak_ref, dav_ref, sbq_ref, sbk_ref, sbv_ref,
                   mq_ref, mk_ref, mv_ref):
    x = x_ref[...]
    h = _rms(x, gmix_ref[...]).astype(_BF16)
    o = _IN_OFFS

    def proj(seg, width=None):
        hi = o[seg + 1] if width is None else o[seg] + width
        return jnp.dot(h, w_ref[:, o[seg]:hi], preferred_element_type=_F32)

    g64 = g64_ref[...]
    t = daq_ref.shape[2]

    def norm64(y, g):
        return y * lax.rsqrt(_group_mean_sq(y, g64, _DA_DIM) + _EPS) * g

    def store_t(ref, y, row0=0):
        for r in range(y.shape[0] // t):
            ref[r, row0:row0 + y.shape[1], :] = y[r * t:(r + 1) * t, :].T.astype(_BF16)

    cos, s1, s2 = cos_ref[...], s1_ref[...], s2_ref[...]

    def head_norm_rope(y, g):
        ms = jnp.sum(y * y, axis=-1, keepdims=True) * (1.0 / _MLA_QK)
        yn = y * lax.rsqrt(ms + _EPS) * g
        half = _MLA_ROPE // 2
        return (yn * cos + pltpu.roll(yn, half, 1) * s1
                + pltpu.roll(yn, _LANES - half, 1) * s2)

    cq = _rms(proj(6), cqg_ref[...]).astype(_BF16)
    q_all = jnp.dot(cq, wuq_ref[...], preferred_element_type=_F32)
    ckv = _rms(proj(7), ckvg_ref[...]).astype(_BF16)
    kv_all = jnp.dot(ckv, wukv_ref[...], preferred_element_type=_F32)
    k_rope = proj(8, _LANES)

    def latent_head(hd):
        sl = slice(hd * _LANES, (hd + 1) * _LANES)
        store_t(mq_ref, head_norm_rope(q_all[:, sl], qg_ref[:, sl]), hd * _LANES)
        mk_ref[:, sl] = head_norm_rope(kv_all[:, sl] + k_rope, kg_ref[:, sl]).astype(_BF16)

    store_t(daq_ref, norm64(proj(0), gq_ref[...]))
    latent_head(0)
    dak_ref[...] = norm64(proj(1), gk_ref[...]).astype(_BF16)
    latent_head(1)
    store_t(dav_ref, proj(2))
    latent_head(2)
    store_t(sbq_ref, proj(3) * (_SB_DIM ** -0.5 * _LOG2E))
    latent_head(3)
    sbk_ref[...] = proj(4).astype(_BF16)
    store_t(mv_ref, kv_all[:, _MLA_HEADS * _LANES:])
    store_t(sbv_ref, proj(5))


def _inproj(x2d, lp, seq):
    m, d = x2d.shape
    tm = _ROW_TILE
    t = _ATT_TILE
    pos_blocks = seq // tm
    tab = pl.BlockSpec((tm, _LANES), lambda i: (i % pos_blocks, 0))
    widths = (512, 512, 512, 256, 256, 256, 512, 512, 256)
    transposed = (True, False, True, True, False, True, True, False, True)
    out_specs, out_shape = [], []
    for c, tr in zip(widths, transposed):
        if tr:
            out_specs.append(pl.BlockSpec((tm // t, c, t), lambda i: (i, 0, 0)))
            out_shape.append(jax.ShapeDtypeStruct((m // t, c, t), _BF16))
        else:
            out_specs.append(pl.BlockSpec((tm, c), lambda i: (i, 0)))
            out_shape.append(jax.ShapeDtypeStruct((m, c), _BF16))
    row = lambda c: pl.BlockSpec((tm, c), lambda i: (i, 0))
    return pl.pallas_call(
        _inproj_kernel,
        grid=(m // tm,),
        in_specs=[row(d), _const_spec((1, d)), _const_spec(lp["w_in"].shape),
                  _const_spec((_MXU, _MXU)), _const_spec((1, 512)), _const_spec((1, 512)),
                  _const_spec((1, _MLA_Q_RANK)), _const_spec((1, _MLA_KV_RANK)),
                  _const_spec(lp["w_uq"].shape), _const_spec(lp["w_ukv"].shape),
                  _const_spec((1, 512)), _const_spec((1, 512)), tab, tab, tab],
        out_specs=out_specs,
        out_shape=out_shape,
        compiler_params=_params("parallel"),
        name="inproj",
    )(x2d, lp["mix_g"], lp["w_in"], lp["g64"], lp["da_qg"], lp["da_kg"], lp["cq_g"],
      lp["ckv_g"], lp["w_uq"], lp["w_ukv"], lp["mla_qg"], lp["mla_kg"],
      lp["rope_cos"], lp["rope_s1"], lp["rope_s2"])


_SUM_ROWS = 16


def _pipelined_tiles(n_tiles, stage, consume, run, s_even, s_odd, stage_first=None):
    def pair(i, carry):
        run, summary = carry
        run = consume(2 * i, s_even, summary, run)
        summary = stage(2 * i + 1, s_odd)
        run = consume(2 * i + 1, s_odd, summary, run)
        return run, stage(2 * i + 2, s_even)

    def tail_two(_, carry):
        run, summary = carry
        run = consume(n_tiles - 2, s_even, summary, run)
        summary = stage(n_tiles - 1, s_odd)
        return consume(n_tiles - 1, s_odd, summary, run), summary

    def tail_one(_, carry):
        run, summary = carry
        return consume(n_tiles - 1, s_even, summary, run), summary

    carry = (run, (stage_first or stage)(0, s_even))
    last = n_tiles - 1
    odd = jnp.bitwise_and(last, 1)
    carry = lax.fori_loop(0, jnp.right_shift(last, 1), pair, carry)
    carry = lax.fori_loop(0, odd, tail_two, carry)
    lax.fori_loop(0, 1 - odd, tail_one, carry)


def _online_softmax(n_tiles, score, value, s_even, s_odd, acc_scr):
    n_chains, _, t = s_even.shape
    ones = jnp.ones((_SUM_ROWS, t), _BF16)
    chains = range(n_chains)

    def stage(ki, buf):
        col_max = []
        for c in chains:
            s = score(c, ki)
            buf[c] = s
            col_max.append(jnp.max(s, axis=0, keepdims=True))
        return tuple(col_max)

    def consume(ki, buf, col_max, m_run):
        out = []
        for c in chains:
            m_new = jnp.maximum(m_run[c], col_max[c])
            alpha = jnp.exp2(m_run[c] - m_new)
            p = jnp.exp2(buf[c] - m_new).astype(_BF16)
            v_ones = jnp.concatenate([value(c, ki), ones], axis=0)
            acc_scr[c] = alpha * acc_scr[c] + jnp.dot(v_ones, p, preferred_element_type=_F32)
            out.append(m_new)
        return tuple(out)

    for c in chains:
        acc_scr[c] = jnp.zeros(acc_scr.shape[1:], _F32)
    m_init = tuple(jnp.full((1, t), -jnp.inf, _F32) for _ in chains)
    _pipelined_tiles(n_tiles, stage, consume, m_init, s_even, s_odd)


def _softmax_finish(acc, rows):
    width = acc.shape[0] - _SUM_ROWS
    return acc[rows] * (1.0 / acc[width:width + 1])


def _rows(i, t, base=0):
    return pl.ds(pl.multiple_of(base + i * t, t), t)


def _split_rows(x_t, half):
    zero = jnp.zeros((half, x_t.shape[1]), x_t.dtype)
    return (jnp.concatenate([x_t[:half], zero], axis=0),
            jnp.concatenate([zero, x_t[half:]], axis=0))


def _half_row_norm(o_t, half, g_t):
    sq = o_t * o_t
    lo = lax.rsqrt(jnp.mean(sq[:half], axis=0, keepdims=True) + _EPS)
    hi = lax.rsqrt(jnp.mean(sq[half:], axis=0, keepdims=True) + _EPS)
    return jnp.concatenate([o_t[:half] * lo, o_t[half:] * hi], axis=0) * g_t


def _da_kernel(lam_ref, q_ref, k_ref, v_ref, bias_ref, g_ref, o_ref, s_even, s_odd, acc_scr, *, t):
    nq = q_ref.shape[0]
    lam = lam_ref[0]

    heads = range(_DA_HEADS)
    head = lambda hd: slice(hd * _LANES, (hd + 1) * _LANES)
    all_rows = slice(0, _DA_VDIM)

    def q_tile(qi, carry):
        qm = [qh for hd in heads for qh in _split_rows(q_ref[qi, head(hd), :], _DA_DIM)]

        def score(c, ki):
            hd, mi = divmod(c, 2)
            d = jnp.minimum(qi - ki, 2)
            return (jnp.dot(k_ref[_rows(ki, t), head(hd)], qm[c], preferred_element_type=_F32)
                    + bias_ref[hd, d, mi])

        value = lambda c, ki: v_ref[ki, head(c // 2), :]
        _online_softmax(qi + 1, score, value, s_even, s_odd, acc_scr)
        for hd in heads:
            o_t = (_softmax_finish(acc_scr[2 * hd], all_rows)
                   - lam * _softmax_finish(acc_scr[2 * hd + 1], all_rows))
            o_t = o_t * lax.rsqrt(jnp.mean(o_t * o_t, axis=0, keepdims=True) + _EPS) * g_ref[...]
            o_ref[_rows(qi, t), head(hd)] = o_t.T.astype(o_ref.dtype)
        return carry

    lax.fori_loop(0, nq, q_tile, 0)


def _da_attention(q_t, k, v_t, bias, lam, g_t, batch, seq):
    t = _ATT_TILE
    nq = seq // t
    width = _DA_HEADS * _LANES
    tiles = pl.BlockSpec((nq, width, t), lambda b: (b, 0, 0))
    rows = pl.BlockSpec((seq, width), lambda b: (b, 0))
    return pl.pallas_call(
        functools.partial(_da_kernel, t=t),
        grid=(batch,),
        in_specs=[pl.BlockSpec(memory_space=pltpu.SMEM), tiles, rows, tiles,
                  _const_spec(bias.shape), _const_spec((_LANES, t))],
        out_specs=rows,
        out_shape=jax.ShapeDtypeStruct(k.shape, _BF16),
        scratch_shapes=[pltpu.VMEM((2 * _DA_HEADS, t, t), _F32), pltpu.VMEM((2 * _DA_HEADS, t, t), _F32),
                        pltpu.VMEM((2 * _DA_HEADS, _DA_VDIM + _SUM_ROWS, t), _F32)],
        compiler_params=_params("parallel"),
        name="diff_attention",
    )(lam, q_t, k, v_t, bias, g_t)


_PAIR_BATCH = 2


def _mla_kernel(q_ref, k_ref, v_ref, mask_ref, g_ref, o_ref, s_even, s_odd, acc_scr, *, t, nb):
    nq = q_ref.shape[0] // nb
    seq = k_ref.shape[0] // nb

    head = lambda hd: slice(hd * _LANES, (hd + 1) * _LANES)
    pair = lambda hd: slice((hd // 2) * _LANES, (hd // 2 + 1) * _LANES)
    chains = [(bb, hd) for bb in range(nb) for hd in range(_MLA_HEADS)]

    def q_tile(qi, carry):
        qs = [q_ref[bb * nq + qi, head(hd), :] for bb, hd in chains]

        def score(c, ki):
            bb, hd = chains[c]
            d = jnp.minimum(qi - ki, 1)
            return (jnp.dot(k_ref[_rows(ki, t, bb * seq), head(hd)], qs[c], preferred_element_type=_F32)
                    + mask_ref[d])

        value = lambda c, ki: v_ref[chains[c][0] * nq + ki, pair(chains[c][1]), :]
        _online_softmax(qi + 1, score, value, s_even, s_odd, acc_scr)
        for bb in range(nb):
            for p in range(_MLA_HEADS // 2):
                c = bb * _MLA_HEADS + 2 * p
                o_t = jnp.concatenate([_softmax_finish(acc_scr[c], slice(0, _MLA_V)),
                                       _softmax_finish(acc_scr[c + 1], slice(_MLA_V, _LANES))], axis=0)
                o_ref[_rows(qi, t, bb * seq), head(p)] = (
                    _half_row_norm(o_t, _MLA_V, g_ref[...]).T.astype(o_ref.dtype))
        return carry

    lax.fori_loop(0, nq, q_tile, 0)


def _mla_attention(q_t, k, v_t, mask_t, g_t, batch, seq):
    t = _ATT_TILE
    nb = _PAIR_BATCH
    nq = seq // t
    qk_width = _MLA_HEADS * _LANES
    v_width = _MLA_HEADS * _MLA_V
    chains = nb * _MLA_HEADS
    return pl.pallas_call(
        functools.partial(_mla_kernel, t=t, nb=nb),
        grid=(batch // nb,),
        in_specs=[pl.BlockSpec((nb * nq, qk_width, t), lambda b: (b, 0, 0)),
                  pl.BlockSpec((nb * seq, qk_width), lambda b: (b, 0)),
                  pl.BlockSpec((nb * nq, v_width, t), lambda b: (b, 0, 0)),
                  _const_spec((2, t, t)), _const_spec((_LANES, t))],
        out_specs=pl.BlockSpec((nb * seq, v_width), lambda b: (b, 0)),
        out_shape=jax.ShapeDtypeStruct((batch * seq, v_width), _BF16),
        scratch_shapes=[pltpu.VMEM((chains, t, t), _F32), pltpu.VMEM((chains, t, t), _F32),
                        pltpu.VMEM((chains, _LANES + _SUM_ROWS, t), _F32)],
        compiler_params=_params("parallel"),
        name="latent_attention",
    )(q_t, k, v_t, mask_t, g_t)


def _sb_kernel(q_ref, k_ref, v_ref, g_ref, o_ref, s_even, s_odd, acc_scr, *, t, nb):
    nq = q_ref.shape[0] // nb
    seq = k_ref.shape[0] // nb
    key_idx = lax.broadcasted_iota(jnp.int32, (t, t), 0)
    query_idx = lax.broadcasted_iota(jnp.int32, (t, t), 1)
    earlier = key_idx < query_idx
    tri = jnp.where(earlier, 1.0, 0.0).astype(_BF16)
    tri2 = jnp.concatenate([tri, tri], axis=1)

    pair = lambda hd: slice((hd // 2) * _LANES, (hd // 2 + 1) * _LANES)
    dot = functools.partial(jnp.dot, preferred_element_type=_F32)
    chains = [(bb, hd) for bb in range(nb) for hd in range(_SB_HEADS)]
    n_chains = range(len(chains))

    def q_tile(qi, carry):
        qh = [q for bb in range(nb) for p in range(_SB_HEADS // 2)
              for q in _split_rows(q_ref[bb * nq + qi, pair(2 * p), :], _SB_DIM)]

        def stage(j, buf, diag=False):
            z = [dot(k_ref[_rows(qi - j, t, bb * seq), pair(hd)], qh[c])
                 for c, (bb, hd) in enumerate(chains)]
            log_beta, first_row, later = [], [], []
            for zi in z:
                lp = jnp.log2(1.0 + jnp.exp2(jnp.minimum(zi, -zi)))
                lb = jnp.minimum(zi, 0.0) - lp
                log_1m = lb - zi
                if diag:
                    log_1m = jnp.where(earlier, log_1m, 0.0)
                l_hi = log_1m.astype(_BF16)
                l_lo = (log_1m - l_hi.astype(_F32)).astype(_BF16)
                later.append(dot(tri2, jnp.concatenate([l_hi, l_lo], axis=0)))
                log_beta.append(lb)
                first_row.append(log_1m[0:1, :])
            for c in n_chains:
                log_w = log_beta[c] + later[c]
                buf[c] = jnp.where(earlier, log_w, -jnp.inf) if diag else log_w
            return tuple(la[0:1, :] + fr for la, fr in zip(later, first_row))

        def consume(j, buf, through, run):
            for c, (bb, hd) in enumerate(chains):
                a = jnp.exp2(buf[c] + run[c]).astype(_BF16)
                acc_scr[c] = acc_scr[c] + dot(v_ref[bb * nq + qi - j, pair(hd), :], a)
            return tuple(r + th for r, th in zip(run, through))

        for c in n_chains:
            acc_scr[c] = jnp.zeros(acc_scr.shape[1:], _F32)
        run0 = tuple(jnp.zeros((1, t), _F32) for _ in n_chains)
        _pipelined_tiles(qi + 1, stage, consume, run0, s_even, s_odd,
                         stage_first=functools.partial(stage, diag=True))
        for bb in range(nb):
            for p in range(_SB_HEADS // 2):
                c = bb * _SB_HEADS + 2 * p
                o_t = jnp.concatenate([acc_scr[c, :_SB_DIM, :], acc_scr[c + 1, _SB_DIM:, :]], axis=0)
                o_ref[_rows(qi, t, bb * seq), pair(2 * p)] = (
                    _half_row_norm(o_t, _SB_DIM, g_ref[...]).T.astype(o_ref.dtype))
        return carry

    lax.fori_loop(0, nq, q_tile, 0)


def _sb_attention(q_t, k, v_t, g_t, batch, seq):
    t = _ATT_TILE
    nb = _PAIR_BATCH
    nq = seq // t
    width = _SB_HEADS * _SB_DIM
    chains = nb * _SB_HEADS
    tiles = pl.BlockSpec((nb * nq, width, t), lambda b: (b, 0, 0))
    rows = pl.BlockSpec((nb * seq, width), lambda b: (b, 0))
    return pl.pallas_call(
        functools.partial(_sb_kernel, t=t, nb=nb),
        grid=(batch // nb,),
        in_specs=[tiles, rows, tiles, _const_spec((_LANES, t))],
        out_specs=rows,
        out_shape=jax.ShapeDtypeStruct(k.shape, _BF16),
        scratch_shapes=[pltpu.VMEM((chains, t, t), _F32), pltpu.VMEM((chains, t, t), _F32),
                        pltpu.VMEM((chains, _LANES, t), _F32)],
        compiler_params=_params("parallel"),
        name="stick_breaking",
    )(q_t, k, v_t, g_t)


def _memkv_kernel(mem_ref, g_ref, w_ref, g64_ref, kg_ref, k_ref, v_ref):
    m = _rms(mem_ref[...], g_ref[...]).astype(_BF16)
    kv = jnp.dot(m, w_ref[...], preferred_element_type=_F32)
    width = _MEM_HEADS * _MEM_DIM
    k = kv[:, :width]
    k = k * lax.rsqrt(_group_mean_sq(k, g64_ref[...], _MEM_DIM) + _EPS) * kg_ref[...]
    k_ref[...] = k.astype(_BF16)
    v_ref[...] = kv[:, width:].astype(_BF16)


def _memkv(mem2d, g, w, g64, kg):
    depth = w.shape[0]
    rows, d = mem2d.shape
    tm = _ROW_TILE
    width = _MEM_HEADS * _MEM_DIM
    out = pl.BlockSpec((None, tm, width), lambda l, i: (l, i, 0))
    return pl.pallas_call(
        _memkv_kernel,
        grid=(depth, rows // tm),
        in_specs=[pl.BlockSpec((tm, d), lambda l, i: (i, 0)),
                  pl.BlockSpec((None, 1, d), lambda l, i: (l, 0, 0)),
                  pl.BlockSpec((None, d, 2 * width), lambda l, i: (l, 0, 0)),
                  _const_spec((_MXU, _MXU)),
                  pl.BlockSpec((None, 1, width), lambda l, i: (l, 0, 0))],
        out_specs=[out, out],
        out_shape=[jax.ShapeDtypeStruct((depth, rows, width), _BF16)] * 2,
        compiler_params=_params("parallel", "parallel"),
        name="memory_kv",
    )(mem2d, g, w, g64, kg)


_MIX_ROWS = 256


def _mix_mem_kernel(x_ref, ya_ref, yb_ref, yc_ref, wo_ref, gx_ref, wq_ref, g64_ref, qg_ref,
                    km_ref, vm_ref, wmo_ref, o_ref):
    km = km_ref[...]
    vm = vm_ref[...]
    head_of_lane = lax.broadcasted_iota(jnp.int32, (1, km.shape[1]), 1) // _MEM_DIM
    groups = [pl.ds(r, _MIX_ROWS) for r in range(0, x_ref.shape[0], _MIX_ROWS)]
    x = [x_ref[r, :] + jnp.dot(jnp.concatenate([ya_ref[r, :], yb_ref[r, :], yc_ref[r, :]], axis=1),
                               wo_ref[...], preferred_element_type=_F32) for r in groups]
    q = [jnp.dot(_rms(xi, gx_ref[...]).astype(_BF16), wq_ref[...], preferred_element_type=_F32)
         for xi in x]
    q = [(qi * lax.rsqrt(_group_mean_sq(qi, g64_ref[...], _MEM_DIM) + _EPS) * qg_ref[...]).astype(_BF16)
         for qi in q]
    o = [jnp.zeros(qi.shape, _F32) for qi in q]
    for hd in range(_MEM_HEADS):
        sel = head_of_lane == hd
        s = [lax.dot_general(jnp.where(sel, qi, jnp.zeros_like(qi)), km, _NT, preferred_element_type=_F32)
             for qi in q]
        p = [jnp.exp(si - jnp.max(si, axis=-1, keepdims=True)) for si in s]
        oh = [jnp.dot(pi.astype(_BF16), vm, preferred_element_type=_F32) for pi in p]
        o = [jnp.where(sel, ohi / jnp.sum(pi, axis=-1, keepdims=True), oi) for ohi, pi, oi in zip(oh, p, o)]
    for r, xi, oi in zip(groups, x, o):
        o_ref[r, :] = xi + jnp.dot(oi.astype(_BF16), wmo_ref[...], preferred_element_type=_F32)


def _mix_mem(x2d, ya, yb, yc, lp, km, vm, seq):
    m, d = x2d.shape
    tm = _ROW_TILE
    per_seq = seq // tm
    n_mem, width = km.shape[1], km.shape[2]
    row = lambda c: pl.BlockSpec((tm, c), lambda i: (i, 0))
    mem = pl.BlockSpec((None, n_mem, width), lambda i: (i // per_seq, 0, 0))
    return pl.pallas_call(
        _mix_mem_kernel,
        grid=(m // tm,),
        in_specs=[row(d), row(ya.shape[1]), row(yb.shape[1]), row(yc.shape[1]),
                  _const_spec(lp["w_out"].shape), _const_spec((1, d)),
                  _const_spec(lp["w_mem_q"].shape), _const_spec((_MXU, _MXU)),
                  _const_spec((1, width)), mem, mem, _const_spec(lp["w_mem_o"].shape)],
        out_specs=row(d),
        out_shape=jax.ShapeDtypeStruct((m, d), _F32),
        compiler_params=_params("parallel"),
        name="mix_and_memory",
    )(x2d, ya, yb, yc, lp["w_out"], lp["memx_g"], lp["w_mem_q"], lp["g64"], lp["mem_qg"],
      km, vm, lp["w_mem_o"])


_FF_CHUNK = 1024


def _ffn_kernel(x_ref, g_ref, w1_ref, w2_ref, o_ref):
    x = x_ref[...]
    h = _rms(x, g_ref[...]).astype(_BF16)
    acc = x
    for c in range(0, w1_ref.shape[1], _FF_CHUNK):
        u = jnp.dot(h, w1_ref[:, c:c + _FF_CHUNK], preferred_element_type=_F32)
        r = jnp.maximum(u, 0.0)
        acc = acc + jnp.dot((r * r).astype(_BF16), w2_ref[c:c + _FF_CHUNK, :],
                            preferred_element_type=_F32)
    o_ref[...] = acc


def _ffn(x2d, g, w1, w2):
    m, d = x2d.shape
    tm = _ROW_TILE
    row = pl.BlockSpec((tm, d), lambda i: (i, 0))
    return pl.pallas_call(
        _ffn_kernel,
        grid=(m // tm,),
        in_specs=[row, _const_spec((1, d)), _const_spec(w1.shape), _const_spec(w2.shape)],
        out_specs=row,
        out_shape=jax.ShapeDtypeStruct((m, d), _F32),
        compiler_params=_params("parallel"),
        name="ffn",
    )(x2d, g, w1, w2)


def _t5_bucket(rel):
    nb = _NUM_BUCKETS // 2
    bucket = (rel > 0).astype(jnp.int32) * nb
    n = jnp.abs(rel)
    max_exact = nb // 2
    is_small = n < max_exact
    large = max_exact + (jnp.log(jnp.maximum(n, 1).astype(jnp.float32) / max_exact)
                         / math.log(_MAX_DISTANCE / max_exact) * (nb - max_exact)).astype(jnp.int32)
    large = jnp.minimum(large, nb - 1)
    return bucket + jnp.where(is_small, n, large)


def _da_bias_tables(rel_bias, t):
    assert t + 1 >= _MAX_DISTANCE and t % _CHUNK == 0
    j = jnp.arange(t, dtype=jnp.int32)[:, None]
    i = jnp.arange(t, dtype=jnp.int32)[None, :]
    rb = rel_bias.astype(_F32) * _LOG2E

    def lookup(bucket):
        hit = bucket[:, :, None, None] == jnp.arange(_NUM_BUCKETS, dtype=jnp.int32)[:, None]
        return jnp.sum(jnp.where(hit, rb[None, None], 0.0), axis=2)

    b0 = lookup(_t5_bucket(j - i))
    b0 = jnp.where(((j // _CHUNK) <= (i // _CHUNK))[:, :, None], b0, -jnp.inf)
    b1 = lookup(_t5_bucket(j - i - t))
    far = jnp.broadcast_to(lookup(_t5_bucket(jnp.full((1, 1), -(t + 1), jnp.int32))), b1.shape)
    tab = jnp.stack([b0, b1, far]).reshape(3, t, t, _DA_HEADS, 2)
    return tab.transpose(3, 0, 4, 1, 2)


def _rope_tables(seq):
    half = _MLA_ROPE // 2
    freqs = _ROPE_THETA ** (-jnp.arange(half, dtype=jnp.float32) / half)
    ang = jnp.arange(seq, dtype=jnp.int32).astype(jnp.float32)[:, None] * freqs[None, :]
    cos, sin = jnp.cos(ang), jnp.sin(ang)
    ones = jnp.ones((seq, _MLA_NOPE), _F32)
    z = lambda w: jnp.zeros((seq, w), _F32)
    tail = _LANES - _MLA_QK
    c = jnp.concatenate([ones, cos, cos, z(tail)], axis=1)
    s1 = jnp.concatenate([z(_MLA_NOPE + half), sin, z(tail)], axis=1)
    s2 = jnp.concatenate([z(_MLA_NOPE), -sin, z(half + tail)], axis=1)
    return c, s1, s2


def _group_ones(group):
    idx = np.arange(_MXU) // group
    return jnp.asarray(idx[:, None] == idx[None, :], dtype=_BF16)


def _layer_params(l, p, rope, g64):
    d = p["w_in"].shape[1]
    o = _IN_OFFS
    w_in = p["w_in"][l]
    kr = jnp.zeros((d, _LANES), _F32).at[:, _MLA_NOPE:_MLA_QK].set(w_in[:, o[8]:o[9]])
    w_uq = jnp.pad(p["w_mla_uq"][l].reshape(_MLA_Q_RANK, _MLA_HEADS, _MLA_QK),
                   ((0, 0), (0, 0), (0, _LANES - _MLA_QK))).reshape(_MLA_Q_RANK, -1)
    w_ukv = p["w_mla_ukv"][l].reshape(_MLA_KV_RANK, _MLA_HEADS, _MLA_NOPE + _MLA_V)
    w_k = jnp.pad(w_ukv[:, :, :_MLA_NOPE], ((0, 0), (0, 0), (0, _LANES - _MLA_NOPE)))
    w_v = w_ukv[:, :, _MLA_NOPE:]
    pad_g = lambda g: jnp.tile(jnp.pad(g, (0, _LANES - _MLA_QK)), _MLA_HEADS)[None]
    col = lambda g: jnp.broadcast_to(g[:, None], (g.shape[0], _ATT_TILE))
    lam_init = 0.8 - 0.6 * math.exp(-0.3 * l)
    lp = p["da_lambda"][l].astype(_F32)
    lam = jnp.exp(jnp.sum(lp[0] * lp[1])) - jnp.exp(jnp.sum(lp[2] * lp[3])) + lam_init
    return {
        "mix_g": p["mix_norm_g"][l][None],
        "w_in": jnp.concatenate([w_in[:, :o[8]], kr], axis=1).astype(_BF16),
        "g64": g64,
        "da_qg": jnp.tile(p["da_q_norm_g"][l], 2 * _DA_HEADS)[None] * (_DA_DIM ** -0.5 * _LOG2E),
        "da_kg": jnp.tile(p["da_k_norm_g"][l], 2 * _DA_HEADS)[None],
        "cq_g": p["mla_cq_norm_g"][l][None],
        "ckv_g": p["mla_ckv_norm_g"][l][None],
        "w_uq": w_uq.astype(_BF16),
        "w_ukv": jnp.concatenate([w_k.reshape(_MLA_KV_RANK, -1), w_v.reshape(_MLA_KV_RANK, -1)],
                                 axis=1).astype(_BF16),
        "mla_qg": pad_g(p["mla_q_norm_g"][l]) * (_MLA_QK ** -0.5 * _LOG2E),
        "mla_kg": pad_g(p["mla_k_norm_g"][l]),
        "rope_cos": rope[0], "rope_s1": rope[1], "rope_s2": rope[2],
        "lam": jnp.reshape(lam, (1,)).astype(_F32),
        "da_og": col(p["da_subln_g"][l] * (1.0 - lam_init)),
        "sb_og": col(jnp.tile(p["sb_out_g"][l], 2)),
        "mla_og": col(jnp.tile(p["mla_out_g"][l], 2)),
        "w_out": p["w_out"][l].astype(_BF16),
        "memx_g": p["memx_norm_g"][l][None],
        "w_mem_q": p["w_mem_q"][l].astype(_BF16),
        "mem_qg": jnp.tile(p["mem_q_norm_g"][l], _MEM_HEADS)[None] * (_MEM_DIM ** -0.5),
        "w_mem_o": p["w_mem_o"][l].astype(_BF16),
        "ffn_g": p["ffn_norm_g"][l][None],
        "w_ff1": p["w_ff1"][l].astype(_BF16),
        "w_ff2": p["w_ff2"][l].astype(_BF16),
    }


def kernel(x, mem, rel_bias, mix_norm_g, w_in, da_q_norm_g, da_k_norm_g, da_lambda, da_subln_g,
           sb_out_g, mla_cq_norm_g, mla_ckv_norm_g, w_mla_uq, w_mla_ukv, mla_q_norm_g, mla_k_norm_g,
           mla_out_g, w_out, memx_norm_g, mem_norm_g, w_mem_q, w_mem_kv, mem_q_norm_g, mem_k_norm_g,
           w_mem_o, ffn_norm_g, w_ff1, w_ff2):
    p = dict(mix_norm_g=mix_norm_g, w_in=w_in, da_q_norm_g=da_q_norm_g, da_k_norm_g=da_k_norm_g,
             da_lambda=da_lambda, da_subln_g=da_subln_g, sb_out_g=sb_out_g,
             mla_cq_norm_g=mla_cq_norm_g, mla_ckv_norm_g=mla_ckv_norm_g, w_mla_uq=w_mla_uq,
             w_mla_ukv=w_mla_ukv, mla_q_norm_g=mla_q_norm_g, mla_k_norm_g=mla_k_norm_g,
             mla_out_g=mla_out_g, w_out=w_out, memx_norm_g=memx_norm_g, w_mem_q=w_mem_q,
             mem_q_norm_g=mem_q_norm_g, w_mem_o=w_mem_o, ffn_norm_g=ffn_norm_g, w_ff1=w_ff1,
             w_ff2=w_ff2)
    batch, seq, d = x.shape
    depth = w_in.shape[0]
    n_mem = mem.shape[1]
    t = _ATT_TILE
    assert seq % _ROW_TILE == 0 and seq % t == 0 and (batch * n_mem) % _ROW_TILE == 0
    assert batch % _PAIR_BATCH == 0
    assert w_in.shape[2] == _IN_OFFS[-1]

    g64 = _group_ones(_MEM_DIM)
    rope = _rope_tables(seq)
    bias = _da_bias_tables(rel_bias, t)
    i = jnp.arange(t, dtype=jnp.int32)
    chunk_mask = jnp.where((i[:, None] // _CHUNK) <= (i[None, :] // _CHUNK), 0.0, -jnp.inf).astype(_F32)
    chunk_mask = jnp.stack([chunk_mask, jnp.zeros_like(chunk_mask)])

    width = _MEM_HEADS * _MEM_DIM
    km, vm = _memkv(mem.reshape(batch * n_mem, d), mem_norm_g[:, None, :], w_mem_kv.astype(_BF16), g64,
                    jnp.tile(mem_k_norm_g, (1, _MEM_HEADS))[:, None, :])
    km = km.reshape(depth, batch, n_mem, width)
    vm = vm.reshape(depth, batch, n_mem, width)

    x2d = x.reshape(batch * seq, d)
    for l in range(depth):
        lp = _layer_params(l, p, rope, g64)
        daq, dak, dav, sbq, sbk, sbv, mq, mk, mv = _inproj(x2d, lp, seq)
        ya = _da_attention(daq, dak, dav, bias, lp["lam"], lp["da_og"], batch, seq)
        yb = _sb_attention(sbq, sbk, sbv, lp["sb_og"], batch, seq)
        yc = _mla_attention(mq, mk, mv, chunk_mask, lp["mla_og"], batch, seq)
        x2d = _mix_mem(x2d, ya, yb, yc, lp, km[l], vm[l], seq)
        x2d = _ffn(x2d, lp["ffn_g"], lp["w_ff1"], lp["w_ff2"])
    return x2d.reshape(batch, seq, d)
```

```python
import functools
import math

import numpy as np
import jax
import jax.numpy as jnp
from jax import lax
from jax.experimental import pallas as pl
from jax.experimental.pallas import tpu as pltpu

_F32 = jnp.float32
_BF16 = jnp.bfloat16
_EPS = 1e-6

_CHUNK = 64
_DA_HEADS, _DA_DIM = 4, 64
_DA_VDIM = 2 * _DA_DIM
_SB_HEADS, _SB_DIM = 4, 64
_MLA_HEADS, _MLA_NOPE, _MLA_ROPE, _MLA_V = 4, 64, 32, 64
_MLA_QK = _MLA_NOPE + _MLA_ROPE
_MLA_Q_RANK, _MLA_KV_RANK = 256, 128
_ROPE_THETA = 10000.0
_NUM_BUCKETS, _MAX_DISTANCE = 32, 128
_MEM_HEADS, _MEM_DIM = 4, 64

_LANES = 128
_MXU = 256
_VMEM_LIMIT = 52 * 1024 * 1024

_ATT_TILE = 256
_ROW_TILE = 512
_MIX_ROWS = 1024

_IN_SIZES = (512, 512, 512, 256, 256, 256, _MLA_Q_RANK, _MLA_KV_RANK, _MLA_ROPE)
_IN_OFFS = tuple(int(v) for v in np.cumsum((0,) + _IN_SIZES))
_NT = (((1,), (1,)), ((), ()))
_LOG2E = math.log2(math.e)


def _const_spec(shape):
    zeros = (0,) * len(shape)
    return pl.BlockSpec(shape, lambda *_: zeros, pipeline_mode=pl.Buffered(1))


def _params(*sem):
    return pltpu.CompilerParams(dimension_semantics=sem, vmem_limit_bytes=_VMEM_LIMIT)


def _rms(x, g):
    return x * lax.rsqrt(jnp.mean(x * x, axis=-1, keepdims=True) + _EPS) * g


def _group_mean_sq(y, gmat, group):
    sq = (y * y).astype(_BF16)
    cols = y.shape[1]
    parts = [jnp.dot(sq[:, c:c + _MXU], gmat, preferred_element_type=_F32)
             for c in range(0, cols, _MXU)]
    ss = parts[0] if len(parts) == 1 else jnp.concatenate(parts, axis=1)
    return ss * (1.0 / group)


def _inproj_kernel(x_ref, gmix_ref, w_ref, g64_ref, gq_ref, gk_ref, cqg_ref, ckvg_ref,
                   wuq_ref, wukv_ref, qg_ref, kg_ref, cos_ref, s1_ref, s2_ref,
                   daq_ref, dak_ref, dav_ref, sbq_ref, sbk_ref, sbv_ref,
                   mq_ref, mk_ref, mv_ref):
    x = x_ref[...]
    h = _rms(x, gmix_ref[...]).astype(_BF16)
    o = _IN_OFFS

    def proj(seg, width=None):
        hi = o[seg + 1] if width is None else o[seg] + width
        return jnp.dot(h, w_ref[:, o[seg]:hi], preferred_element_type=_F32)

    g64 = g64_ref[...]
    t = daq_ref.shape[2]

    def norm64(y, g):
        return y * lax.rsqrt(_group_mean_sq(y, g64, _DA_DIM) + _EPS) * g

    def store_t(ref, y, row0=0):
        for r in range(y.shape[0] // t):
            ref[r, row0:row0 + y.shape[1], :] = y[r * t:(r + 1) * t, :].T.astype(_BF16)

    cos, s1, s2 = cos_ref[...], s1_ref[...], s2_ref[...]

    def head_norm_rope(y, g):
        ms = jnp.sum(y * y, axis=-1, keepdims=True) * (1.0 / _MLA_QK)
        yn = y * lax.rsqrt(ms + _EPS) * g
        half = _MLA_ROPE // 2
        return (yn * cos + pltpu.roll(yn, half, 1) * s1
                + pltpu.roll(yn, _LANES - half, 1) * s2)

    cq = _rms(proj(6), cqg_ref[...]).astype(_BF16)
    q_all = jnp.dot(cq, wuq_ref[...], preferred_element_type=_F32)
    ckv = _rms(proj(7), ckvg_ref[...]).astype(_BF16)
    kv_all = jnp.dot(ckv, wukv_ref[...], preferred_element_type=_F32)
    k_rope = proj(8, _LANES)

    def latent_head(hd):
        sl = slice(hd * _LANES, (hd + 1) * _LANES)
        store_t(mq_ref, head_norm_rope(q_all[:, sl], qg_ref[:, sl]), hd * _LANES)
        mk_ref[:, sl] = head_norm_rope(kv_all[:, sl] + k_rope, kg_ref[:, sl]).astype(_BF16)

    store_t(daq_ref, norm64(proj(0), gq_ref[...]))
    latent_head(0)
    dak_ref[...] = norm64(proj(1), gk_ref[...]).astype(_BF16)
    latent_head(1)
    store_t(dav_ref, proj(2))
    latent_head(2)
    store_t(sbq_ref, proj(3) * (_SB_DIM ** -0.5 * _LOG2E))
    latent_head(3)
    sbk_ref[...] = proj(4).astype(_BF16)
    store_t(mv_ref, kv_all[:, _MLA_HEADS * _LANES:])
    store_t(sbv_ref, proj(5))


def _inproj(x2d, lp, seq):
    m, d = x2d.shape
    tm = _ROW_TILE
    t = _ATT_TILE
    pos_blocks = seq // tm
    tab = pl.BlockSpec((tm, _LANES), lambda i: (i % pos_blocks, 0))
    widths = (512, 512, 512, 256, 256, 256, 512, 512, 256)
    transposed = (True, False, True, True, False, True, True, False, True)
    out_specs, out_shape = [], []
    for c, tr in zip(widths, transposed):
        if tr:
            out_specs.append(pl.BlockSpec((tm // t, c, t), lambda i: (i, 0, 0)))
            out_shape.append(jax.ShapeDtypeStruct((m // t, c, t), _BF16))
        else:
            out_specs.append(pl.BlockSpec((tm, c), lambda i: (i, 0)))
            out_shape.append(jax.ShapeDtypeStruct((m, c), _BF16))
    row = lambda c: pl.BlockSpec((tm, c), lambda i: (i, 0))
    return pl.pallas_call(
        _inproj_kernel,
        grid=(m // tm,),
        in_specs=[row(d), _const_spec((1, d)), _const_spec(lp["w_in"].shape),
                  _const_spec((_MXU, _MXU)), _const_spec((1, 512)), _const_spec((1, 512)),
                  _const_spec((1, _MLA_Q_RANK)), _const_spec((1, _MLA_KV_RANK)),
                  _const_spec(lp["w_uq"].shape), _const_spec(lp["w_ukv"].shape),
                  _const_spec((1, 512)), _const_spec((1, 512)), tab, tab, tab],
        out_specs=out_specs,
        out_shape=out_shape,
        compiler_params=_params("parallel"),
        name="inproj",
    )(x2d, lp["mix_g"], lp["w_in"], lp["g64"], lp["da_qg"], lp["da_kg"], lp["cq_g"],
      lp["ckv_g"], lp["w_uq"], lp["w_ukv"], lp["mla_qg"], lp["mla_kg"],
      lp["rope_cos"], lp["rope_s1"], lp["rope_s2"])


_SUM_ROWS = 16


def _pipelined_tiles(n_tiles, stage, consume, run, s_even, s_odd, stage_first=None):
    def pair(i, carry):
        run, summary = carry
        run = consume(2 * i, s_even, summary, run)
        summary = stage(2 * i + 1, s_odd)
        run = consume(2 * i + 1, s_odd, summary, run)
        return run, stage(2 * i + 2, s_even)

    def tail_two(_, carry):
        run, summary = carry
        run = consume(n_tiles - 2, s_even, summary, run)
        summary = stage(n_tiles - 1, s_odd)
        return consume(n_tiles - 1, s_odd, summary, run), summary

    def tail_one(_, carry):
        run, summary = carry
        return consume(n_tiles - 1, s_even, summary, run), summary

    carry = (run, (stage_first or stage)(0, s_even))
    last = n_tiles - 1
    odd = jnp.bitwise_and(last, 1)
    carry = lax.fori_loop(0, jnp.right_shift(last, 1), pair, carry)
    carry = lax.fori_loop(0, odd, tail_two, carry)
    lax.fori_loop(0, 1 - odd, tail_one, carry)


def _online_softmax(n_tiles, score, value, s_even, s_odd, acc_scr):
    n_chains, _, t = s_even.shape
    ones = jnp.ones((_SUM_ROWS, t), _BF16)
    chains = range(n_chains)

    def stage(ki, buf):
        col_max = []
        for c in chains:
            s = score(c, ki)
            buf[c] = s
            col_max.append(jnp.max(s, axis=0, keepdims=True))
        return tuple(col_max)

    def consume(ki, buf, col_max, m_run):
        out = []
        for c in chains:
            m_new = jnp.maximum(m_run[c], col_max[c])
            alpha = jnp.exp2(m_run[c] - m_new)
            p = jnp.exp2(buf[c] - m_new).astype(_BF16)
            v_ones = jnp.concatenate([value(c, ki), ones], axis=0)
            acc_scr[c] = alpha * acc_scr[c] + jnp.dot(v_ones, p, preferred_element_type=_F32)
            out.append(m_new)
        return tuple(out)

    for c in chains:
        acc_scr[c] = jnp.zeros(acc_scr.shape[1:], _F32)
    m_init = tuple(jnp.full((1, t), -jnp.inf, _F32) for _ in chains)
    _pipelined_tiles(n_tiles, stage, consume, m_init, s_even, s_odd)


def _softmax_finish(acc, rows):
    width = acc.shape[0] - _SUM_ROWS
    return acc[rows] * (1.0 / acc[width:width + 1])


def _rows(i, t, base=0):
    return pl.ds(pl.multiple_of(base + i * t, t), t)


def _split_rows(x_t, half):
    zero = jnp.zeros((half, x_t.shape[1]), x_t.dtype)
    return (jnp.concatenate([x_t[:half], zero], axis=0),
            jnp.concatenate([zero, x_t[half:]], axis=0))


def _half_row_norm(o_t, half, g_t):
    sq = o_t * o_t
    lo = lax.rsqrt(jnp.mean(sq[:half], axis=0, keepdims=True) + _EPS)
    hi = lax.rsqrt(jnp.mean(sq[half:], axis=0, keepdims=True) + _EPS)
    return jnp.concatenate([o_t[:half] * lo, o_t[half:] * hi], axis=0) * g_t


def _da_kernel(lam_ref, q_ref, k_ref, v_ref, bias_ref, g_ref, o_ref, s_even, s_odd, acc_scr, *, t):
    nq = q_ref.shape[0]
    lam = lam_ref[0]

    heads = range(_DA_HEADS)
    head = lambda hd: slice(hd * _LANES, (hd + 1) * _LANES)
    all_rows = slice(0, _DA_VDIM)

    def q_tile(qi, carry):
        qm = [qh for hd in heads for qh in _split_rows(q_ref[qi, head(hd), :], _DA_DIM)]

        def score(c, ki):
            hd, mi = divmod(c, 2)
            d = jnp.minimum(qi - ki, 2)
            return (jnp.dot(k_ref[_rows(ki, t), head(hd)], qm[c], preferred_element_type=_F32)
                    + bias_ref[hd, d, mi])

        value = lambda c, ki: v_ref[ki, head(c // 2), :]
        _online_softmax(qi + 1, score, value, s_even, s_odd, acc_scr)
        for hd in heads:
            o_t = (_softmax_finish(acc_scr[2 * hd], all_rows)
                   - lam * _softmax_finish(acc_scr[2 * hd + 1], all_rows))
            o_t = o_t * lax.rsqrt(jnp.mean(o_t * o_t, axis=0, keepdims=True) + _EPS) * g_ref[...]
            o_ref[_rows(qi, t), head(hd)] = o_t.T.astype(o_ref.dtype)
        return carry

    lax.fori_loop(0, nq, q_tile, 0)


def _da_attention(q_t, k, v_t, bias, lam, g_t, batch, seq):
    t = _ATT_TILE
    nq = seq // t
    width = _DA_HEADS * _LANES
    tiles = pl.BlockSpec((nq, width, t), lambda b: (b, 0, 0))
    rows = pl.BlockSpec((seq, width), lambda b: (b, 0))
    return pl.pallas_call(
        functools.partial(_da_kernel, t=t),
        grid=(batch,),
        in_specs=[pl.BlockSpec(memory_space=pltpu.SMEM), tiles, rows, tiles,
                  _const_spec(bias.shape), _const_spec((_LANES, t))],
        out_specs=rows,
        out_shape=jax.ShapeDtypeStruct(k.shape, _BF16),
        scratch_shapes=[pltpu.VMEM((2 * _DA_HEADS, t, t), _F32), pltpu.VMEM((2 * _DA_HEADS, t, t), _F32),
                        pltpu.VMEM((2 * _DA_HEADS, _DA_VDIM + _SUM_ROWS, t), _F32)],
        compiler_params=_params("parallel"),
        name="diff_attention",
    )(lam, q_t, k, v_t, bias, g_t)


_PAIR_BATCH = 2


def _mla_kernel(q_ref, k_ref, v_ref, mask_ref, g_ref, o_ref, s_even, s_odd, acc_scr, *, t, nb):
    nq = q_ref.shape[0] // nb
    seq = k_ref.shape[0] // nb

    head = lambda hd: slice(hd * _LANES, (hd + 1) * _LANES)
    pair = lambda hd: slice((hd // 2) * _LANES, (hd // 2 + 1) * _LANES)
    chains = [(bb, hd) for bb in range(nb) for hd in range(_MLA_HEADS)]

    def q_tile(qi, carry):
        qs = [q_ref[bb * nq + qi, head(hd), :] for bb, hd in chains]

        def score(c, ki):
            bb, hd = chains[c]
            d = jnp.minimum(qi - ki, 1)
            return (jnp.dot(k_ref[_rows(ki, t, bb * seq), head(hd)], qs[c], preferred_element_type=_F32)
                    + mask_ref[d])

        value = lambda c, ki: v_ref[chains[c][0] * nq + ki, pair(chains[c][1]), :]
        _online_softmax(qi + 1, score, value, s_even, s_odd, acc_scr)
        for bb in range(nb):
            for p in range(_MLA_HEADS // 2):
                c = bb * _MLA_HEADS + 2 * p
                o_t = jnp.concatenate([_softmax_finish(acc_scr[c], slice(0, _MLA_V)),
                                       _softmax_finish(acc_scr[c + 1], slice(_MLA_V, _LANES))], axis=0)
                o_ref[_rows(qi, t, bb * seq), head(p)] = (
                    _half_row_norm(o_t, _MLA_V, g_ref[...]).T.astype(o_ref.dtype))
        return carry

    lax.fori_loop(0, nq, q_tile, 0)


def _mla_attention(q_t, k, v_t, mask_t, g_t, batch, seq):
    t = _ATT_TILE
    nb = _PAIR_BATCH
    nq = seq // t
    qk_width = _MLA_HEADS * _LANES
    v_width = _MLA_HEADS * _MLA_V
    chains = nb * _MLA_HEADS
    return pl.pallas_call(
        functools.partial(_mla_kernel, t=t, nb=nb),
        grid=(batch // nb,),
        in_specs=[pl.BlockSpec((nb * nq, qk_width, t), lambda b: (b, 0, 0)),
                  pl.BlockSpec((nb * seq, qk_width), lambda b: (b, 0)),
                  pl.BlockSpec((nb * nq, v_width, t), lambda b: (b, 0, 0)),
                  _const_spec((2, t, t)), _const_spec((_LANES, t))],
        out_specs=pl.BlockSpec((nb * seq, v_width), lambda b: (b, 0)),
        out_shape=jax.ShapeDtypeStruct((batch * seq, v_width), _BF16),
        scratch_shapes=[pltpu.VMEM((chains, t, t), _F32), pltpu.VMEM((chains, t, t), _F32),
                        pltpu.VMEM((chains, _LANES + _SUM_ROWS, t), _F32)],
        compiler_params=_params("parallel"),
        name="latent_attention",
    )(q_t, k, v_t, mask_t, g_t)


def _sb_kernel(q_ref, k_ref, v_ref, g_ref, o_ref, s_even, s_odd, acc_scr, *, t, nb):
    nq = q_ref.shape[0] // nb
    seq = k_ref.shape[0] // nb
    key_idx = lax.broadcasted_iota(jnp.int32, (t, t), 0)
    query_idx = lax.broadcasted_iota(jnp.int32, (t, t), 1)
    earlier = key_idx < query_idx
    tri = jnp.where(earlier, 1.0, 0.0).astype(_BF16)
    tri2 = jnp.concatenate([tri, tri], axis=1)

    pair = lambda hd: slice((hd // 2) * _LANES, (hd // 2 + 1) * _LANES)
    dot = functools.partial(jnp.dot, preferred_element_type=_F32)
    chains = [(bb, hd) for bb in range(nb) for hd in range(_SB_HEADS)]
    n_chains = range(len(chains))

    def q_tile(qi, carry):
        qh = [q for bb in range(nb) for p in range(_SB_HEADS // 2)
              for q in _split_rows(q_ref[bb * nq + qi, pair(2 * p), :], _SB_DIM)]

        def stage(j, buf, diag=False):
            z = [dot(k_ref[_rows(qi - j, t, bb * seq), pair(hd)], qh[c])
                 for c, (bb, hd) in enumerate(chains)]
            log_beta, first_row, later = [], [], []
            for zi in z:
                lp = jnp.log2(1.0 + jnp.exp2(jnp.minimum(zi, -zi)))
                lb = jnp.minimum(zi, 0.0) - lp
                log_1m = lb - zi
                if diag:
                    log_1m = jnp.where(earlier, log_1m, 0.0)
                l_hi = log_1m.astype(_BF16)
                l_lo = (log_1m - l_hi.astype(_F32)).astype(_BF16)
                later.append(dot(tri2, jnp.concatenate([l_hi, l_lo], axis=0)))
                log_beta.append(lb)
                first_row.append(log_1m[0:1, :])
            for c in n_chains:
                log_w = log_beta[c] + later[c]
                buf[c] = jnp.where(earlier, log_w, -jnp.inf) if diag else log_w
            return tuple(la[0:1, :] + fr for la, fr in zip(later, first_row))

        def consume(j, buf, through, run):
            for c, (bb, hd) in enumerate(chains):
                a = jnp.exp2(buf[c] + run[c]).astype(_BF16)
                acc_scr[c] = acc_scr[c] + dot(v_ref[bb * nq + qi - j, pair(hd), :], a)
            return tuple(r + th for r, th in zip(run, through))

        for c in n_chains:
            acc_scr[c] = jnp.zeros(acc_scr.shape[1:], _F32)
        run0 = tuple(jnp.zeros((1, t), _F32) for _ in n_chains)
        _pipelined_tiles(qi + 1, stage, consume, run0, s_even, s_odd,
                         stage_first=functools.partial(stage, diag=True))
        for bb in range(nb):
            for p in range(_SB_HEADS // 2):
                c = bb * _SB_HEADS + 2 * p
                o_t = jnp.concatenate([acc_scr[c, :_SB_DIM, :], acc_scr[c + 1, _SB_DIM:, :]], axis=0)
                o_ref[_rows(qi, t, bb * seq), pair(2 * p)] = (
                    _half_row_norm(o_t, _SB_DIM, g_ref[...]).T.astype(o_ref.dtype))
        return carry

    lax.fori_loop(0, nq, q_tile, 0)


def _sb_attention(q_t, k, v_t, g_t, batch, seq):
    t = _ATT_TILE
    nb = _PAIR_BATCH
    nq = seq // t
    width = _SB_HEADS * _SB_DIM
    chains = nb * _SB_HEADS
    tiles = pl.BlockSpec((nb * nq, width, t), lambda b: (b, 0, 0))
    rows = pl.BlockSpec((nb * seq, width), lambda b: (b, 0))
    return pl.pallas_call(
        functools.partial(_sb_kernel, t=t, nb=nb),
        grid=(batch // nb,),
        in_specs=[tiles, rows, tiles, _const_spec((_LANES, t))],
        out_specs=rows,
        out_shape=jax.ShapeDtypeStruct(k.shape, _BF16),
        scratch_shapes=[pltpu.VMEM((chains, t, t), _F32), pltpu.VMEM((chains, t, t), _F32),
                        pltpu.VMEM((chains, _LANES, t), _F32)],
        compiler_params=_params("parallel"),
        name="stick_breaking",
    )(q_t, k, v_t, g_t)


def _memkv_kernel(mem_ref, g_ref, w_ref, g64_ref, kg_ref, k_ref, v_ref):
    m = _rms(mem_ref[...], g_ref[...]).astype(_BF16)
    kv = jnp.dot(m, w_ref[...], preferred_element_type=_F32)
    width = _MEM_HEADS * _MEM_DIM
    k = kv[:, :width]
    k = k * lax.rsqrt(_group_mean_sq(k, g64_ref[...], _MEM_DIM) + _EPS) * kg_ref[...]
    k_ref[...] = k.astype(_BF16)
    v_ref[...] = kv[:, width:].astype(_BF16)


def _memkv(mem2d, g, w, g64, kg):
    depth = w.shape[0]
    rows, d = mem2d.shape
    tm = _ROW_TILE
    width = _MEM_HEADS * _MEM_DIM
    out = pl.BlockSpec((None, tm, width), lambda l, i: (l, i, 0))
    return pl.pallas_call(
        _memkv_kernel,
        grid=(depth, rows // tm),
        in_specs=[pl.BlockSpec((tm, d), lambda l, i: (i, 0)),
                  pl.BlockSpec((None, 1, d), lambda l, i: (l, 0, 0)),
                  pl.BlockSpec((None, d, 2 * width), lambda l, i: (l, 0, 0)),
                  _const_spec((_MXU, _MXU)),
                  pl.BlockSpec((None, 1, width), lambda l, i: (l, 0, 0))],
        out_specs=[out, out],
        out_shape=[jax.ShapeDtypeStruct((depth, rows, width), _BF16)] * 2,
        compiler_params=_params("parallel", "parallel"),
        name="memory_kv",
    )(mem2d, g, w, g64, kg)


def _mix_mem_kernel(x_ref, ya_ref, yb_ref, yc_ref, wo_ref, gx_ref, wq_ref, g64_ref, qg_ref,
                    km_ref, vm_ref, wmo_ref, o_ref):
    wa = ya_ref.shape[1]
    wb = wa + yb_ref.shape[1]
    x = (x_ref[...]
         + jnp.dot(ya_ref[...], wo_ref[:wa, :], preferred_element_type=_F32)
         + jnp.dot(yb_ref[...], wo_ref[wa:wb, :], preferred_element_type=_F32)
         + jnp.dot(yc_ref[...], wo_ref[wb:, :], preferred_element_type=_F32))
    h = _rms(x, gx_ref[...]).astype(_BF16)
    q = jnp.dot(h, wq_ref[...], preferred_element_type=_F32)
    q = (q * lax.rsqrt(_group_mean_sq(q, g64_ref[...], _MEM_DIM) + _EPS) * qg_ref[...]).astype(_BF16)
    km = km_ref[...]
    vm = vm_ref[...]
    head_of_lane = lax.broadcasted_iota(jnp.int32, (1, q.shape[1]), 1) // _MEM_DIM
    zero = jnp.zeros_like(q)
    o = jnp.zeros(q.shape, _F32)
    for hd in range(_MEM_HEADS):
        sel = head_of_lane == hd
        s = lax.dot_general(jnp.where(sel, q, zero), km, _NT, preferred_element_type=_F32)
        p = jnp.exp(s - jnp.max(s, axis=-1, keepdims=True))
        oh = jnp.dot(p.astype(_BF16), vm, preferred_element_type=_F32)
        o = jnp.where(sel, oh / jnp.sum(p, axis=-1, keepdims=True), o)
    o_ref[...] = x + jnp.dot(o.astype(_BF16), wmo_ref[...], preferred_element_type=_F32)


def _mix_mem(x2d, ya, yb, yc, lp, km, vm, seq):
    m, d = x2d.shape
    tm = _MIX_ROWS
    per_seq = seq // tm
    n_mem, width = km.shape[1], km.shape[2]
    row = lambda c: pl.BlockSpec((tm, c), lambda i: (i, 0))
    mem = pl.BlockSpec((None, n_mem, width), lambda i: (i // per_seq, 0, 0))
    return pl.pallas_call(
        _mix_mem_kernel,
        grid=(m // tm,),
        in_specs=[row(d), row(ya.shape[1]), row(yb.shape[1]), row(yc.shape[1]),
                  _const_spec(lp["w_out"].shape), _const_spec((1, d)),
                  _const_spec(lp["w_mem_q"].shape), _const_spec((_MXU, _MXU)),
                  _const_spec((1, width)), mem, mem, _const_spec(lp["w_mem_o"].shape)],
        out_specs=row(d),
        out_shape=jax.ShapeDtypeStruct((m, d), _F32),
        compiler_params=_params("parallel"),
        name="mix_and_memory",
    )(x2d, ya, yb, yc, lp["w_out"], lp["memx_g"], lp["w_mem_q"], lp["g64"], lp["mem_qg"],
      km, vm, lp["w_mem_o"])


_FF_CHUNK = 1024


def _ffn_kernel(x_ref, g_ref, w1_ref, w2_ref, o_ref):
    x = x_ref[...]
    h = _rms(x, g_ref[...]).astype(_BF16)
    acc = x
    for c in range(0, w1_ref.shape[1], _FF_CHUNK):
        u = jnp.dot(h, w1_ref[:, c:c + _FF_CHUNK], preferred_element_type=_F32)
        r = jnp.maximum(u, 0.0)
        acc = acc + jnp.dot((r * r).astype(_BF16), w2_ref[c:c + _FF_CHUNK, :],
                            preferred_element_type=_F32)
    o_ref[...] = acc


def _ffn(x2d, g, w1, w2):
    m, d = x2d.shape
    tm = _ROW_TILE
    row = pl.BlockSpec((tm, d), lambda i: (i, 0))
    return pl.pallas_call(
        _ffn_kernel,
        grid=(m // tm,),
        in_specs=[row, _const_spec((1, d)), _const_spec(w1.shape), _const_spec(w2.shape)],
        out_specs=row,
        out_shape=jax.ShapeDtypeStruct((m, d), _F32),
        compiler_params=_params("parallel"),
        name="ffn",
    )(x2d, g, w1, w2)


def _t5_bucket(rel):
    nb = _NUM_BUCKETS // 2
    bucket = (rel > 0).astype(jnp.int32) * nb
    n = jnp.abs(rel)
    max_exact = nb // 2
    is_small = n < max_exact
    large = max_exact + (jnp.log(jnp.maximum(n, 1).astype(jnp.float32) / max_exact)
                         / math.log(_MAX_DISTANCE / max_exact) * (nb - max_exact)).astype(jnp.int32)
    large = jnp.minimum(large, nb - 1)
    return bucket + jnp.where(is_small, n, large)


def _da_bias_tables(rel_bias, t):
    assert t + 1 >= _MAX_DISTANCE and t % _CHUNK == 0
    j = jnp.arange(t, dtype=jnp.int32)[:, None]
    i = jnp.arange(t, dtype=jnp.int32)[None, :]
    rb = rel_bias.astype(_F32) * _LOG2E

    def lookup(bucket):
        hit = bucket[:, :, None, None] == jnp.arange(_NUM_BUCKETS, dtype=jnp.int32)[:, None]
        return jnp.sum(jnp.where(hit, rb[None, None], 0.0), axis=2)

    b0 = lookup(_t5_bucket(j - i))
    b0 = jnp.where(((j // _CHUNK) <= (i // _CHUNK))[:, :, None], b0, -jnp.inf)
    b1 = lookup(_t5_bucket(j - i - t))
    far = jnp.broadcast_to(lookup(_t5_bucket(jnp.full((1, 1), -(t + 1), jnp.int32))), b1.shape)
    tab = jnp.stack([b0, b1, far]).reshape(3, t, t, _DA_HEADS, 2)
    return tab.transpose(3, 0, 4, 1, 2)


def _rope_tables(seq):
    half = _MLA_ROPE // 2
    freqs = _ROPE_THETA ** (-jnp.arange(half, dtype=jnp.float32) / half)
    ang = jnp.arange(seq, dtype=jnp.int32).astype(jnp.float32)[:, None] * freqs[None, :]
    cos, sin = jnp.cos(ang), jnp.sin(ang)
    ones = jnp.ones((seq, _MLA_NOPE), _F32)
    z = lambda w: jnp.zeros((seq, w), _F32)
    tail = _LANES - _MLA_QK
    c = jnp.concatenate([ones, cos, cos, z(tail)], axis=1)
    s1 = jnp.concatenate([z(_MLA_NOPE + half), sin, z(tail)], axis=1)
    s2 = jnp.concatenate([z(_MLA_NOPE), -sin, z(half + tail)], axis=1)
    return c, s1, s2


def _group_ones(group):
    idx = np.arange(_MXU) // group
    return jnp.asarray(idx[:, None] == idx[None, :], dtype=_BF16)


def _layer_params(l, p, rope, g64):
    d = p["w_in"].shape[1]
    o = _IN_OFFS
    w_in = p["w_in"][l]
    kr = jnp.zeros((d, _LANES), _F32).at[:, _MLA_NOPE:_MLA_QK].set(w_in[:, o[8]:o[9]])
    w_uq = jnp.pad(p["w_mla_uq"][l].reshape(_MLA_Q_RANK, _MLA_HEADS, _MLA_QK),
                   ((0, 0), (0, 0), (0, _LANES - _MLA_QK))).reshape(_MLA_Q_RANK, -1)
    w_ukv = p["w_mla_ukv"][l].reshape(_MLA_KV_RANK, _MLA_HEADS, _MLA_NOPE + _MLA_V)
    w_k = jnp.pad(w_ukv[:, :, :_MLA_NOPE], ((0, 0), (0, 0), (0, _LANES - _MLA_NOPE)))
    w_v = w_ukv[:, :, _MLA_NOPE:]
    pad_g = lambda g: jnp.tile(jnp.pad(g, (0, _LANES - _MLA_QK)), _MLA_HEADS)[None]
    col = lambda g: jnp.broadcast_to(g[:, None], (g.shape[0], _ATT_TILE))
    lam_init = 0.8 - 0.6 * math.exp(-0.3 * l)
    lp = p["da_lambda"][l].astype(_F32)
    lam = jnp.exp(jnp.sum(lp[0] * lp[1])) - jnp.exp(jnp.sum(lp[2] * lp[3])) + lam_init
    return {
        "mix_g": p["mix_norm_g"][l][None],
        "w_in": jnp.concatenate([w_in[:, :o[8]], kr], axis=1).astype(_BF16),
        "g64": g64,
        "da_qg": jnp.tile(p["da_q_norm_g"][l], 2 * _DA_HEADS)[None] * (_DA_DIM ** -0.5 * _LOG2E),
        "da_kg": jnp.tile(p["da_k_norm_g"][l], 2 * _DA_HEADS)[None],
        "cq_g": p["mla_cq_norm_g"][l][None],
        "ckv_g": p["mla_ckv_norm_g"][l][None],
        "w_uq": w_uq.astype(_BF16),
        "w_ukv": jnp.concatenate([w_k.reshape(_MLA_KV_RANK, -1), w_v.reshape(_MLA_KV_RANK, -1)],
                                 axis=1).astype(_BF16),
        "mla_qg": pad_g(p["mla_q_norm_g"][l]) * (_MLA_QK ** -0.5 * _LOG2E),
        "mla_kg": pad_g(p["mla_k_norm_g"][l]),
        "rope_cos": rope[0], "rope_s1": rope[1], "rope_s2": rope[2],
        "lam": jnp.reshape(lam, (1,)).astype(_F32),
        "da_og": col(p["da_subln_g"][l] * (1.0 - lam_init)),
        "sb_og": col(jnp.tile(p["sb_out_g"][l], 2)),
        "mla_og": col(jnp.tile(p["mla_out_g"][l], 2)),
        "w_out": p["w_out"][l].astype(_BF16),
        "memx_g": p["memx_norm_g"][l][None],
        "w_mem_q": p["w_mem_q"][l].astype(_BF16),
        "mem_qg": jnp.tile(p["mem_q_norm_g"][l], _MEM_HEADS)[None] * (_MEM_DIM ** -0.5),
        "w_mem_o": p["w_mem_o"][l].astype(_BF16),
        "ffn_g": p["ffn_norm_g"][l][None],
        "w_ff1": p["w_ff1"][l].astype(_BF16),
        "w_ff2": p["w_ff2"][l].astype(_BF16),
    }


def kernel(x, mem, rel_bias, mix_norm_g, w_in, da_q_norm_g, da_k_norm_g, da_lambda, da_subln_g,
           sb_out_g, mla_cq_norm_g, mla_ckv_norm_g, w_mla_uq, w_mla_ukv, mla_q_norm_g, mla_k_norm_g,
           mla_out_g, w_out, memx_norm_g, mem_norm_g, w_mem_q, w_mem_kv, mem_q_norm_g, mem_k_norm_g,
           w_mem_o, ffn_norm_g, w_ff1, w_ff2):
    p = dict(mix_norm_g=mix_norm_g, w_in=w_in, da_q_norm_g=da_q_norm_g, da_k_norm_g=da_k_norm_g,
             da_lambda=da_lambda, da_subln_g=da_subln_g, sb_out_g=sb_out_g,
             mla_cq_norm_g=mla_cq_norm_g, mla_ckv_norm_g=mla_ckv_norm_g, w_mla_uq=w_mla_uq,
             w_mla_ukv=w_mla_ukv, mla_q_norm_g=mla_q_norm_g, mla_k_norm_g=mla_k_norm_g,
             mla_out_g=mla_out_g, w_out=w_out, memx_norm_g=memx_norm_g, w_mem_q=w_mem_q,
             mem_q_norm_g=mem_q_norm_g, w_mem_o=w_mem_o, ffn_norm_g=ffn_norm_g, w_ff1=w_ff1,
             w_ff2=w_ff2)
    batch, seq, d = x.shape
    depth = w_in.shape[0]
    n_mem = mem.shape[1]
    t = _ATT_TILE
    assert seq % _ROW_TILE == 0 and seq % _MIX_ROWS == 0 and seq % t == 0
    assert (batch * n_mem) % _ROW_TILE == 0
    assert batch % _PAIR_BATCH == 0
    assert w_in.shape[2] == _IN_OFFS[-1]

    g64 = _group_ones(_MEM_DIM)
    rope = _rope_tables(seq)
    bias = _da_bias_tables(rel_bias, t)
    i = jnp.arange(t, dtype=jnp.int32)
    chunk_mask = jnp.where((i[:, None] // _CHUNK) <= (i[None, :] // _CHUNK), 0.0, -jnp.inf).astype(_F32)
    chunk_mask = jnp.stack([chunk_mask, jnp.zeros_like(chunk_mask)])

    width = _MEM_HEADS * _MEM_DIM
    km, vm = _memkv(mem.reshape(batch * n_mem, d), mem_norm_g[:, None, :], w_mem_kv.astype(_BF16), g64,
                    jnp.tile(mem_k_norm_g, (1, _MEM_HEADS))[:, None, :])
    km = km.reshape(depth, batch, n_mem, width)
    vm = vm.reshape(depth, batch, n_mem, width)

    x2d = x.reshape(batch * seq, d)
    for l in range(depth):
        lp = _layer_params(l, p, rope, g64)
        daq, dak, dav, sbq, sbk, sbv, mq, mk, mv = _inproj(x2d, lp, seq)
        ya = _da_attention(daq, dak, dav, bias, lp["lam"], lp["da_og"], batch, seq)
        yb = _sb_attention(sbq, sbk, sbv, lp["sb_og"], batch, seq)
        yc = _mla_attention(mq, mk, mv, chunk_mask, lp["mla_og"], batch, seq)
        x2d = _mix_mem(x2d, ya, yb, yc, lp, km[l], vm[l], seq)
        x2d = _ffn(x2d, lp["ffn_g"], lp["w_ff1"], lp["w_ff2"])
    return x2d.reshape(batch, seq, d)
```

```python
import functools
import math

import numpy as np
import jax
import jax.numpy as jnp
from jax import lax
from jax.experimental import pallas as pl
from jax.experimental.pallas import tpu as pltpu

_F32 = jnp.float32
_BF16 = jnp.bfloat16
_EPS = 1e-6

_CHUNK = 64
_DA_HEADS, _DA_DIM = 4, 64
_DA_VDIM = 2 * _DA_DIM
_SB_HEADS, _SB_DIM = 4, 64
_MLA_HEADS, _MLA_NOPE, _MLA_ROPE, _MLA_V = 4, 64, 32, 64
_MLA_QK = _MLA_NOPE + _MLA_ROPE
_MLA_Q_RANK, _MLA_KV_RANK = 256, 128
_ROPE_THETA = 10000.0
_NUM_BUCKETS, _MAX_DISTANCE = 32, 128
_MEM_HEADS, _MEM_DIM = 4, 64

_LANES = 128
_MXU = 256
_VMEM_LIMIT = 52 * 1024 * 1024

_ATT_TILE = 256
_ROW_TILE = 512
_MIX_ROWS = 1024

_IN_SIZES = (512, 512, 512, 256, 256, 256, _MLA_Q_RANK, _MLA_KV_RANK, _MLA_ROPE)
_IN_OFFS = tuple(int(v) for v in np.cumsum((0,) + _IN_SIZES))
_NT = (((1,), (1,)), ((), ()))
_LOG2E = math.log2(math.e)


def _const_spec(shape):
    zeros = (0,) * len(shape)
    return pl.BlockSpec(shape, lambda *_: zeros, pipeline_mode=pl.Buffered(1))


def _params(*sem):
    return pltpu.CompilerParams(dimension_semantics=sem, vmem_limit_bytes=_VMEM_LIMIT)


def _rms(x, g):
    return x * lax.rsqrt(jnp.mean(x * x, axis=-1, keepdims=True) + _EPS) * g


def _group_mean_sq(y, gmat, group):
    sq = (y * y).astype(_BF16)
    cols = y.shape[1]
    parts = [jnp.dot(sq[:, c:c + _MXU], gmat, preferred_element_type=_F32)
             for c in range(0, cols, _MXU)]
    ss = parts[0] if len(parts) == 1 else jnp.concatenate(parts, axis=1)
    return ss * (1.0 / group)


def _inproj_kernel(x_ref, gmix_ref, w_ref, g64_ref, gq_ref, gk_ref, cqg_ref, ckvg_ref,
                   wuq_ref, wukv_ref, qg_ref, kg_ref, cos_ref, s1_ref, s2_ref,
                   daq_ref, dak_ref, dav_ref, sbq_ref, sbk_ref, sbv_ref,
                   mq_ref, mk_ref, mv_ref):
    x = x_ref[...]
    h = _rms(x, gmix_ref[...]).astype(_BF16)
    o = _IN_OFFS

    def proj(seg, width=None):
        hi = o[seg + 1] if width is None else o[seg] + width
        return jnp.dot(h, w_ref[:, o[seg]:hi], preferred_element_type=_F32)

    g64 = g64_ref[...]
    t = daq_ref.shape[2]

    def norm64(y, g):
        return y * lax.rsqrt(_group_mean_sq(y, g64, _DA_DIM) + _EPS) * g

    def store_t(ref, y, row0=0):
        for r in range(y.shape[0] // t):
            ref[r, row0:row0 + y.shape[1], :] = y[r * t:(r + 1) * t, :].T.astype(_BF16)

    cos, s1, s2 = cos_ref[...], s1_ref[...], s2_ref[...]

    def head_norm_rope(y, g):
        ms = jnp.sum(y * y, axis=-1, keepdims=True) * (1.0 / _MLA_QK)
        yn = y * lax.rsqrt(ms + _EPS) * g
        half = _MLA_ROPE // 2
        return (yn * cos + pltpu.roll(yn, half, 1) * s1
                + pltpu.roll(yn, _LANES - half, 1) * s2)

    cq = _rms(proj(6), cqg_ref[...]).astype(_BF16)
    q_all = jnp.dot(cq, wuq_ref[...], preferred_element_type=_F32)
    ckv = _rms(proj(7), ckvg_ref[...]).astype(_BF16)
    kv_all = jnp.dot(ckv, wukv_ref[...], preferred_element_type=_F32)
    k_rope = proj(8, _LANES)

    def latent_head(hd):
        sl = slice(hd * _LANES, (hd + 1) * _LANES)
        store_t(mq_ref, head_norm_rope(q_all[:, sl], qg_ref[:, sl]), hd * _LANES)
        mk_ref[:, sl] = head_norm_rope(kv_all[:, sl] + k_rope, kg_ref[:, sl]).astype(_BF16)

    store_t(daq_ref, norm64(proj(0), gq_ref[...]))
    latent_head(0)
    dak_ref[...] = norm64(proj(1), gk_ref[...]).astype(_BF16)
    latent_head(1)
    store_t(dav_ref, proj(2))
    latent_head(2)
    store_t(sbq_ref, proj(3) * (_SB_DIM ** -0.5 * _LOG2E))
    latent_head(3)
    sbk_ref[...] = proj(4).astype(_BF16)
    store_t(mv_ref, kv_all[:, _MLA_HEADS * _LANES:])
    store_t(sbv_ref, proj(5))


def _inproj(x2d, lp, seq):
    m, d = x2d.shape
    tm = _ROW_TILE
    t = _ATT_TILE
    pos_blocks = seq // tm
    tab = pl.BlockSpec((tm, _LANES), lambda i: (i % pos_blocks, 0))
    widths = (512, 512, 512, 256, 256, 256, 512, 512, 256)
    transposed = (True, False, True, True, False, True, True, False, True)
    out_specs, out_shape = [], []
    for c, tr in zip(widths, transposed):
        if tr:
            out_specs.append(pl.BlockSpec((tm // t, c, t), lambda i: (i, 0, 0)))
            out_shape.append(jax.ShapeDtypeStruct((m // t, c, t), _BF16))
        else:
            out_specs.append(pl.BlockSpec((tm, c), lambda i: (i, 0)))
            out_shape.append(jax.ShapeDtypeStruct((m, c), _BF16))
    row = lambda c: pl.BlockSpec((tm, c), lambda i: (i, 0))
    return pl.pallas_call(
        _inproj_kernel,
        grid=(m // tm,),
        in_specs=[row(d), _const_spec((1, d)), _const_spec(lp["w_in"].shape),
                  _const_spec((_MXU, _MXU)), _const_spec((1, 512)), _const_spec((1, 512)),
                  _const_spec((1, _MLA_Q_RANK)), _const_spec((1, _MLA_KV_RANK)),
                  _const_spec(lp["w_uq"].shape), _const_spec(lp["w_ukv"].shape),
                  _const_spec((1, 512)), _const_spec((1, 512)), tab, tab, tab],
        out_specs=out_specs,
        out_shape=out_shape,
        compiler_params=_params("parallel"),
        name="inproj",
    )(x2d, lp["mix_g"], lp["w_in"], lp["g64"], lp["da_qg"], lp["da_kg"], lp["cq_g"],
      lp["ckv_g"], lp["w_uq"], lp["w_ukv"], lp["mla_qg"], lp["mla_kg"],
      lp["rope_cos"], lp["rope_s1"], lp["rope_s2"])


_SUM_ROWS = 16


def _pipelined_tiles(n_tiles, n_chains, stage, consume, run, s_even, s_odd, group):
    groups = [range(g, min(g + group, n_chains)) for g in range(0, n_chains, group)]

    def overlap(j_stage, buf_stage, j_done, buf_done, summary, run):
        new_summary, new_run = [], []
        for members in groups:
            new_summary += stage(j_stage, buf_stage, members, False)
            new_run += consume(j_done, buf_done, members,
                               [summary[c] for c in members], [run[c] for c in members])
        return tuple(new_run), tuple(new_summary)

    def finish(j, buf, summary, run):
        return tuple(consume(j, buf, range(n_chains), list(summary), list(run)))

    def pair(i, carry):
        run, summary = carry
        run, summary = overlap(2 * i + 1, s_odd, 2 * i, s_even, summary, run)
        return overlap(2 * i + 2, s_even, 2 * i + 1, s_odd, summary, run)

    def tail_two(_, carry):
        run, summary = carry
        run, summary = overlap(n_tiles - 1, s_odd, n_tiles - 2, s_even, summary, run)
        return finish(n_tiles - 1, s_odd, summary, run), summary

    def tail_one(_, carry):
        run, summary = carry
        return finish(n_tiles - 1, s_even, summary, run), summary

    carry = (run, tuple(stage(0, s_even, range(n_chains), True)))
    last = n_tiles - 1
    odd = jnp.bitwise_and(last, 1)
    carry = lax.fori_loop(0, jnp.right_shift(last, 1), pair, carry)
    carry = lax.fori_loop(0, odd, tail_two, carry)
    lax.fori_loop(0, 1 - odd, tail_one, carry)


def _online_softmax(n_tiles, score, value, s_even, s_odd, acc_scr):
    n_chains, _, t = s_even.shape
    ones = jnp.ones((_SUM_ROWS, t), _BF16)

    def stage(ki, buf, members, first):
        col_max = []
        for c in members:
            s = score(c, ki)
            buf[c] = s
            col_max.append(jnp.max(s, axis=0, keepdims=True))
        return col_max

    def consume(ki, buf, members, col_max, m_run):
        out = []
        for c, cm, m_old in zip(members, col_max, m_run):
            m_new = jnp.maximum(m_old, cm)
            alpha = jnp.exp2(m_old - m_new)
            p = jnp.exp2(buf[c] - m_new).astype(_BF16)
            v_ones = jnp.concatenate([value(c, ki), ones], axis=0)
            acc_scr[c] = alpha * acc_scr[c] + jnp.dot(v_ones, p, preferred_element_type=_F32)
            out.append(m_new)
        return out

    for c in range(n_chains):
        acc_scr[c] = jnp.zeros(acc_scr.shape[1:], _F32)
    m_init = tuple(jnp.full((1, t), -jnp.inf, _F32) for _ in range(n_chains))
    _pipelined_tiles(n_tiles, n_chains, stage, consume, m_init, s_even, s_odd, group=1)


def _softmax_finish(acc, rows):
    width = acc.shape[0] - _SUM_ROWS
    return acc[rows] * (1.0 / acc[width:width + 1])


def _rows(i, t, base=0):
    return pl.ds(pl.multiple_of(base + i * t, t), t)


def _split_rows(x_t, half):
    zero = jnp.zeros((half, x_t.shape[1]), x_t.dtype)
    return (jnp.concatenate([x_t[:half], zero], axis=0),
            jnp.concatenate([zero, x_t[half:]], axis=0))


def _half_row_norm(o_t, half, g_t):
    sq = o_t * o_t
    lo = lax.rsqrt(jnp.mean(sq[:half], axis=0, keepdims=True) + _EPS)
    hi = lax.rsqrt(jnp.mean(sq[half:], axis=0, keepdims=True) + _EPS)
    return jnp.concatenate([o_t[:half] * lo, o_t[half:] * hi], axis=0) * g_t


def _da_kernel(lam_ref, q_ref, k_ref, v_ref, bias_ref, g_ref, o_ref, s_even, s_odd, acc_scr, *, t):
    nq = q_ref.shape[0]
    lam = lam_ref[0]

    heads = range(_DA_HEADS)
    head = lambda hd: slice(hd * _LANES, (hd + 1) * _LANES)
    all_rows = slice(0, _DA_VDIM)

    def q_tile(qi, carry):
        qm = [qh for hd in heads for qh in _split_rows(q_ref[qi, head(hd), :], _DA_DIM)]

        def score(c, ki):
            hd, mi = divmod(c, 2)
            d = jnp.minimum(qi - ki, 2)
            return (jnp.dot(k_ref[_rows(ki, t), head(hd)], qm[c], preferred_element_type=_F32)
                    + bias_ref[hd, d, mi])

        value = lambda c, ki: v_ref[ki, head(c // 2), :]
        _online_softmax(qi + 1, score, value, s_even, s_odd, acc_scr)
        for hd in heads:
            o_t = (_softmax_finish(acc_scr[2 * hd], all_rows)
                   - lam * _softmax_finish(acc_scr[2 * hd + 1], all_rows))
            o_t = o_t * lax.rsqrt(jnp.mean(o_t * o_t, axis=0, keepdims=True) + _EPS) * g_ref[...]
            o_ref[_rows(qi, t), head(hd)] = o_t.T.astype(o_ref.dtype)
        return carry

    lax.fori_loop(0, nq, q_tile, 0)


def _da_attention(q_t, k, v_t, bias, lam, g_t, batch, seq):
    t = _ATT_TILE
    nq = seq // t
    width = _DA_HEADS * _LANES
    tiles = pl.BlockSpec((nq, width, t), lambda b: (b, 0, 0))
    rows = pl.BlockSpec((seq, width), lambda b: (b, 0))
    return pl.pallas_call(
        functools.partial(_da_kernel, t=t),
        grid=(batch,),
        in_specs=[pl.BlockSpec(memory_space=pltpu.SMEM), tiles, rows, tiles,
                  _const_spec(bias.shape), _const_spec((_LANES, t))],
        out_specs=rows,
        out_shape=jax.ShapeDtypeStruct(k.shape, _BF16),
        scratch_shapes=[pltpu.VMEM((2 * _DA_HEADS, t, t), _F32), pltpu.VMEM((2 * _DA_HEADS, t, t), _F32),
                        pltpu.VMEM((2 * _DA_HEADS, _DA_VDIM + _SUM_ROWS, t), _F32)],
        compiler_params=_params("parallel"),
        name="diff_attention",
    )(lam, q_t, k, v_t, bias, g_t)


_PAIR_BATCH = 2


def _mla_kernel(q_ref, k_ref, v_ref, mask_ref, g_ref, o_ref, s_even, s_odd, acc_scr, *, t, nb):
    nq = q_ref.shape[0] // nb
    seq = k_ref.shape[0] // nb

    head = lambda hd: slice(hd * _LANES, (hd + 1) * _LANES)
    pair = lambda hd: slice((hd // 2) * _LANES, (hd // 2 + 1) * _LANES)
    chains = [(bb, hd) for bb in range(nb) for hd in range(_MLA_HEADS)]

    def q_tile(qi, carry):
        qs = [q_ref[bb * nq + qi, head(hd), :] for bb, hd in chains]

        def score(c, ki):
            bb, hd = chains[c]
            d = jnp.minimum(qi - ki, 1)
            return (jnp.dot(k_ref[_rows(ki, t, bb * seq), head(hd)], qs[c], preferred_element_type=_F32)
                    + mask_ref[d])

        value = lambda c, ki: v_ref[chains[c][0] * nq + ki, pair(chains[c][1]), :]
        _online_softmax(qi + 1, score, value, s_even, s_odd, acc_scr)
        for bb in range(nb):
            for p in range(_MLA_HEADS // 2):
                c = bb * _MLA_HEADS + 2 * p
                o_t = jnp.concatenate([_softmax_finish(acc_scr[c], slice(0, _MLA_V)),
                                       _softmax_finish(acc_scr[c + 1], slice(_MLA_V, _LANES))], axis=0)
                o_ref[_rows(qi, t, bb * seq), head(p)] = (
                    _half_row_norm(o_t, _MLA_V, g_ref[...]).T.astype(o_ref.dtype))
        return carry

    lax.fori_loop(0, nq, q_tile, 0)


def _mla_attention(q_t, k, v_t, mask_t, g_t, batch, seq):
    t = _ATT_TILE
    nb = _PAIR_BATCH
    nq = seq // t
    qk_width = _MLA_HEADS * _LANES
    v_width = _MLA_HEADS * _MLA_V
    chains = nb * _MLA_HEADS
    return pl.pallas_call(
        functools.partial(_mla_kernel, t=t, nb=nb),
        grid=(batch // nb,),
        in_specs=[pl.BlockSpec((nb * nq, qk_width, t), lambda b: (b, 0, 0)),
                  pl.BlockSpec((nb * seq, qk_width), lambda b: (b, 0)),
                  pl.BlockSpec((nb * nq, v_width, t), lambda b: (b, 0, 0)),
                  _const_spec((2, t, t)), _const_spec((_LANES, t))],
        out_specs=pl.BlockSpec((nb * seq, v_width), lambda b: (b, 0)),
        out_shape=jax.ShapeDtypeStruct((batch * seq, v_width), _BF16),
        scratch_shapes=[pltpu.VMEM((chains, t, t), _F32), pltpu.VMEM((chains, t, t), _F32),
                        pltpu.VMEM((chains, _LANES + _SUM_ROWS, t), _F32)],
        compiler_params=_params("parallel"),
        name="latent_attention",
    )(q_t, k, v_t, mask_t, g_t)


def _sb_kernel(q_ref, k_ref, v_ref, g_ref, o_ref, s_even, s_odd, acc_scr, *, t, nb):
    nq = q_ref.shape[0] // nb
    seq = k_ref.shape[0] // nb
    key_idx = lax.broadcasted_iota(jnp.int32, (t, t), 0)
    query_idx = lax.broadcasted_iota(jnp.int32, (t, t), 1)
    earlier = key_idx < query_idx
    tri = jnp.where(earlier, 1.0, 0.0).astype(_BF16)
    tri2 = jnp.concatenate([tri, tri], axis=1)

    pair = lambda hd: slice((hd // 2) * _LANES, (hd // 2 + 1) * _LANES)
    dot = functools.partial(jnp.dot, preferred_element_type=_F32)
    chains = [(bb, hd) for bb in range(nb) for hd in range(_SB_HEADS)]
    n_chains = range(len(chains))

    def q_tile(qi, carry):
        qh = [q for bb in range(nb) for p in range(_SB_HEADS // 2)
              for q in _split_rows(q_ref[bb * nq + qi, pair(2 * p), :], _SB_DIM)]

        def stage(j, buf, members, diag):
            z = [dot(k_ref[_rows(qi - j, t, chains[c][0] * seq), pair(chains[c][1])], qh[c])
                 for c in members]
            log_beta, first_row, later = [], [], []
            for zi in z:
                lp = jnp.log2(1.0 + jnp.exp2(jnp.minimum(zi, -zi)))
                lb = jnp.minimum(zi, 0.0) - lp
                log_1m = lb - zi
                if diag:
                    log_1m = jnp.where(earlier, log_1m, 0.0)
                l_hi = log_1m.astype(_BF16)
                l_lo = (log_1m - l_hi.astype(_F32)).astype(_BF16)
                later.append(dot(tri2, jnp.concatenate([l_hi, l_lo], axis=0)))
                log_beta.append(lb)
                first_row.append(log_1m[0:1, :])
            for c, lb, la in zip(members, log_beta, later):
                log_w = lb + la
                buf[c] = jnp.where(earlier, log_w, -jnp.inf) if diag else log_w
            return [la[0:1, :] + fr for la, fr in zip(later, first_row)]

        def consume(j, buf, members, through, run):
            for c, r in zip(members, run):
                bb, hd = chains[c]
                a = jnp.exp2(buf[c] + r).astype(_BF16)
                acc_scr[c] = acc_scr[c] + dot(v_ref[bb * nq + qi - j, pair(hd), :], a)
            return [r + th for r, th in zip(run, through)]

        for c in n_chains:
            acc_scr[c] = jnp.zeros(acc_scr.shape[1:], _F32)
        run0 = tuple(jnp.zeros((1, t), _F32) for _ in n_chains)
        _pipelined_tiles(qi + 1, len(chains), stage, consume, run0, s_even, s_odd, group=len(chains))
        for bb in range(nb):
            for p in range(_SB_HEADS // 2):
                c = bb * _SB_HEADS + 2 * p
                o_t = jnp.concatenate([acc_scr[c, :_SB_DIM, :], acc_scr[c + 1, _SB_DIM:, :]], axis=0)
                o_ref[_rows(qi, t, bb * seq), pair(2 * p)] = (
                    _half_row_norm(o_t, _SB_DIM, g_ref[...]).T.astype(o_ref.dtype))
        return carry

    lax.fori_loop(0, nq, q_tile, 0)


def _sb_attention(q_t, k, v_t, g_t, batch, seq):
    t = _ATT_TILE
    nb = _PAIR_BATCH
    nq = seq // t
    width = _SB_HEADS * _SB_DIM
    chains = nb * _SB_HEADS
    tiles = pl.BlockSpec((nb * nq, width, t), lambda b: (b, 0, 0))
    rows = pl.BlockSpec((nb * seq, width), lambda b: (b, 0))
    return pl.pallas_call(
        functools.partial(_sb_kernel, t=t, nb=nb),
        grid=(batch // nb,),
        in_specs=[tiles, rows, tiles, _const_spec((_LANES, t))],
        out_specs=rows,
        out_shape=jax.ShapeDtypeStruct(k.shape, _BF16),
        scratch_shapes=[pltpu.VMEM((chains, t, t), _F32), pltpu.VMEM((chains, t, t), _F32),
                        pltpu.VMEM((chains, _LANES, t), _F32)],
        compiler_params=_params("parallel"),
        name="stick_breaking",
    )(q_t, k, v_t, g_t)


def _memkv_kernel(mem_ref, g_ref, w_ref, g64_ref, kg_ref, k_ref, v_ref):
    m = _rms(mem_ref[...], g_ref[...]).astype(_BF16)
    kv = jnp.dot(m, w_ref[...], preferred_element_type=_F32)
    width = _MEM_HEADS * _MEM_DIM
    k = kv[:, :width]
    k = k * lax.rsqrt(_group_mean_sq(k, g64_ref[...], _MEM_DIM) + _EPS) * kg_ref[...]
    k_ref[...] = k.astype(_BF16)
    v_ref[...] = kv[:, width:].astype(_BF16)


def _memkv(mem2d, g, w, g64, kg):
    depth = w.shape[0]
    rows, d = mem2d.shape
    tm = _ROW_TILE
    width = _MEM_HEADS * _MEM_DIM
    out = pl.BlockSpec((None, tm, width), lambda l, i: (l, i, 0))
    return pl.pallas_call(
        _memkv_kernel,
        grid=(depth, rows // tm),
        in_specs=[pl.BlockSpec((tm, d), lambda l, i: (i, 0)),
                  pl.BlockSpec((None, 1, d), lambda l, i: (l, 0, 0)),
                  pl.BlockSpec((None, d, 2 * width), lambda l, i: (l, 0, 0)),
                  _const_spec((_MXU, _MXU)),
                  pl.BlockSpec((None, 1, width), lambda l, i: (l, 0, 0))],
        out_specs=[out, out],
        out_shape=[jax.ShapeDtypeStruct((depth, rows, width), _BF16)] * 2,
        compiler_params=_params("parallel", "parallel"),
        name="memory_kv",
    )(mem2d, g, w, g64, kg)


def _mix_mem_kernel(x_ref, ya_ref, yb_ref, yc_ref, wo_ref, gx_ref, wq_ref, g64_ref, qg_ref,
                    km_ref, vm_ref, wmo_ref, o_ref):
    wa = ya_ref.shape[1]
    wb = wa + yb_ref.shape[1]
    x = (x_ref[...]
         + jnp.dot(ya_ref[...], wo_ref[:wa, :], preferred_element_type=_F32)
         + jnp.dot(yb_ref[...], wo_ref[wa:wb, :], preferred_element_type=_F32)
         + jnp.dot(yc_ref[...], wo_ref[wb:, :], preferred_element_type=_F32))
    h = _rms(x, gx_ref[...]).astype(_BF16)
    q = jnp.dot(h, wq_ref[...], preferred_element_type=_F32)
    q = (q * lax.rsqrt(_group_mean_sq(q, g64_ref[...], _MEM_DIM) + _EPS) * qg_ref[...]).astype(_BF16)
    km = km_ref[...]
    vm = vm_ref[...]
    head_of_lane = lax.broadcasted_iota(jnp.int32, (1, q.shape[1]), 1) // _MEM_DIM
    zero = jnp.zeros_like(q)
    o = jnp.zeros(q.shape, _F32)
    for hd in range(_MEM_HEADS):
        sel = head_of_lane == hd
        s = lax.dot_general(jnp.where(sel, q, zero), km, _NT, preferred_element_type=_F32)
        p = jnp.exp(s - jnp.max(s, axis=-1, keepdims=True))
        oh = jnp.dot(p.astype(_BF16), vm, preferred_element_type=_F32)
        o = jnp.where(sel, oh / jnp.sum(p, axis=-1, keepdims=True), o)
    o_ref[...] = x + jnp.dot(o.astype(_BF16), wmo_ref[...], preferred_element_type=_F32)


def _mix_mem(x2d, ya, yb, yc, lp, km, vm, seq):
    m, d = x2d.shape
    tm = _MIX_ROWS
    per_seq = seq // tm
    n_mem, width = km.shape[1], km.shape[2]
    row = lambda c: pl.BlockSpec((tm, c), lambda i: (i, 0))
    mem = pl.BlockSpec((None, n_mem, width), lambda i: (i // per_seq, 0, 0))
    return pl.pallas_call(
        _mix_mem_kernel,
        grid=(m // tm,),
        in_specs=[row(d), row(ya.shape[1]), row(yb.shape[1]), row(yc.shape[1]),
                  _const_spec(lp["w_out"].shape), _const_spec((1, d)),
                  _const_spec(lp["w_mem_q"].shape), _const_spec((_MXU, _MXU)),
                  _const_spec((1, width)), mem, mem, _const_spec(lp["w_mem_o"].shape)],
        out_specs=row(d),
        out_shape=jax.ShapeDtypeStruct((m, d), _F32),
        compiler_params=_params("parallel"),
        name="mix_and_memory",
    )(x2d, ya, yb, yc, lp["w_out"], lp["memx_g"], lp["w_mem_q"], lp["g64"], lp["mem_qg"],
      km, vm, lp["w_mem_o"])


_FF_CHUNK = 1024


def _ffn_kernel(x_ref, g_ref, w1_ref, w2_ref, o_ref):
    x = x_ref[...]
    h = _rms(x, g_ref[...]).astype(_BF16)
    acc = x
    for c in range(0, w1_ref.shape[1], _FF_CHUNK):
        u = jnp.dot(h, w1_ref[:, c:c + _FF_CHUNK], preferred_element_type=_F32)
        r = jnp.maximum(u, 0.0)
        acc = acc + jnp.dot((r * r).astype(_BF16), w2_ref[c:c + _FF_CHUNK, :],
                            preferred_element_type=_F32)
    o_ref[...] = acc


def _ffn(x2d, g, w1, w2):
    m, d = x2d.shape
    tm = _ROW_TILE
    row = pl.BlockSpec((tm, d), lambda i: (i, 0))
    return pl.pallas_call(
        _ffn_kernel,
        grid=(m // tm,),
        in_specs=[row, _const_spec((1, d)), _const_spec(w1.shape), _const_spec(w2.shape)],
        out_specs=row,
        out_shape=jax.ShapeDtypeStruct((m, d), _F32),
        compiler_params=_params("parallel"),
        name="ffn",
    )(x2d, g, w1, w2)


def _t5_bucket(rel):
    nb = _NUM_BUCKETS // 2
    bucket = (rel > 0).astype(jnp.int32) * nb
    n = jnp.abs(rel)
    max_exact = nb // 2
    is_small = n < max_exact
    large = max_exact + (jnp.log(jnp.maximum(n, 1).astype(jnp.float32) / max_exact)
                         / math.log(_MAX_DISTANCE / max_exact) * (nb - max_exact)).astype(jnp.int32)
    large = jnp.minimum(large, nb - 1)
    return bucket + jnp.where(is_small, n, large)


def _da_bias_tables(rel_bias, t):
    assert t + 1 >= _MAX_DISTANCE and t % _CHUNK == 0
    j = jnp.arange(t, dtype=jnp.int32)[:, None]
    i = jnp.arange(t, dtype=jnp.int32)[None, :]
    rb = rel_bias.astype(_F32) * _LOG2E

    def lookup(bucket):
        hit = bucket[:, :, None, None] == jnp.arange(_NUM_BUCKETS, dtype=jnp.int32)[:, None]
        return jnp.sum(jnp.where(hit, rb[None, None], 0.0), axis=2)

    b0 = lookup(_t5_bucket(j - i))
    b0 = jnp.where(((j // _CHUNK) <= (i // _CHUNK))[:, :, None], b0, -jnp.inf)
    b1 = lookup(_t5_bucket(j - i - t))
    far = jnp.broadcast_to(lookup(_t5_bucket(jnp.full((1, 1), -(t + 1), jnp.int32))), b1.shape)
    tab = jnp.stack([b0, b1, far]).reshape(3, t, t, _DA_HEADS, 2)
    return tab.transpose(3, 0, 4, 1, 2)


def _rope_tables(seq):
    half = _MLA_ROPE // 2
    freqs = _ROPE_THETA ** (-jnp.arange(half, dtype=jnp.float32) / half)
    ang = jnp.arange(seq, dtype=jnp.int32).astype(jnp.float32)[:, None] * freqs[None, :]
    cos, sin = jnp.cos(ang), jnp.sin(ang)
    ones = jnp.ones((seq, _MLA_NOPE), _F32)
    z = lambda w: jnp.zeros((seq, w), _F32)
    tail = _LANES - _MLA_QK
    c = jnp.concatenate([ones, cos, cos, z(tail)], axis=1)
    s1 = jnp.concatenate([z(_MLA_NOPE + half), sin, z(tail)], axis=1)
    s2 = jnp.concatenate([z(_MLA_NOPE), -sin, z(half + tail)], axis=1)
    return c, s1, s2


def _group_ones(group):
    idx = np.arange(_MXU) // group
    return jnp.asarray(idx[:, None] == idx[None, :], dtype=_BF16)


def _layer_params(l, p, rope, g64):
    d = p["w_in"].shape[1]
    o = _IN_OFFS
    w_in = p["w_in"][l]
    kr = jnp.zeros((d, _LANES), _F32).at[:, _MLA_NOPE:_MLA_QK].set(w_in[:, o[8]:o[9]])
    w_uq = jnp.pad(p["w_mla_uq"][l].reshape(_MLA_Q_RANK, _MLA_HEADS, _MLA_QK),
                   ((0, 0), (0, 0), (0, _LANES - _MLA_QK))).reshape(_MLA_Q_RANK, -1)
    w_ukv = p["w_mla_ukv"][l].reshape(_MLA_KV_RANK, _MLA_HEADS, _MLA_NOPE + _MLA_V)
    w_k = jnp.pad(w_ukv[:, :, :_MLA_NOPE], ((0, 0), (0, 0), (0, _LANES - _MLA_NOPE)))
    w_v = w_ukv[:, :, _MLA_NOPE:]
    pad_g = lambda g: jnp.tile(jnp.pad(g, (0, _LANES - _MLA_QK)), _MLA_HEADS)[None]
    col = lambda g: jnp.broadcast_to(g[:, None], (g.shape[0], _ATT_TILE))
    lam_init = 0.8 - 0.6 * math.exp(-0.3 * l)
    lp = p["da_lambda"][l].astype(_F32)
    lam = jnp.exp(jnp.sum(lp[0] * lp[1])) - jnp.exp(jnp.sum(lp[2] * lp[3])) + lam_init
    return {
        "mix_g": p["mix_norm_g"][l][None],
        "w_in": jnp.concatenate([w_in[:, :o[8]], kr], axis=1).astype(_BF16),
        "g64": g64,
        "da_qg": jnp.tile(p["da_q_norm_g"][l], 2 * _DA_HEADS)[None] * (_DA_DIM ** -0.5 * _LOG2E),
        "da_kg": jnp.tile(p["da_k_norm_g"][l], 2 * _DA_HEADS)[None],
        "cq_g": p["mla_cq_norm_g"][l][None],
        "ckv_g": p["mla_ckv_norm_g"][l][None],
        "w_uq": w_uq.astype(_BF16),
        "w_ukv": jnp.concatenate([w_k.reshape(_MLA_KV_RANK, -1), w_v.reshape(_MLA_KV_RANK, -1)],
                                 axis=1).astype(_BF16),
        "mla_qg": pad_g(p["mla_q_norm_g"][l]) * (_MLA_QK ** -0.5 * _LOG2E),
        "mla_kg": pad_g(p["mla_k_norm_g"][l]),
        "rope_cos": rope[0], "rope_s1": rope[1], "rope_s2": rope[2],
        "lam": jnp.reshape(lam, (1,)).astype(_F32),
        "da_og": col(p["da_subln_g"][l] * (1.0 - lam_init)),
        "sb_og": col(jnp.tile(p["sb_out_g"][l], 2)),
        "mla_og": col(jnp.tile(p["mla_out_g"][l], 2)),
        "w_out": p["w_out"][l].astype(_BF16),
        "memx_g": p["memx_norm_g"][l][None],
        "w_mem_q": p["w_mem_q"][l].astype(_BF16),
        "mem_qg": jnp.tile(p["mem_q_norm_g"][l], _MEM_HEADS)[None] * (_MEM_DIM ** -0.5),
        "w_mem_o": p["w_mem_o"][l].astype(_BF16),
        "ffn_g": p["ffn_norm_g"][l][None],
        "w_ff1": p["w_ff1"][l].astype(_BF16),
        "w_ff2": p["w_ff2"][l].astype(_BF16),
    }


def kernel(x, mem, rel_bias, mix_norm_g, w_in, da_q_norm_g, da_k_norm_g, da_lambda, da_subln_g,
           sb_out_g, mla_cq_norm_g, mla_ckv_norm_g, w_mla_uq, w_mla_ukv, mla_q_norm_g, mla_k_norm_g,
           mla_out_g, w_out, memx_norm_g, mem_norm_g, w_mem_q, w_mem_kv, mem_q_norm_g, mem_k_norm_g,
           w_mem_o, ffn_norm_g, w_ff1, w_ff2):
    p = dict(mix_norm_g=mix_norm_g, w_in=w_in, da_q_norm_g=da_q_norm_g, da_k_norm_g=da_k_norm_g,
             da_lambda=da_lambda, da_subln_g=da_subln_g, sb_out_g=sb_out_g,
             mla_cq_norm_g=mla_cq_norm_g, mla_ckv_norm_g=mla_ckv_norm_g, w_mla_uq=w_mla_uq,
             w_mla_ukv=w_mla_ukv, mla_q_norm_g=mla_q_norm_g, mla_k_norm_g=mla_k_norm_g,
             mla_out_g=mla_out_g, w_out=w_out, memx_norm_g=memx_norm_g, w_mem_q=w_mem_q,
             mem_q_norm_g=mem_q_norm_g, w_mem_o=w_mem_o, ffn_norm_g=ffn_norm_g, w_ff1=w_ff1,
             w_ff2=w_ff2)
    batch, seq, d = x.shape
    depth = w_in.shape[0]
    n_mem = mem.shape[1]
    t = _ATT_TILE
    assert seq % _ROW_TILE == 0 and seq % _MIX_ROWS == 0 and seq % t == 0
    assert (batch * n_mem) % _ROW_TILE == 0
    assert batch % _PAIR_BATCH == 0
    assert w_in.shape[2] == _IN_OFFS[-1]

    g64 = _group_ones(_MEM_DIM)
    rope = _rope_tables(seq)
    bias = _da_bias_tables(rel_bias, t)
    i = jnp.arange(t, dtype=jnp.int32)
    chunk_mask = jnp.where((i[:, None] // _CHUNK) <= (i[None, :] // _CHUNK), 0.0, -jnp.inf).astype(_F32)
    chunk_mask = jnp.stack([chunk_mask, jnp.zeros_like(chunk_mask)])

    width = _MEM_HEADS * _MEM_DIM
    km, vm = _memkv(mem.reshape(batch * n_mem, d), mem_norm_g[:, None, :], w_mem_kv.astype(_BF16), g64,
                    jnp.tile(mem_k_norm_g, (1, _MEM_HEADS))[:, None, :])
    km = km.reshape(depth, batch, n_mem, width)
    vm = vm.reshape(depth, batch, n_mem, width)

    x2d = x.reshape(batch * seq, d)
    for l in range(depth):
        lp = _layer_params(l, p, rope, g64)
        daq, dak, dav, sbq, sbk, sbv, mq, mk, mv = _inproj(x2d, lp, seq)
        ya = _da_attention(daq, dak, dav, bias, lp["lam"], lp["da_og"], batch, seq)
        yb = _sb_attention(sbq, sbk, sbv, lp["sb_og"], batch, seq)
        yc = _mla_attention(mq, mk, mv, chunk_mask, lp["mla_og"], batch, seq)
        x2d = _mix_mem(x2d, ya, yb, yc, lp, km[l], vm[l], seq)
        x2d = _ffn(x2d, lp["ffn_g"], lp["w_ff1"], lp["w_ff2"])
    return x2d.reshape(batch, seq, d)
```

```python
import functools
import math

import numpy as np
import jax
import jax.numpy as jnp
from jax import lax
from jax.experimental import pallas as pl
from jax.experimental.pallas import tpu as pltpu

_F32 = jnp.float32
_BF16 = jnp.bfloat16
_EPS = 1e-6

_CHUNK = 64
_DA_HEADS, _DA_DIM = 4, 64
_DA_VDIM = 2 * _DA_DIM
_SB_HEADS, _SB_DIM = 4, 64
_MLA_HEADS, _MLA_NOPE, _MLA_ROPE, _MLA_V = 4, 64, 32, 64
_MLA_QK = _MLA_NOPE + _MLA_ROPE
_MLA_Q_RANK, _MLA_KV_RANK = 256, 128
_ROPE_THETA = 10000.0
_NUM_BUCKETS, _MAX_DISTANCE = 32, 128
_MEM_HEADS, _MEM_DIM = 4, 64

_LANES = 128
_MXU = 256
_VMEM_LIMIT = 52 * 1024 * 1024

_ATT_TILE = 256
_ROW_TILE = 512
_MIX_ROWS = 1024

_IN_SIZES = (512, 512, 512, 256, 256, 256, _MLA_Q_RANK, _MLA_KV_RANK, _MLA_ROPE)
_IN_OFFS = tuple(int(v) for v in np.cumsum((0,) + _IN_SIZES))
_NT = (((1,), (1,)), ((), ()))
_LOG2E = math.log2(math.e)


def _const_spec(shape):
    zeros = (0,) * len(shape)
    return pl.BlockSpec(shape, lambda *_: zeros, pipeline_mode=pl.Buffered(1))


def _params(*sem):
    return pltpu.CompilerParams(dimension_semantics=sem, vmem_limit_bytes=_VMEM_LIMIT)


def _rms(x, g):
    return x * lax.rsqrt(jnp.mean(x * x, axis=-1, keepdims=True) + _EPS) * g


def _group_mean_sq(y, gmat, group):
    sq = (y * y).astype(_BF16)
    cols = y.shape[1]
    parts = [jnp.dot(sq[:, c:c + _MXU], gmat, preferred_element_type=_F32)
             for c in range(0, cols, _MXU)]
    ss = parts[0] if len(parts) == 1 else jnp.concatenate(parts, axis=1)
    return ss * (1.0 / group)


def _inproj_kernel(x_ref, gmix_ref, w_ref, g64_ref, gq_ref, gk_ref, cqg_ref, ckvg_ref,
                   wuq_ref, wukv_ref, qg_ref, kg_ref, cos_ref, s1_ref, s2_ref,
                   daq_ref, dak_ref, dav_ref, sbq_ref, sbk_ref, sbv_ref,
                   mq_ref, mk_ref, mv_ref):
    x = x_ref[...]
    h = _rms(x, gmix_ref[...]).astype(_BF16)
    o = _IN_OFFS

    def proj(seg, width=None):
        hi = o[seg + 1] if width is None else o[seg] + width
        return jnp.dot(h, w_ref[:, o[seg]:hi], preferred_element_type=_F32)

    g64 = g64_ref[...]
    t = daq_ref.shape[2]

    def norm64(y, g):
        return y * lax.rsqrt(_group_mean_sq(y, g64, _DA_DIM) + _EPS) * g

    def store_t(ref, y, row0=0):
        for r in range(y.shape[0] // t):
            ref[r, row0:row0 + y.shape[1], :] = y[r * t:(r + 1) * t, :].T.astype(_BF16)

    cos, s1, s2 = cos_ref[...], s1_ref[...], s2_ref[...]

    def head_norm_rope(y, g):
        ms = jnp.sum(y * y, axis=-1, keepdims=True) * (1.0 / _MLA_QK)
        yn = y * lax.rsqrt(ms + _EPS) * g
        half = _MLA_ROPE // 2
        return (yn * cos + pltpu.roll(yn, half, 1) * s1
                + pltpu.roll(yn, _LANES - half, 1) * s2)

    cq = _rms(proj(6), cqg_ref[...]).astype(_BF16)
    q_all = jnp.dot(cq, wuq_ref[...], preferred_element_type=_F32)
    ckv = _rms(proj(7), ckvg_ref[...]).astype(_BF16)
    kv_all = jnp.dot(ckv, wukv_ref[...], preferred_element_type=_F32)
    k_rope = proj(8, _LANES)

    def latent_head(hd):
        sl = slice(hd * _LANES, (hd + 1) * _LANES)
        store_t(mq_ref, head_norm_rope(q_all[:, sl], qg_ref[:, sl]), hd * _LANES)
        mk_ref[:, sl] = head_norm_rope(kv_all[:, sl] + k_rope, kg_ref[:, sl]).astype(_BF16)

    store_t(daq_ref, norm64(proj(0), gq_ref[...]))
    latent_head(0)
    dak_ref[...] = norm64(proj(1), gk_ref[...]).astype(_BF16)
    latent_head(1)
    store_t(dav_ref, proj(2))
    latent_head(2)
    store_t(sbq_ref, proj(3) * (_SB_DIM ** -0.5 * _LOG2E))
    latent_head(3)
    sbk_ref[...] = proj(4).astype(_BF16)
    store_t(mv_ref, kv_all[:, _MLA_HEADS * _LANES:])
    store_t(sbv_ref, proj(5))


def _inproj(x2d, lp, seq):
    m, d = x2d.shape
    tm = _ROW_TILE
    t = _ATT_TILE
    pos_blocks = seq // tm
    tab = pl.BlockSpec((tm, _LANES), lambda i: (i % pos_blocks, 0))
    widths = (512, 512, 512, 256, 256, 256, 512, 512, 256)
    transposed = (True, False, True, True, False, True, True, False, True)
    out_specs, out_shape = [], []
    for c, tr in zip(widths, transposed):
        if tr:
            out_specs.append(pl.BlockSpec((tm // t, c, t), lambda i: (i, 0, 0)))
            out_shape.append(jax.ShapeDtypeStruct((m // t, c, t), _BF16))
        else:
            out_specs.append(pl.BlockSpec((tm, c), lambda i: (i, 0)))
            out_shape.append(jax.ShapeDtypeStruct((m, c), _BF16))
    row = lambda c: pl.BlockSpec((tm, c), lambda i: (i, 0))
    return pl.pallas_call(
        _inproj_kernel,
        grid=(m // tm,),
        in_specs=[row(d), _const_spec((1, d)), _const_spec(lp["w_in"].shape),
                  _const_spec((_MXU, _MXU)), _const_spec((1, 512)), _const_spec((1, 512)),
                  _const_spec((1, _MLA_Q_RANK)), _const_spec((1, _MLA_KV_RANK)),
                  _const_spec(lp["w_uq"].shape), _const_spec(lp["w_ukv"].shape),
                  _const_spec((1, 512)), _const_spec((1, 512)), tab, tab, tab],
        out_specs=out_specs,
        out_shape=out_shape,
        compiler_params=_params("parallel"),
        name="inproj",
    )(x2d, lp["mix_g"], lp["w_in"], lp["g64"], lp["da_qg"], lp["da_kg"], lp["cq_g"],
      lp["ckv_g"], lp["w_uq"], lp["w_ukv"], lp["mla_qg"], lp["mla_kg"],
      lp["rope_cos"], lp["rope_s1"], lp["rope_s2"])


_SUM_ROWS = 16


def _pipelined_tiles(n_tiles, n_chains, stage, consume, run, s_even, s_odd, group):
    groups = [range(g, min(g + group, n_chains)) for g in range(0, n_chains, group)]

    def overlap(j_stage, buf_stage, j_done, buf_done, summary, run):
        new_summary, new_run = [], []
        for members in groups:
            new_summary += stage(j_stage, buf_stage, members, False)
            new_run += consume(j_done, buf_done, members,
                               [summary[c] for c in members], [run[c] for c in members])
        return tuple(new_run), tuple(new_summary)

    def finish(j, buf, summary, run):
        return tuple(consume(j, buf, range(n_chains), list(summary), list(run)))

    def pair(i, carry):
        run, summary = carry
        run, summary = overlap(2 * i + 1, s_odd, 2 * i, s_even, summary, run)
        return overlap(2 * i + 2, s_even, 2 * i + 1, s_odd, summary, run)

    def tail_two(_, carry):
        run, summary = carry
        run, summary = overlap(n_tiles - 1, s_odd, n_tiles - 2, s_even, summary, run)
        return finish(n_tiles - 1, s_odd, summary, run), summary

    def tail_one(_, carry):
        run, summary = carry
        return finish(n_tiles - 1, s_even, summary, run), summary

    carry = (run, tuple(stage(0, s_even, range(n_chains), True)))
    last = n_tiles - 1
    odd = jnp.bitwise_and(last, 1)
    carry = lax.fori_loop(0, jnp.right_shift(last, 1), pair, carry)
    carry = lax.fori_loop(0, odd, tail_two, carry)
    lax.fori_loop(0, 1 - odd, tail_one, carry)


def _online_softmax(n_tiles, score, value, s_even, s_odd, acc_scr):
    n_chains, _, t = s_even.shape
    ones = jnp.ones((_SUM_ROWS, t), _BF16)

    def stage(ki, buf, members, first):
        col_max = []
        for c in members:
            s = score(c, ki)
            buf[c] = s
            col_max.append(jnp.max(s, axis=0, keepdims=True))
        return col_max

    def consume(ki, buf, members, col_max, m_run):
        out = []
        for c, cm, m_old in zip(members, col_max, m_run):
            m_new = jnp.maximum(m_old, cm)
            alpha = jnp.exp2(m_old - m_new)
            p = jnp.exp2(buf[c] - m_new).astype(_BF16)
            v_ones = jnp.concatenate([value(c, ki), ones], axis=0)
            acc_scr[c] = alpha * acc_scr[c] + jnp.dot(v_ones, p, preferred_element_type=_F32)
            out.append(m_new)
        return out

    for c in range(n_chains):
        acc_scr[c] = jnp.zeros(acc_scr.shape[1:], _F32)
    m_init = tuple(jnp.full((1, t), -jnp.inf, _F32) for _ in range(n_chains))
    _pipelined_tiles(n_tiles, n_chains, stage, consume, m_init, s_even, s_odd, group=1)


def _softmax_finish(acc, rows):
    width = acc.shape[0] - _SUM_ROWS
    return acc[rows] * (1.0 / acc[width:width + 1])


def _rows(i, t, base=0):
    return pl.ds(pl.multiple_of(base + i * t, t), t)


def _split_rows(x_t, half):
    zero = jnp.zeros((half, x_t.shape[1]), x_t.dtype)
    return (jnp.concatenate([x_t[:half], zero], axis=0),
            jnp.concatenate([zero, x_t[half:]], axis=0))


def _half_row_norm(o_t, half, g_t):
    sq = o_t * o_t
    lo = lax.rsqrt(jnp.mean(sq[:half], axis=0, keepdims=True) + _EPS)
    hi = lax.rsqrt(jnp.mean(sq[half:], axis=0, keepdims=True) + _EPS)
    return jnp.concatenate([o_t[:half] * lo, o_t[half:] * hi], axis=0) * g_t


def _da_kernel(lam_ref, q_ref, k_ref, v_ref, bias_ref, g_ref, o_ref, s_even, s_odd, acc_scr, *, t):
    nq = q_ref.shape[0]
    lam = lam_ref[0]

    heads = range(_DA_HEADS)
    head = lambda hd: slice(hd * _LANES, (hd + 1) * _LANES)
    all_rows = slice(0, _DA_VDIM)

    def q_tile(qi, carry):
        qm = [qh for hd in heads for qh in _split_rows(q_ref[qi, head(hd), :], _DA_DIM)]

        def score(c, ki):
            hd, mi = divmod(c, 2)
            d = jnp.minimum(qi - ki, 2)
            return (jnp.dot(k_ref[_rows(ki, t), head(hd)], qm[c], preferred_element_type=_F32)
                    + bias_ref[hd, d, mi])

        value = lambda c, ki: v_ref[ki, head(c // 2), :]
        _online_softmax(qi + 1, score, value, s_even, s_odd, acc_scr)
        for hd in heads:
            o_t = (_softmax_finish(acc_scr[2 * hd], all_rows)
                   - lam * _softmax_finish(acc_scr[2 * hd + 1], all_rows))
            o_t = o_t * lax.rsqrt(jnp.mean(o_t * o_t, axis=0, keepdims=True) + _EPS) * g_ref[...]
            o_ref[_rows(qi, t), head(hd)] = o_t.T.astype(o_ref.dtype)
        return carry

    lax.fori_loop(0, nq, q_tile, 0)


def _da_attention(q_t, k, v_t, bias, lam, g_t, batch, seq):
    t = _ATT_TILE
    nq = seq // t
    width = _DA_HEADS * _LANES
    tiles = pl.BlockSpec((nq, width, t), lambda b: (b, 0, 0))
    rows = pl.BlockSpec((seq, width), lambda b: (b, 0))
    return pl.pallas_call(
        functools.partial(_da_kernel, t=t),
        grid=(batch,),
        in_specs=[pl.BlockSpec(memory_space=pltpu.SMEM), tiles, rows, tiles,
                  _const_spec(bias.shape), _const_spec((_LANES, t))],
        out_specs=rows,
        out_shape=jax.ShapeDtypeStruct(k.shape, _BF16),
        scratch_shapes=[pltpu.VMEM((2 * _DA_HEADS, t, t), _F32), pltpu.VMEM((2 * _DA_HEADS, t, t), _F32),
                        pltpu.VMEM((2 * _DA_HEADS, _DA_VDIM + _SUM_ROWS, t), _F32)],
        compiler_params=_params("parallel"),
        name="diff_attention",
    )(lam, q_t, k, v_t, bias, g_t)


_PAIR_BATCH = 2


def _mla_kernel(q_ref, k_ref, v_ref, mask_ref, g_ref, o_ref, s_even, s_odd, acc_scr, *, t, nb):
    nq = q_ref.shape[0] // nb
    seq = k_ref.shape[0] // nb

    head = lambda hd: slice(hd * _LANES, (hd + 1) * _LANES)
    pair = lambda hd: slice((hd // 2) * _LANES, (hd // 2 + 1) * _LANES)
    chains = [(bb, hd) for bb in range(nb) for hd in range(_MLA_HEADS)]

    def q_tile(qi, carry):
        qs = [q_ref[bb * nq + qi, head(hd), :] for bb, hd in chains]

        def score(c, ki):
            bb, hd = chains[c]
            d = jnp.minimum(qi - ki, 1)
            return (jnp.dot(k_ref[_rows(ki, t, bb * seq), head(hd)], qs[c], preferred_element_type=_F32)
                    + mask_ref[d])

        value = lambda c, ki: v_ref[chains[c][0] * nq + ki, pair(chains[c][1]), :]
        _online_softmax(qi + 1, score, value, s_even, s_odd, acc_scr)
        for bb in range(nb):
            for p in range(_MLA_HEADS // 2):
                c = bb * _MLA_HEADS + 2 * p
                o_t = jnp.concatenate([_softmax_finish(acc_scr[c], slice(0, _MLA_V)),
                                       _softmax_finish(acc_scr[c + 1], slice(_MLA_V, _LANES))], axis=0)
                o_ref[_rows(qi, t, bb * seq), head(p)] = (
                    _half_row_norm(o_t, _MLA_V, g_ref[...]).T.astype(o_ref.dtype))
        return carry

    lax.fori_loop(0, nq, q_tile, 0)


def _mla_attention(q_t, k, v_t, mask_t, g_t, batch, seq):
    t = _ATT_TILE
    nb = _PAIR_BATCH
    nq = seq // t
    qk_width = _MLA_HEADS * _LANES
    v_width = _MLA_HEADS * _MLA_V
    chains = nb * _MLA_HEADS
    return pl.pallas_call(
        functools.partial(_mla_kernel, t=t, nb=nb),
        grid=(batch // nb,),
        in_specs=[pl.BlockSpec((nb * nq, qk_width, t), lambda b: (b, 0, 0)),
                  pl.BlockSpec((nb * seq, qk_width), lambda b: (b, 0)),
                  pl.BlockSpec((nb * nq, v_width, t), lambda b: (b, 0, 0)),
                  _const_spec((2, t, t)), _const_spec((_LANES, t))],
        out_specs=pl.BlockSpec((nb * seq, v_width), lambda b: (b, 0)),
        out_shape=jax.ShapeDtypeStruct((batch * seq, v_width), _BF16),
        scratch_shapes=[pltpu.VMEM((chains, t, t), _F32), pltpu.VMEM((chains, t, t), _F32),
                        pltpu.VMEM((chains, _LANES + _SUM_ROWS, t), _F32)],
        compiler_params=_params("parallel"),
        name="latent_attention",
    )(q_t, k, v_t, mask_t, g_t)


def _sb_kernel(q_ref, k_ref, v_ref, g_ref, o_ref, s_even, s_odd, acc_scr, *, t, nb):
    nq = q_ref.shape[0] // nb
    seq = k_ref.shape[0] // nb
    key_idx = lax.broadcasted_iota(jnp.int32, (t, t), 0)
    query_idx = lax.broadcasted_iota(jnp.int32, (t, t), 1)
    earlier = key_idx < query_idx
    tri = jnp.where(earlier, 1.0, 0.0).astype(_BF16)

    pair = lambda hd: slice((hd // 2) * _LANES, (hd // 2 + 1) * _LANES)
    dot = functools.partial(jnp.dot, preferred_element_type=_F32)
    chains = [(bb, hd) for bb in range(nb) for hd in range(_SB_HEADS)]
    n_chains = range(len(chains))

    def q_tile(qi, carry):
        qh = [q for bb in range(nb) for p in range(_SB_HEADS // 2)
              for q in _split_rows(q_ref[bb * nq + qi, pair(2 * p), :], _SB_DIM)]

        def stage(j, buf, members, diag):
            z = [dot(k_ref[_rows(qi - j, t, chains[c][0] * seq), pair(chains[c][1])], qh[c])
                 for c in members]
            log_beta, first_row, later = [], [], []
            for zi in z:
                lp = jnp.log2(1.0 + jnp.exp2(jnp.minimum(zi, -zi)))
                lb = jnp.minimum(zi, 0.0) - lp
                log_1m = lb - zi
                if diag:
                    log_1m = jnp.where(earlier, log_1m, 0.0)
                later.append(dot(tri, log_1m.astype(_BF16)))
                log_beta.append(lb)
                first_row.append(log_1m[0:1, :])
            for c, lb, la in zip(members, log_beta, later):
                log_w = lb + la
                buf[c] = jnp.where(earlier, log_w, -jnp.inf) if diag else log_w
            return [la[0:1, :] + fr for la, fr in zip(later, first_row)]

        def consume(j, buf, members, through, run):
            for c, r in zip(members, run):
                bb, hd = chains[c]
                a = jnp.exp2(buf[c] + r).astype(_BF16)
                acc_scr[c] = acc_scr[c] + dot(v_ref[bb * nq + qi - j, pair(hd), :], a)
            return [r + th for r, th in zip(run, through)]

        for c in n_chains:
            acc_scr[c] = jnp.zeros(acc_scr.shape[1:], _F32)
        run0 = tuple(jnp.zeros((1, t), _F32) for _ in n_chains)
        _pipelined_tiles(qi + 1, len(chains), stage, consume, run0, s_even, s_odd, group=len(chains))
        for bb in range(nb):
            for p in range(_SB_HEADS // 2):
                c = bb * _SB_HEADS + 2 * p
                o_t = jnp.concatenate([acc_scr[c, :_SB_DIM, :], acc_scr[c + 1, _SB_DIM:, :]], axis=0)
                o_ref[_rows(qi, t, bb * seq), pair(2 * p)] = (
                    _half_row_norm(o_t, _SB_DIM, g_ref[...]).T.astype(o_ref.dtype))
        return carry

    lax.fori_loop(0, nq, q_tile, 0)


def _sb_attention(q_t, k, v_t, g_t, batch, seq):
    t = _ATT_TILE
    nb = _PAIR_BATCH
    nq = seq // t
    width = _SB_HEADS * _SB_DIM
    chains = nb * _SB_HEADS
    tiles = pl.BlockSpec((nb * nq, width, t), lambda b: (b, 0, 0))
    rows = pl.BlockSpec((nb * seq, width), lambda b: (b, 0))
    return pl.pallas_call(
        functools.partial(_sb_kernel, t=t, nb=nb),
        grid=(batch // nb,),
        in_specs=[tiles, rows, tiles, _const_spec((_LANES, t))],
        out_specs=rows,
        out_shape=jax.ShapeDtypeStruct(k.shape, _BF16),
        scratch_shapes=[pltpu.VMEM((chains, t, t), _F32), pltpu.VMEM((chains, t, t), _F32),
                        pltpu.VMEM((chains, _LANES, t), _F32)],
        compiler_params=_params("parallel"),
        name="stick_breaking",
    )(q_t, k, v_t, g_t)


def _memkv_kernel(mem_ref, g_ref, w_ref, g64_ref, kg_ref, k_ref, v_ref):
    m = _rms(mem_ref[...], g_ref[...]).astype(_BF16)
    kv = jnp.dot(m, w_ref[...], preferred_element_type=_F32)
    width = _MEM_HEADS * _MEM_DIM
    k = kv[:, :width]
    k = k * lax.rsqrt(_group_mean_sq(k, g64_ref[...], _MEM_DIM) + _EPS) * kg_ref[...]
    k_ref[...] = k.astype(_BF16)
    v_ref[...] = kv[:, width:].astype(_BF16)


def _memkv(mem2d, g, w, g64, kg):
    depth = w.shape[0]
    rows, d = mem2d.shape
    tm = _ROW_TILE
    width = _MEM_HEADS * _MEM_DIM
    out = pl.BlockSpec((None, tm, width), lambda l, i: (l, i, 0))
    return pl.pallas_call(
        _memkv_kernel,
        grid=(depth, rows // tm),
        in_specs=[pl.BlockSpec((tm, d), lambda l, i: (i, 0)),
                  pl.BlockSpec((None, 1, d), lambda l, i: (l, 0, 0)),
                  pl.BlockSpec((None, d, 2 * width), lambda l, i: (l, 0, 0)),
                  _const_spec((_MXU, _MXU)),
                  pl.BlockSpec((None, 1, width), lambda l, i: (l, 0, 0))],
        out_specs=[out, out],
        out_shape=[jax.ShapeDtypeStruct((depth, rows, width), _BF16)] * 2,
        compiler_params=_params("parallel", "parallel"),
        name="memory_kv",
    )(mem2d, g, w, g64, kg)


def _mix_mem_kernel(x_ref, ya_ref, yb_ref, yc_ref, wo_ref, gx_ref, wq_ref, g64_ref, qg_ref,
                    km_ref, vm_ref, wmo_ref, o_ref):
    wa = ya_ref.shape[1]
    wb = wa + yb_ref.shape[1]
    x = (x_ref[...]
         + jnp.dot(ya_ref[...], wo_ref[:wa, :], preferred_element_type=_F32)
         + jnp.dot(yb_ref[...], wo_ref[wa:wb, :], preferred_element_type=_F32)
         + jnp.dot(yc_ref[...], wo_ref[wb:, :], preferred_element_type=_F32))
    h = _rms(x, gx_ref[...]).astype(_BF16)
    q = jnp.dot(h, wq_ref[...], preferred_element_type=_F32)
    q = (q * lax.rsqrt(_group_mean_sq(q, g64_ref[...], _MEM_DIM) + _EPS) * qg_ref[...]).astype(_BF16)
    km = km_ref[...]
    vm = vm_ref[...]
    head_of_lane = lax.broadcasted_iota(jnp.int32, (1, q.shape[1]), 1) // _MEM_DIM
    zero = jnp.zeros_like(q)
    o = jnp.zeros(q.shape, _F32)
    for hd in range(_MEM_HEADS):
        sel = head_of_lane == hd
        s = lax.dot_general(jnp.where(sel, q, zero), km, _NT, preferred_element_type=_F32)
        p = jnp.exp(s - jnp.max(s, axis=-1, keepdims=True))
        oh = jnp.dot(p.astype(_BF16), vm, preferred_element_type=_F32)
        o = jnp.where(sel, oh / jnp.sum(p, axis=-1, keepdims=True), o)
    o_ref[...] = x + jnp.dot(o.astype(_BF16), wmo_ref[...], preferred_element_type=_F32)


def _mix_mem(x2d, ya, yb, yc, lp, km, vm, seq):
    m, d = x2d.shape
    tm = _MIX_ROWS
    per_seq = seq // tm
    n_mem, width = km.shape[1], km.shape[2]
    row = lambda c: pl.BlockSpec((tm, c), lambda i: (i, 0))
    mem = pl.BlockSpec((None, n_mem, width), lambda i: (i // per_seq, 0, 0))
    return pl.pallas_call(
        _mix_mem_kernel,
        grid=(m // tm,),
        in_specs=[row(d), row(ya.shape[1]), row(yb.shape[1]), row(yc.shape[1]),
                  _const_spec(lp["w_out"].shape), _const_spec((1, d)),
                  _const_spec(lp["w_mem_q"].shape), _const_spec((_MXU, _MXU)),
                  _const_spec((1, width)), mem, mem, _const_spec(lp["w_mem_o"].shape)],
        out_specs=row(d),
        out_shape=jax.ShapeDtypeStruct((m, d), _F32),
        compiler_params=_params("parallel"),
        name="mix_and_memory",
    )(x2d, ya, yb, yc, lp["w_out"], lp["memx_g"], lp["w_mem_q"], lp["g64"], lp["mem_qg"],
      km, vm, lp["w_mem_o"])


_FF_CHUNK = 1024


def _ffn_kernel(x_ref, g_ref, w1_ref, w2_ref, o_ref):
    x = x_ref[...]
    h = _rms(x, g_ref[...]).astype(_BF16)
    acc = x
    for c in range(0, w1_ref.shape[1], _FF_CHUNK):
        u = jnp.dot(h, w1_ref[:, c:c + _FF_CHUNK], preferred_element_type=_F32)
        r = jnp.maximum(u, 0.0)
        acc = acc + jnp.dot((r * r).astype(_BF16), w2_ref[c:c + _FF_CHUNK, :],
                            preferred_element_type=_F32)
    o_ref[...] = acc


def _ffn(x2d, g, w1, w2):
    m, d = x2d.shape
    tm = _ROW_TILE
    row = pl.BlockSpec((tm, d), lambda i: (i, 0))
    return pl.pallas_call(
        _ffn_kernel,
        grid=(m // tm,),
        in_specs=[row, _const_spec((1, d)), _const_spec(w1.shape), _const_spec(w2.shape)],
        out_specs=row,
        out_shape=jax.ShapeDtypeStruct((m, d), _F32),
        compiler_params=_params("parallel"),
        name="ffn",
    )(x2d, g, w1, w2)


def _t5_bucket(rel):
    nb = _NUM_BUCKETS // 2
    bucket = (rel > 0).astype(jnp.int32) * nb
    n = jnp.abs(rel)
    max_exact = nb // 2
    is_small = n < max_exact
    large = max_exact + (jnp.log(jnp.maximum(n, 1).astype(jnp.float32) / max_exact)
                         / math.log(_MAX_DISTANCE / max_exact) * (nb - max_exact)).astype(jnp.int32)
    large = jnp.minimum(large, nb - 1)
    return bucket + jnp.where(is_small, n, large)


def _da_bias_tables(rel_bias, t):
    assert t + 1 >= _MAX_DISTANCE and t % _CHUNK == 0
    j = jnp.arange(t, dtype=jnp.int32)[:, None]
    i = jnp.arange(t, dtype=jnp.int32)[None, :]
    rb = rel_bias.astype(_F32) * _LOG2E

    def lookup(bucket):
        hit = bucket[:, :, None, None] == jnp.arange(_NUM_BUCKETS, dtype=jnp.int32)[:, None]
        return jnp.sum(jnp.where(hit, rb[None, None], 0.0), axis=2)

    b0 = lookup(_t5_bucket(j - i))
    b0 = jnp.where(((j // _CHUNK) <= (i // _CHUNK))[:, :, None], b0, -jnp.inf)
    b1 = lookup(_t5_bucket(j - i - t))
    far = jnp.broadcast_to(lookup(_t5_bucket(jnp.full((1, 1), -(t + 1), jnp.int32))), b1.shape)
    tab = jnp.stack([b0, b1, far]).reshape(3, t, t, _DA_HEADS, 2)
    return tab.transpose(3, 0, 4, 1, 2)


def _rope_tables(seq):
    half = _MLA_ROPE // 2
    freqs = _ROPE_THETA ** (-jnp.arange(half, dtype=jnp.float32) / half)
    ang = jnp.arange(seq, dtype=jnp.int32).astype(jnp.float32)[:, None] * freqs[None, :]
    cos, sin = jnp.cos(ang), jnp.sin(ang)
    ones = jnp.ones((seq, _MLA_NOPE), _F32)
    z = lambda w: jnp.zeros((seq, w), _F32)
    tail = _LANES - _MLA_QK
    c = jnp.concatenate([ones, cos, cos, z(tail)], axis=1)
    s1 = jnp.concatenate([z(_MLA_NOPE + half), sin, z(tail)], axis=1)
    s2 = jnp.concatenate([z(_MLA_NOPE), -sin, z(half + tail)], axis=1)
    return c, s1, s2


def _group_ones(group):
    idx = np.arange(_MXU) // group
    return jnp.asarray(idx[:, None] == idx[None, :], dtype=_BF16)


def _layer_params(l, p, rope, g64):
    d = p["w_in"].shape[1]
    o = _IN_OFFS
    w_in = p["w_in"][l]
    kr = jnp.zeros((d, _LANES), _F32).at[:, _MLA_NOPE:_MLA_QK].set(w_in[:, o[8]:o[9]])
    w_uq = jnp.pad(p["w_mla_uq"][l].reshape(_MLA_Q_RANK, _MLA_HEADS, _MLA_QK),
                   ((0, 0), (0, 0), (0, _LANES - _MLA_QK))).reshape(_MLA_Q_RANK, -1)
    w_ukv = p["w_mla_ukv"][l].reshape(_MLA_KV_RANK, _MLA_HEADS, _MLA_NOPE + _MLA_V)
    w_k = jnp.pad(w_ukv[:, :, :_MLA_NOPE], ((0, 0), (0, 0), (0, _LANES - _MLA_NOPE)))
    w_v = w_ukv[:, :, _MLA_NOPE:]
    pad_g = lambda g: jnp.tile(jnp.pad(g, (0, _LANES - _MLA_QK)), _MLA_HEADS)[None]
    col = lambda g: jnp.broadcast_to(g[:, None], (g.shape[0], _ATT_TILE))
    lam_init = 0.8 - 0.6 * math.exp(-0.3 * l)
    lp = p["da_lambda"][l].astype(_F32)
    lam = jnp.exp(jnp.sum(lp[0] * lp[1])) - jnp.exp(jnp.sum(lp[2] * lp[3])) + lam_init
    return {
        "mix_g": p["mix_norm_g"][l][None],
        "w_in": jnp.concatenate([w_in[:, :o[8]], kr], axis=1).astype(_BF16),
        "g64": g64,
        "da_qg": jnp.tile(p["da_q_norm_g"][l], 2 * _DA_HEADS)[None] * (_DA_DIM ** -0.5 * _LOG2E),
        "da_kg": jnp.tile(p["da_k_norm_g"][l], 2 * _DA_HEADS)[None],
        "cq_g": p["mla_cq_norm_g"][l][None],
        "ckv_g": p["mla_ckv_norm_g"][l][None],
        "w_uq": w_uq.astype(_BF16),
        "w_ukv": jnp.concatenate([w_k.reshape(_MLA_KV_RANK, -1), w_v.reshape(_MLA_KV_RANK, -1)],
                                 axis=1).astype(_BF16),
        "mla_qg": pad_g(p["mla_q_norm_g"][l]) * (_MLA_QK ** -0.5 * _LOG2E),
        "mla_kg": pad_g(p["mla_k_norm_g"][l]),
        "rope_cos": rope[0], "rope_s1": rope[1], "rope_s2": rope[2],
        "lam": jnp.reshape(lam, (1,)).astype(_F32),
        "da_og": col(p["da_subln_g"][l] * (1.0 - lam_init)),
        "sb_og": col(jnp.tile(p["sb_out_g"][l], 2)),
        "mla_og": col(jnp.tile(p["mla_out_g"][l], 2)),
        "w_out": p["w_out"][l].astype(_BF16),
        "memx_g": p["memx_norm_g"][l][None],
        "w_mem_q": p["w_mem_q"][l].astype(_BF16),
        "mem_qg": jnp.tile(p["mem_q_norm_g"][l], _MEM_HEADS)[None] * (_MEM_DIM ** -0.5),
        "w_mem_o": p["w_mem_o"][l].astype(_BF16),
        "ffn_g": p["ffn_norm_g"][l][None],
        "w_ff1": p["w_ff1"][l].astype(_BF16),
        "w_ff2": p["w_ff2"][l].astype(_BF16),
    }


def kernel(x, mem, rel_bias, mix_norm_g, w_in, da_q_norm_g, da_k_norm_g, da_lambda, da_subln_g,
           sb_out_g, mla_cq_norm_g, mla_ckv_norm_g, w_mla_uq, w_mla_ukv, mla_q_norm_g, mla_k_norm_g,
           mla_out_g, w_out, memx_norm_g, mem_norm_g, w_mem_q, w_mem_kv, mem_q_norm_g, mem_k_norm_g,
           w_mem_o, ffn_norm_g, w_ff1, w_ff2):
    p = dict(mix_norm_g=mix_norm_g, w_in=w_in, da_q_norm_g=da_q_norm_g, da_k_norm_g=da_k_norm_g,
             da_lambda=da_lambda, da_subln_g=da_subln_g, sb_out_g=sb_out_g,
             mla_cq_norm_g=mla_cq_norm_g, mla_ckv_norm_g=mla_ckv_norm_g, w_mla_uq=w_mla_uq,
             w_mla_ukv=w_mla_ukv, mla_q_norm_g=mla_q_norm_g, mla_k_norm_g=mla_k_norm_g,
             mla_out_g=mla_out_g, w_out=w_out, memx_norm_g=memx_norm_g, w_mem_q=w_mem_q,
             mem_q_norm_g=mem_q_norm_g, w_mem_o=w_mem_o, ffn_norm_g=ffn_norm_g, w_ff1=w_ff1,
             w_ff2=w_ff2)
    batch, seq, d = x.shape
    depth = w_in.shape[0]
    n_mem = mem.shape[1]
    t = _ATT_TILE
    assert seq % _ROW_TILE == 0 and seq % _MIX_ROWS == 0 and seq % t == 0
    assert (batch * n_mem) % _ROW_TILE == 0
    assert batch % _PAIR_BATCH == 0
    assert w_in.shape[2] == _IN_OFFS[-1]

    g64 = _group_ones(_MEM_DIM)
    rope = _rope_tables(seq)
    bias = _da_bias_tables(rel_bias, t)
    i = jnp.arange(t, dtype=jnp.int32)
    chunk_mask = jnp.where((i[:, None] // _CHUNK) <= (i[None, :] // _CHUNK), 0.0, -jnp.inf).astype(_F32)
    chunk_mask = jnp.stack([chunk_mask, jnp.zeros_like(chunk_mask)])

    width = _MEM_HEADS * _MEM_DIM
    km, vm = _memkv(mem.reshape(batch * n_mem, d), mem_norm_g[:, None, :], w_mem_kv.astype(_BF16), g64,
                    jnp.tile(mem_k_norm_g, (1, _MEM_HEADS))[:, None, :])
    km = km.reshape(depth, batch, n_mem, width)
    vm = vm.reshape(depth, batch, n_mem, width)

    x2d = x.reshape(batch * seq, d)
    for l in range(depth):
        lp = _layer_params(l, p, rope, g64)
        daq, dak, dav, sbq, sbk, sbv, mq, mk, mv = _inproj(x2d, lp, seq)
        ya = _da_attention(daq, dak, dav, bias, lp["lam"], lp["da_og"], batch, seq)
        yb = _sb_attention(sbq, sbk, sbv, lp["sb_og"], batch, seq)
        yc = _mla_attention(mq, mk, mv, chunk_mask, lp["mla_og"], batch, seq)
        x2d = _mix_mem(x2d, ya, yb, yc, lp, km[l], vm[l], seq)
        x2d = _ffn(x2d, lp["ffn_g"], lp["w_ff1"], lp["w_ff2"])
    return x2d.reshape(batch, seq, d)
```

```python
import functools
import math

import numpy as np
import jax
import jax.numpy as jnp
from jax import lax
from jax.experimental import pallas as pl
from jax.experimental.pallas import tpu as pltpu

_F32 = jnp.float32
_BF16 = jnp.bfloat16
_EPS = 1e-6

_CHUNK = 64
_DA_HEADS, _DA_DIM = 4, 64
_DA_VDIM = 2 * _DA_DIM
_SB_HEADS, _SB_DIM = 4, 64
_MLA_HEADS, _MLA_NOPE, _MLA_ROPE, _MLA_V = 4, 64, 32, 64
_MLA_QK = _MLA_NOPE + _MLA_ROPE
_MLA_Q_RANK, _MLA_KV_RANK = 256, 128
_ROPE_THETA = 10000.0
_NUM_BUCKETS, _MAX_DISTANCE = 32, 128
_MEM_HEADS, _MEM_DIM = 4, 64

_LANES = 128
_MXU = 256
_VMEM_BYTES = 64 * 1024 * 1024
_VMEM_LIMIT = _VMEM_BYTES - 12 * 1024 * 1024

_ATT_TILE = 256
_ROW_TILE = 512
_MIX_ROWS = 1024

_DA_WIDTH = _DA_HEADS * _DA_VDIM
_SB_WIDTH = _SB_HEADS * _SB_DIM
_MLA_QK_WIDTH = _MLA_HEADS * _LANES
_MLA_V_WIDTH = _MLA_HEADS * _MLA_V
_IN_SIZES = (_DA_WIDTH, _DA_WIDTH, _DA_WIDTH, _SB_WIDTH, _SB_WIDTH, _SB_WIDTH,
             _MLA_Q_RANK, _MLA_KV_RANK, _MLA_ROPE)
_IN_OFFS = tuple(int(v) for v in np.cumsum((0,) + _IN_SIZES))
_NT = (((1,), (1,)), ((), ()))
_LOG2E = math.log2(math.e)


def _const_spec(shape):
    zeros = (0,) * len(shape)
    return pl.BlockSpec(shape, lambda *_: zeros, pipeline_mode=pl.Buffered(1))


def _params(*sem):
    return pltpu.CompilerParams(dimension_semantics=sem, vmem_limit_bytes=_VMEM_LIMIT)


def _rms(x, g):
    return x * lax.rsqrt(jnp.mean(x * x, axis=-1, keepdims=True) + _EPS) * g


def _group_mean_sq(y, gmat, group):
    sq = (y * y).astype(_BF16)
    cols = y.shape[1]
    parts = [jnp.dot(sq[:, c:c + _MXU], gmat, preferred_element_type=_F32)
             for c in range(0, cols, _MXU)]
    ss = parts[0] if len(parts) == 1 else jnp.concatenate(parts, axis=1)
    return ss * (1.0 / group)


def _inproj_kernel(x_ref, gmix_ref, w_ref, g64_ref, gq_ref, gk_ref, cqg_ref, ckvg_ref,
                   wuq_ref, wukv_ref, qg_ref, kg_ref, cos_ref, s1_ref, s2_ref,
                   daq_ref, dak_ref, dav_ref, sbq_ref, sbk_ref, sbv_ref,
                   mq_ref, mk_ref, mv_ref):
    x = x_ref[...]
    h = _rms(x, gmix_ref[...]).astype(_BF16)
    o = _IN_OFFS

    def proj(seg, width=None):
        hi = o[seg + 1] if width is None else o[seg] + width
        return jnp.dot(h, w_ref[:, o[seg]:hi], preferred_element_type=_F32)

    g64 = g64_ref[...]
    t = daq_ref.shape[2]

    def norm64(y, g):
        return y * lax.rsqrt(_group_mean_sq(y, g64, _DA_DIM) + _EPS) * g

    def store_t(ref, y, row0=0):
        for r in range(y.shape[0] // t):
            ref[r, row0:row0 + y.shape[1], :] = y[r * t:(r + 1) * t, :].T.astype(_BF16)

    cos, s1, s2 = cos_ref[...], s1_ref[...], s2_ref[...]

    def head_norm_rope(y, g):
        ms = jnp.sum(y * y, axis=-1, keepdims=True) * (1.0 / _MLA_QK)
        yn = y * lax.rsqrt(ms + _EPS) * g
        half = _MLA_ROPE // 2
        return (yn * cos + pltpu.roll(yn, half, 1) * s1
                + pltpu.roll(yn, _LANES - half, 1) * s2)

    cq = _rms(proj(6), cqg_ref[...]).astype(_BF16)
    q_all = jnp.dot(cq, wuq_ref[...], preferred_element_type=_F32)
    ckv = _rms(proj(7), ckvg_ref[...]).astype(_BF16)
    kv_all = jnp.dot(ckv, wukv_ref[...], preferred_element_type=_F32)
    k_rope = proj(8, _LANES)

    def latent_head(hd):
        sl = slice(hd * _LANES, (hd + 1) * _LANES)
        store_t(mq_ref, head_norm_rope(q_all[:, sl], qg_ref[:, sl]), hd * _LANES)
        mk_ref[:, sl] = head_norm_rope(kv_all[:, sl] + k_rope, kg_ref[:, sl]).astype(_BF16)

    store_t(daq_ref, norm64(proj(0), gq_ref[...]))
    latent_head(0)
    dak_ref[...] = norm64(proj(1), gk_ref[...]).astype(_BF16)
    latent_head(1)
    store_t(dav_ref, proj(2))
    latent_head(2)
    store_t(sbq_ref, proj(3) * (_SB_DIM ** -0.5 * _LOG2E))
    latent_head(3)
    sbk_ref[...] = proj(4).astype(_BF16)
    store_t(mv_ref, kv_all[:, _MLA_HEADS * _LANES:])
    store_t(sbv_ref, proj(5))


def _inproj(x2d, lp, seq):
    m, d = x2d.shape
    tm = _ROW_TILE
    t = _ATT_TILE
    pos_blocks = seq // tm
    tab = pl.BlockSpec((tm, _LANES), lambda i: (i % pos_blocks, 0))
    widths = (_DA_WIDTH, _DA_WIDTH, _DA_WIDTH, _SB_WIDTH, _SB_WIDTH, _SB_WIDTH,
              _MLA_QK_WIDTH, _MLA_QK_WIDTH, _MLA_V_WIDTH)
    transposed = (True, False, True, True, False, True, True, False, True)
    out_specs, out_shape = [], []
    for c, tr in zip(widths, transposed):
        if tr:
            out_specs.append(pl.BlockSpec((tm // t, c, t), lambda i: (i, 0, 0)))
            out_shape.append(jax.ShapeDtypeStruct((m // t, c, t), _BF16))
        else:
            out_specs.append(pl.BlockSpec((tm, c), lambda i: (i, 0)))
            out_shape.append(jax.ShapeDtypeStruct((m, c), _BF16))
    row = lambda c: pl.BlockSpec((tm, c), lambda i: (i, 0))
    return pl.pallas_call(
        _inproj_kernel,
        grid=(m // tm,),
        in_specs=[row(d), _const_spec((1, d)), _const_spec(lp["w_in"].shape),
                  _const_spec((_MXU, _MXU)), _const_spec((1, _DA_WIDTH)), _const_spec((1, _DA_WIDTH)),
                  _const_spec((1, _MLA_Q_RANK)), _const_spec((1, _MLA_KV_RANK)),
                  _const_spec(lp["w_uq"].shape), _const_spec(lp["w_ukv"].shape),
                  _const_spec((1, _MLA_QK_WIDTH)), _const_spec((1, _MLA_QK_WIDTH)), tab, tab, tab],
        out_specs=out_specs,
        out_shape=out_shape,
        compiler_params=_params("parallel"),
        name="inproj",
    )(x2d, lp["mix_g"], lp["w_in"], lp["g64"], lp["da_qg"], lp["da_kg"], lp["cq_g"],
      lp["ckv_g"], lp["w_uq"], lp["w_ukv"], lp["mla_qg"], lp["mla_kg"],
      lp["rope_cos"], lp["rope_s1"], lp["rope_s2"])


_SUM_ROWS = 16


def _pipelined_tiles(n_tiles, n_chains, stage, consume, run, s_even, s_odd, group):
    groups = [range(g, min(g + group, n_chains)) for g in range(0, n_chains, group)]

    def overlap(j_stage, buf_stage, j_done, buf_done, summary, run):
        new_summary, new_run = [], []
        new_summary += stage(j_stage, buf_stage, groups[0], False)
        for g, members in enumerate(groups):
            if g + 1 < len(groups):
                new_summary += stage(j_stage, buf_stage, groups[g + 1], False)
            new_run += consume(j_done, buf_done, members,
                               [summary[c] for c in members], [run[c] for c in members])
        return tuple(new_run), tuple(new_summary)

    def finish(j, buf, summary, run):
        return tuple(consume(j, buf, range(n_chains), list(summary), list(run)))

    def pair(i, carry):
        run, summary = carry
        run, summary = overlap(2 * i + 1, s_odd, 2 * i, s_even, summary, run)
        return overlap(2 * i + 2, s_even, 2 * i + 1, s_odd, summary, run)

    def tail_two(_, carry):
        run, summary = carry
        run, summary = overlap(n_tiles - 1, s_odd, n_tiles - 2, s_even, summary, run)
        return finish(n_tiles - 1, s_odd, summary, run), summary

    def tail_one(_, carry):
        run, summary = carry
        return finish(n_tiles - 1, s_even, summary, run), summary

    carry = (run, tuple(stage(0, s_even, range(n_chains), True)))
    last = n_tiles - 1
    odd = jnp.bitwise_and(last, 1)
    carry = lax.fori_loop(0, jnp.right_shift(last, 1), pair, carry)
    carry = lax.fori_loop(0, odd, tail_two, carry)
    lax.fori_loop(0, 1 - odd, tail_one, carry)


def _online_softmax(n_tiles, score, value, s_even, s_odd, acc_scr):
    n_chains, _, t = s_even.shape
    ones = jnp.ones((_SUM_ROWS, t), _BF16)

    def stage(ki, buf, members, first):
        col_max = []
        for c in members:
            s = score(c, ki)
            buf[c] = s
            col_max.append(jnp.max(s, axis=0, keepdims=True))
        return col_max

    def consume(ki, buf, members, col_max, m_run):
        out = []
        for c, cm, m_old in zip(members, col_max, m_run):
            m_new = jnp.maximum(m_old, cm)
            alpha = jnp.exp2(m_old - m_new)
            p = jnp.exp2(buf[c] - m_new).astype(_BF16)
            v_ones = jnp.concatenate([value(c, ki), ones], axis=0)
            acc_scr[c] = alpha * acc_scr[c] + jnp.dot(v_ones, p, preferred_element_type=_F32)
            out.append(m_new)
        return out

    for c in range(n_chains):
        acc_scr[c] = jnp.zeros(acc_scr.shape[1:], _F32)
    m_init = tuple(jnp.full((1, t), -jnp.inf, _F32) for _ in range(n_chains))
    _pipelined_tiles(n_tiles, n_chains, stage, consume, m_init, s_even, s_odd, group=1)


def _softmax_finish(acc, rows):
    width = acc.shape[0] - _SUM_ROWS
    return acc[rows] * (1.0 / acc[width:width + 1])


def _rows(i, t, base=0):
    return pl.ds(pl.multiple_of(base + i * t, t), t)


def _split_rows(x_t, half):
    zero = jnp.zeros((half, x_t.shape[1]), x_t.dtype)
    return (jnp.concatenate([x_t[:half], zero], axis=0),
            jnp.concatenate([zero, x_t[half:]], axis=0))


def _half_row_norm(o_t, half, g_t):
    sq = o_t * o_t
    lo = lax.rsqrt(jnp.mean(sq[:half], axis=0, keepdims=True) + _EPS)
    hi = lax.rsqrt(jnp.mean(sq[half:], axis=0, keepdims=True) + _EPS)
    return jnp.concatenate([o_t[:half] * lo, o_t[half:] * hi], axis=0) * g_t


def _da_kernel(lam_ref, q_ref, k_ref, v_ref, bias_ref, g_ref, o_ref, s_even, s_odd, acc_scr, *, t):
    nq = q_ref.shape[0]
    lam = lam_ref[0]

    heads = range(_DA_HEADS)
    head = lambda hd: slice(hd * _LANES, (hd + 1) * _LANES)
    all_rows = slice(0, _DA_VDIM)

    def q_tile(qi, carry):
        qm = [qh for hd in heads for qh in _split_rows(q_ref[qi, head(hd), :], _DA_DIM)]

        def score(c, ki):
            hd, mi = divmod(c, 2)
            d = jnp.minimum(qi - ki, 2)
            return (jnp.dot(k_ref[_rows(ki, t), head(hd)], qm[c], preferred_element_type=_F32)
                    + bias_ref[hd, d, mi])

        value = lambda c, ki: v_ref[ki, head(c // 2), :]
        _online_softmax(qi + 1, score, value, s_even, s_odd, acc_scr)
        for hd in heads:
            o_t = (_softmax_finish(acc_scr[2 * hd], all_rows)
                   - lam * _softmax_finish(acc_scr[2 * hd + 1], all_rows))
            o_t = o_t * lax.rsqrt(jnp.mean(o_t * o_t, axis=0, keepdims=True) + _EPS) * g_ref[...]
            o_ref[_rows(qi, t), head(hd)] = o_t.T.astype(o_ref.dtype)
        return carry

    lax.fori_loop(0, nq, q_tile, 0)


def _da_attention(q_t, k, v_t, bias, lam, g_t, batch, seq):
    t = _ATT_TILE
    nq = seq // t
    width = _DA_HEADS * _LANES
    tiles = pl.BlockSpec((nq, width, t), lambda b: (b, 0, 0))
    rows = pl.BlockSpec((seq, width), lambda b: (b, 0))
    return pl.pallas_call(
        functools.partial(_da_kernel, t=t),
        grid=(batch,),
        in_specs=[pl.BlockSpec(memory_space=pltpu.SMEM), tiles, rows, tiles,
                  _const_spec(bias.shape), _const_spec((_LANES, t))],
        out_specs=rows,
        out_shape=jax.ShapeDtypeStruct(k.shape, _BF16),
        scratch_shapes=[pltpu.VMEM((2 * _DA_HEADS, t, t), _F32), pltpu.VMEM((2 * _DA_HEADS, t, t), _F32),
                        pltpu.VMEM((2 * _DA_HEADS, _DA_VDIM + _SUM_ROWS, t), _F32)],
        compiler_params=_params("parallel"),
        name="diff_attention",
    )(lam, q_t, k, v_t, bias, g_t)


_PAIR_BATCH = 2


def _mla_kernel(q_ref, k_ref, v_ref, mask_ref, g_ref, o_ref, s_even, s_odd, acc_scr, *, t, nb):
    nq = q_ref.shape[0] // nb
    seq = k_ref.shape[0] // nb

    head = lambda hd: slice(hd * _LANES, (hd + 1) * _LANES)
    pair = lambda hd: slice((hd // 2) * _LANES, (hd // 2 + 1) * _LANES)
    chains = [(bb, hd) for bb in range(nb) for hd in range(_MLA_HEADS)]

    def q_tile(qi, carry):
        qs = [q_ref[bb * nq + qi, head(hd), :] for bb, hd in chains]

        def score(c, ki):
            bb, hd = chains[c]
            d = jnp.minimum(qi - ki, 1)
            return (jnp.dot(k_ref[_rows(ki, t, bb * seq), head(hd)], qs[c], preferred_element_type=_F32)
                    + mask_ref[d])

        value = lambda c, ki: v_ref[chains[c][0] * nq + ki, pair(chains[c][1]), :]
        _online_softmax(qi + 1, score, value, s_even, s_odd, acc_scr)
        for bb in range(nb):
            for p in range(_MLA_HEADS // 2):
                c = bb * _MLA_HEADS + 2 * p
                o_t = jnp.concatenate([_softmax_finish(acc_scr[c], slice(0, _MLA_V)),
                                       _softmax_finish(acc_scr[c + 1], slice(_MLA_V, _LANES))], axis=0)
                o_ref[_rows(qi, t, bb * seq), head(p)] = (
                    _half_row_norm(o_t, _MLA_V, g_ref[...]).T.astype(o_ref.dtype))
        return carry

    lax.fori_loop(0, nq, q_tile, 0)


def _mla_attention(q_t, k, v_t, mask_t, g_t, batch, seq):
    t = _ATT_TILE
    nb = _PAIR_BATCH
    nq = seq // t
    qk_width = _MLA_HEADS * _LANES
    v_width = _MLA_HEADS * _MLA_V
    chains = nb * _MLA_HEADS
    return pl.pallas_call(
        functools.partial(_mla_kernel, t=t, nb=nb),
        grid=(batch // nb,),
        in_specs=[pl.BlockSpec((nb * nq, qk_width, t), lambda b: (b, 0, 0)),
                  pl.BlockSpec((nb * seq, qk_width), lambda b: (b, 0)),
                  pl.BlockSpec((nb * nq, v_width, t), lambda b: (b, 0, 0)),
                  _const_spec((2, t, t)), _const_spec((_LANES, t))],
        out_specs=pl.BlockSpec((nb * seq, v_width), lambda b: (b, 0)),
        out_shape=jax.ShapeDtypeStruct((batch * seq, v_width), _BF16),
        scratch_shapes=[pltpu.VMEM((chains, t, t), _F32), pltpu.VMEM((chains, t, t), _F32),
                        pltpu.VMEM((chains, _LANES + _SUM_ROWS, t), _F32)],
        compiler_params=_params("parallel"),
        name="latent_attention",
    )(q_t, k, v_t, mask_t, g_t)


def _sb_kernel(q_ref, k_ref, v_ref, g_ref, o_ref, s_even, s_odd, acc_scr, *, t, nb):
    nq = q_ref.shape[0] // nb
    seq = k_ref.shape[0] // nb
    key_idx = lax.broadcasted_iota(jnp.int32, (t, t), 0)
    query_idx = lax.broadcasted_iota(jnp.int32, (t, t), 1)
    earlier = key_idx < query_idx
    tri = jnp.where(earlier, 1.0, 0.0).astype(_BF16)

    pair = lambda hd: slice((hd // 2) * _LANES, (hd // 2 + 1) * _LANES)
    dot = functools.partial(jnp.dot, preferred_element_type=_F32)
    chains = [(bb, hd) for bb in range(nb) for hd in range(_SB_HEADS)]
    n_chains = range(len(chains))

    def q_tile(qi, carry):
        qh = [q for bb in range(nb) for p in range(_SB_HEADS // 2)
              for q in _split_rows(q_ref[bb * nq + qi, pair(2 * p), :], _SB_DIM)]

        def stage(j, buf, members, diag):
            z = [dot(k_ref[_rows(qi - j, t, chains[c][0] * seq), pair(chains[c][1])], qh[c])
                 for c in members]
            log_beta, first_row, later = [], [], []
            for zi in z:
                lp = jnp.log2(1.0 + jnp.exp2(jnp.minimum(zi, -zi)))
                lb = jnp.minimum(zi, 0.0) - lp
                log_1m = lb - zi
                if diag:
                    log_1m = jnp.where(earlier, log_1m, 0.0)
                later.append(dot(tri, log_1m.astype(_BF16)))
                log_beta.append(lb)
                first_row.append(log_1m[0:1, :])
            for c, lb, la in zip(members, log_beta, later):
                log_w = lb + la
                buf[c] = jnp.where(earlier, log_w, -jnp.inf) if diag else log_w
            return [la[0:1, :] + fr for la, fr in zip(later, first_row)]

        def consume(j, buf, members, through, run):
            for c, r in zip(members, run):
                bb, hd = chains[c]
                a = jnp.exp2(buf[c] + r).astype(_BF16)
                acc_scr[c] = acc_scr[c] + dot(v_ref[bb * nq + qi - j, pair(hd), :], a)
            return [r + th for r, th in zip(run, through)]

        for c in n_chains:
            acc_scr[c] = jnp.zeros(acc_scr.shape[1:], _F32)
        run0 = tuple(jnp.zeros((1, t), _F32) for _ in n_chains)
        _pipelined_tiles(qi + 1, len(chains), stage, consume, run0, s_even, s_odd, group=len(chains))
        for bb in range(nb):
            for p in range(_SB_HEADS // 2):
                c = bb * _SB_HEADS + 2 * p
                o_t = jnp.concatenate([acc_scr[c, :_SB_DIM, :], acc_scr[c + 1, _SB_DIM:, :]], axis=0)
                o_ref[_rows(qi, t, bb * seq), pair(2 * p)] = (
                    _half_row_norm(o_t, _SB_DIM, g_ref[...]).T.astype(o_ref.dtype))
        return carry

    lax.fori_loop(0, nq, q_tile, 0)


def _sb_attention(q_t, k, v_t, g_t, batch, seq):
    t = _ATT_TILE
    nb = _PAIR_BATCH
    nq = seq // t
    width = _SB_HEADS * _SB_DIM
    chains = nb * _SB_HEADS
    tiles = pl.BlockSpec((nb * nq, width, t), lambda b: (b, 0, 0))
    rows = pl.BlockSpec((nb * seq, width), lambda b: (b, 0))
    return pl.pallas_call(
        functools.partial(_sb_kernel, t=t, nb=nb),
        grid=(batch // nb,),
        in_specs=[tiles, rows, tiles, _const_spec((_LANES, t))],
        out_specs=rows,
        out_shape=jax.ShapeDtypeStruct(k.shape, _BF16),
        scratch_shapes=[pltpu.VMEM((chains, t, t), _F32), pltpu.VMEM((chains, t, t), _F32),
                        pltpu.VMEM((chains, _LANES, t), _F32)],
        compiler_params=_params("parallel"),
        name="stick_breaking",
    )(q_t, k, v_t, g_t)


def _memkv_kernel(mem_ref, g_ref, w_ref, g64_ref, kg_ref, k_ref, v_ref):
    m = _rms(mem_ref[...], g_ref[...]).astype(_BF16)
    kv = jnp.dot(m, w_ref[...], preferred_element_type=_F32)
    width = _MEM_HEADS * _MEM_DIM
    k = kv[:, :width]
    k = k * lax.rsqrt(_group_mean_sq(k, g64_ref[...], _MEM_DIM) + _EPS) * kg_ref[...]
    k_ref[...] = k.astype(_BF16)
    v_ref[...] = kv[:, width:].astype(_BF16)


def _memkv(mem2d, g, w, g64, kg):
    depth = w.shape[0]
    rows, d = mem2d.shape
    tm = _ROW_TILE
    width = _MEM_HEADS * _MEM_DIM
    out = pl.BlockSpec((None, tm, width), lambda l, i: (l, i, 0))
    return pl.pallas_call(
        _memkv_kernel,
        grid=(depth, rows // tm),
        in_specs=[pl.BlockSpec((tm, d), lambda l, i: (i, 0)),
                  pl.BlockSpec((None, 1, d), lambda l, i: (l, 0, 0)),
                  pl.BlockSpec((None, d, 2 * width), lambda l, i: (l, 0, 0)),
                  _const_spec((_MXU, _MXU)),
                  pl.BlockSpec((None, 1, width), lambda l, i: (l, 0, 0))],
        out_specs=[out, out],
        out_shape=[jax.ShapeDtypeStruct((depth, rows, width), _BF16)] * 2,
        compiler_params=_params("parallel", "parallel"),
        name="memory_kv",
    )(mem2d, g, w, g64, kg)


def _mix_mem_kernel(x_ref, ya_ref, yb_ref, yc_ref, wo_ref, gx_ref, wq_ref, g64_ref, qg_ref,
                    km_ref, vm_ref, wmo_ref, o_ref):
    wa = ya_ref.shape[1]
    wb = wa + yb_ref.shape[1]
    x = (x_ref[...]
         + jnp.dot(ya_ref[...], wo_ref[:wa, :], preferred_element_type=_F32)
         + jnp.dot(yb_ref[...], wo_ref[wa:wb, :], preferred_element_type=_F32)
         + jnp.dot(yc_ref[...], wo_ref[wb:, :], preferred_element_type=_F32))
    h = _rms(x, gx_ref[...]).astype(_BF16)
    q = jnp.dot(h, wq_ref[...], preferred_element_type=_F32)
    q = (q * lax.rsqrt(_group_mean_sq(q, g64_ref[...], _MEM_DIM) + _EPS) * qg_ref[...]).astype(_BF16)
    km = km_ref[...]
    vm = vm_ref[...]
    head_of_lane = lax.broadcasted_iota(jnp.int32, (1, q.shape[1]), 1) // _MEM_DIM
    zero = jnp.zeros_like(q)
    o = jnp.zeros(q.shape, _F32)
    for hd in range(_MEM_HEADS):
        sel = head_of_lane == hd
        s = lax.dot_general(jnp.where(sel, q, zero), km, _NT, preferred_element_type=_F32)
        p = jnp.exp(s - jnp.max(s, axis=-1, keepdims=True))
        oh = jnp.dot(p.astype(_BF16), vm, preferred_element_type=_F32)
        o = jnp.where(sel, oh / jnp.sum(p, axis=-1, keepdims=True), o)
    o_ref[...] = x + jnp.dot(o.astype(_BF16), wmo_ref[...], preferred_element_type=_F32)


def _mix_mem(x2d, ya, yb, yc, lp, km, vm, seq):
    m, d = x2d.shape
    tm = _MIX_ROWS
    per_seq = seq // tm
    n_mem, width = km.shape[1], km.shape[2]
    row = lambda c: pl.BlockSpec((tm, c), lambda i: (i, 0))
    mem = pl.BlockSpec((None, n_mem, width), lambda i: (i // per_seq, 0, 0))
    return pl.pallas_call(
        _mix_mem_kernel,
        grid=(m // tm,),
        in_specs=[row(d), row(ya.shape[1]), row(yb.shape[1]), row(yc.shape[1]),
                  _const_spec(lp["w_out"].shape), _const_spec((1, d)),
                  _const_spec(lp["w_mem_q"].shape), _const_spec((_MXU, _MXU)),
                  _const_spec((1, width)), mem, mem, _const_spec(lp["w_mem_o"].shape)],
        out_specs=row(d),
        out_shape=jax.ShapeDtypeStruct((m, d), _F32),
        compiler_params=_params("parallel"),
        name="mix_and_memory",
    )(x2d, ya, yb, yc, lp["w_out"], lp["memx_g"], lp["w_mem_q"], lp["g64"], lp["mem_qg"],
      km, vm, lp["w_mem_o"])


_FF_CHUNK = 1024


def _ffn_kernel(x_ref, g_ref, w1_ref, w2_ref, o_ref):
    x = x_ref[...]
    h = _rms(x, g_ref[...]).astype(_BF16)
    acc = x
    for c in range(0, w1_ref.shape[1], _FF_CHUNK):
        u = jnp.dot(h, w1_ref[:, c:c + _FF_CHUNK], preferred_element_type=_F32)
        r = jnp.maximum(u, 0.0)
        acc = acc + jnp.dot((r * r).astype(_BF16), w2_ref[c:c + _FF_CHUNK, :],
                            preferred_element_type=_F32)
    o_ref[...] = acc


def _ffn(x2d, g, w1, w2):
    m, d = x2d.shape
    tm = _ROW_TILE
    row = pl.BlockSpec((tm, d), lambda i: (i, 0))
    return pl.pallas_call(
        _ffn_kernel,
        grid=(m // tm,),
        in_specs=[row, _const_spec((1, d)), _const_spec(w1.shape), _const_spec(w2.shape)],
        out_specs=row,
        out_shape=jax.ShapeDtypeStruct((m, d), _F32),
        compiler_params=_params("parallel"),
        name="ffn",
    )(x2d, g, w1, w2)


def _t5_bucket(rel):
    nb = _NUM_BUCKETS // 2
    bucket = (rel > 0).astype(jnp.int32) * nb
    n = jnp.abs(rel)
    max_exact = nb // 2
    is_small = n < max_exact
    large = max_exact + (jnp.log(jnp.maximum(n, 1).astype(jnp.float32) / max_exact)
                         / math.log(_MAX_DISTANCE / max_exact) * (nb - max_exact)).astype(jnp.int32)
    large = jnp.minimum(large, nb - 1)
    return bucket + jnp.where(is_small, n, large)


def _da_bias_tables(rel_bias, t):
    assert t + 1 >= _MAX_DISTANCE and t % _CHUNK == 0
    j = jnp.arange(t, dtype=jnp.int32)[:, None]
    i = jnp.arange(t, dtype=jnp.int32)[None, :]
    rb = rel_bias.astype(_F32) * _LOG2E

    def lookup(bucket):
        hit = bucket[:, :, None, None] == jnp.arange(_NUM_BUCKETS, dtype=jnp.int32)[:, None]
        return jnp.sum(jnp.where(hit, rb[None, None], 0.0), axis=2)

    b0 = lookup(_t5_bucket(j - i))
    b0 = jnp.where(((j // _CHUNK) <= (i // _CHUNK))[:, :, None], b0, -jnp.inf)
    b1 = lookup(_t5_bucket(j - i - t))
    far = jnp.broadcast_to(lookup(_t5_bucket(jnp.full((1, 1), -(t + 1), jnp.int32))), b1.shape)
    tab = jnp.stack([b0, b1, far]).reshape(3, t, t, _DA_HEADS, 2)
    return tab.transpose(3, 0, 4, 1, 2)


def _rope_tables(seq):
    half = _MLA_ROPE // 2
    freqs = _ROPE_THETA ** (-jnp.arange(half, dtype=jnp.float32) / half)
    ang = jnp.arange(seq, dtype=jnp.int32).astype(jnp.float32)[:, None] * freqs[None, :]
    cos, sin = jnp.cos(ang), jnp.sin(ang)
    ones = jnp.ones((seq, _MLA_NOPE), _F32)
    z = lambda w: jnp.zeros((seq, w), _F32)
    tail = _LANES - _MLA_QK
    c = jnp.concatenate([ones, cos, cos, z(tail)], axis=1)
    s1 = jnp.concatenate([z(_MLA_NOPE + half), sin, z(tail)], axis=1)
    s2 = jnp.concatenate([z(_MLA_NOPE), -sin, z(half + tail)], axis=1)
    return c, s1, s2


def _group_ones(group):
    idx = np.arange(_MXU) // group
    return jnp.asarray(idx[:, None] == idx[None, :], dtype=_BF16)


def _layer_params(l, p, rope, g64):
    d = p["w_in"].shape[1]
    o = _IN_OFFS
    w_in = p["w_in"][l]
    kr = jnp.zeros((d, _LANES), _F32).at[:, _MLA_NOPE:_MLA_QK].set(w_in[:, o[8]:o[9]])
    w_uq = jnp.pad(p["w_mla_uq"][l].reshape(_MLA_Q_RANK, _MLA_HEADS, _MLA_QK),
                   ((0, 0), (0, 0), (0, _LANES - _MLA_QK))).reshape(_MLA_Q_RANK, -1)
    w_ukv = p["w_mla_ukv"][l].reshape(_MLA_KV_RANK, _MLA_HEADS, _MLA_NOPE + _MLA_V)
    w_k = jnp.pad(w_ukv[:, :, :_MLA_NOPE], ((0, 0), (0, 0), (0, _LANES - _MLA_NOPE)))
    w_v = w_ukv[:, :, _MLA_NOPE:]
    pad_g = lambda g: jnp.tile(jnp.pad(g, (0, _LANES - _MLA_QK)), _MLA_HEADS)[None]
    col = lambda g: jnp.broadcast_to(g[:, None], (g.shape[0], _ATT_TILE))
    lam_init = 0.8 - 0.6 * math.exp(-0.3 * l)
    lp = p["da_lambda"][l].astype(_F32)
    lam = jnp.exp(jnp.sum(lp[0] * lp[1])) - jnp.exp(jnp.sum(lp[2] * lp[3])) + lam_init
    return {
        "mix_g": p["mix_norm_g"][l][None],
        "w_in": jnp.concatenate([w_in[:, :o[8]], kr], axis=1).astype(_BF16),
        "g64": g64,
        "da_qg": jnp.tile(p["da_q_norm_g"][l], 2 * _DA_HEADS)[None] * (_DA_DIM ** -0.5 * _LOG2E),
        "da_kg": jnp.tile(p["da_k_norm_g"][l], 2 * _DA_HEADS)[None],
        "cq_g": p["mla_cq_norm_g"][l][None],
        "ckv_g": p["mla_ckv_norm_g"][l][None],
        "w_uq": w_uq.astype(_BF16),
        "w_ukv": jnp.concatenate([w_k.reshape(_MLA_KV_RANK, -1), w_v.reshape(_MLA_KV_RANK, -1)],
                                 axis=1).astype(_BF16),
        "mla_qg": pad_g(p["mla_q_norm_g"][l]) * (_MLA_QK ** -0.5 * _LOG2E),
        "mla_kg": pad_g(p["mla_k_norm_g"][l]),
        "rope_cos": rope[0], "rope_s1": rope[1], "rope_s2": rope[2],
        "lam": jnp.reshape(lam, (1,)).astype(_F32),
        "da_og": col(p["da_subln_g"][l] * (1.0 - lam_init)),
        "sb_og": col(jnp.tile(p["sb_out_g"][l], 2)),
        "mla_og": col(jnp.tile(p["mla_out_g"][l], 2)),
        "w_out": p["w_out"][l].astype(_BF16),
        "memx_g": p["memx_norm_g"][l][None],
        "w_mem_q": p["w_mem_q"][l].astype(_BF16),
        "mem_qg": jnp.tile(p["mem_q_norm_g"][l], _MEM_HEADS)[None] * (_MEM_DIM ** -0.5),
        "w_mem_o": p["w_mem_o"][l].astype(_BF16),
        "ffn_g": p["ffn_norm_g"][l][None],
        "w_ff1": p["w_ff1"][l].astype(_BF16),
        "w_ff2": p["w_ff2"][l].astype(_BF16),
    }


def kernel(x, mem, rel_bias, mix_norm_g, w_in, da_q_norm_g, da_k_norm_g, da_lambda, da_subln_g,
           sb_out_g, mla_cq_norm_g, mla_ckv_norm_g, w_mla_uq, w_mla_ukv, mla_q_norm_g, mla_k_norm_g,
           mla_out_g, w_out, memx_norm_g, mem_norm_g, w_mem_q, w_mem_kv, mem_q_norm_g, mem_k_norm_g,
           w_mem_o, ffn_norm_g, w_ff1, w_ff2):
    p = dict(mix_norm_g=mix_norm_g, w_in=w_in, da_q_norm_g=da_q_norm_g, da_k_norm_g=da_k_norm_g,
             da_lambda=da_lambda, da_subln_g=da_subln_g, sb_out_g=sb_out_g,
             mla_cq_norm_g=mla_cq_norm_g, mla_ckv_norm_g=mla_ckv_norm_g, w_mla_uq=w_mla_uq,
             w_mla_ukv=w_mla_ukv, mla_q_norm_g=mla_q_norm_g, mla_k_norm_g=mla_k_norm_g,
             mla_out_g=mla_out_g, w_out=w_out, memx_norm_g=memx_norm_g, w_mem_q=w_mem_q,
             mem_q_norm_g=mem_q_norm_g, w_mem_o=w_mem_o, ffn_norm_g=ffn_norm_g, w_ff1=w_ff1,
             w_ff2=w_ff2)
    batch, seq, d = x.shape
    depth = w_in.shape[0]
    n_mem = mem.shape[1]
    t = _ATT_TILE
    assert seq % _ROW_TILE == 0 and seq % _MIX_ROWS == 0 and seq % t == 0
    assert (batch * n_mem) % _ROW_TILE == 0
    assert batch % _PAIR_BATCH == 0
    assert w_in.shape[2] == _IN_OFFS[-1]

    g64 = _group_ones(_MEM_DIM)
    rope = _rope_tables(seq)
    bias = _da_bias_tables(rel_bias, t)
    i = jnp.arange(t, dtype=jnp.int32)
    chunk_mask = jnp.where((i[:, None] // _CHUNK) <= (i[None, :] // _CHUNK), 0.0, -jnp.inf).astype(_F32)
    chunk_mask = jnp.stack([chunk_mask, jnp.zeros_like(chunk_mask)])

    width = _MEM_HEADS * _MEM_DIM
    km, vm = _memkv(mem.reshape(batch * n_mem, d), mem_norm_g[:, None, :], w_mem_kv.astype(_BF16), g64,
                    jnp.tile(mem_k_norm_g, (1, _MEM_HEADS))[:, None, :])
    km = km.reshape(depth, batch, n_mem, width)
    vm = vm.reshape(depth, batch, n_mem, width)

    x2d = x.reshape(batch * seq, d)
    for l in range(depth):
        lp = _layer_params(l, p, rope, g64)
        daq, dak, dav, sbq, sbk, sbv, mq, mk, mv = _inproj(x2d, lp, seq)
        ya = _da_attention(daq, dak, dav, bias, lp["lam"], lp["da_og"], batch, seq)
        yb = _sb_attention(sbq, sbk, sbv, lp["sb_og"], batch, seq)
        yc = _mla_attention(mq, mk, mv, chunk_mask, lp["mla_og"], batch, seq)
        x2d = _mix_mem(x2d, ya, yb, yc, lp, km[l], vm[l], seq)
        x2d = _ffn(x2d, lp["ffn_g"], lp["w_ff1"], lp["w_ff2"])
    return x2d.reshape(batch, seq, d)
```

```python
import functools
import math

import numpy as np
import jax
import jax.numpy as jnp
from jax import lax
from jax.experimental import pallas as pl
from jax.experimental.pallas import tpu as pltpu

_F32 = jnp.float32
_BF16 = jnp.bfloat16
_EPS = 1e-6

_CHUNK = 64
_DA_HEADS, _DA_DIM = 4, 64
_DA_VDIM = 2 * _DA_DIM
_SB_HEADS, _SB_DIM = 4, 64
_MLA_HEADS, _MLA_NOPE, _MLA_ROPE, _MLA_V = 4, 64, 32, 64
_MLA_QK = _MLA_NOPE + _MLA_ROPE
_MLA_Q_RANK, _MLA_KV_RANK = 256, 128
_ROPE_THETA = 10000.0
_NUM_BUCKETS, _MAX_DISTANCE = 32, 128
_MEM_HEADS, _MEM_DIM = 4, 64

_LANES = 128
_MXU = 256
_VMEM_BYTES = 64 * 1024 * 1024
_VMEM_LIMIT = _VMEM_BYTES - 12 * 1024 * 1024

_ATT_TILE = 256
_ROW_TILE = 512
_MIX_ROWS = 1024

_DA_WIDTH = _DA_HEADS * _DA_VDIM
_SB_WIDTH = _SB_HEADS * _SB_DIM
_MLA_QK_WIDTH = _MLA_HEADS * _LANES
_MLA_V_WIDTH = _MLA_HEADS * _MLA_V
_IN_SIZES = (_DA_WIDTH, _DA_WIDTH, _DA_WIDTH, _SB_WIDTH, _SB_WIDTH, _SB_WIDTH,
             _MLA_Q_RANK, _MLA_KV_RANK, _MLA_ROPE)
_IN_OFFS = tuple(int(v) for v in np.cumsum((0,) + _IN_SIZES))
_NT = (((1,), (1,)), ((), ()))
_LOG2E = math.log2(math.e)


def _const_spec(shape):
    zeros = (0,) * len(shape)
    return pl.BlockSpec(shape, lambda *_: zeros, pipeline_mode=pl.Buffered(1))


def _params(*sem):
    return pltpu.CompilerParams(dimension_semantics=sem, vmem_limit_bytes=_VMEM_LIMIT)


def _rms(x, g):
    return x * lax.rsqrt(jnp.mean(x * x, axis=-1, keepdims=True) + _EPS) * g


def _group_mean_sq(y, gmat, group):
    sq = (y * y).astype(_BF16)
    cols = y.shape[1]
    parts = [jnp.dot(sq[:, c:c + _MXU], gmat, preferred_element_type=_F32)
             for c in range(0, cols, _MXU)]
    ss = parts[0] if len(parts) == 1 else jnp.concatenate(parts, axis=1)
    return ss * (1.0 / group)


def _inproj_kernel(x_ref, gmix_ref, w_ref, g64_ref, gq_ref, gk_ref, cqg_ref, ckvg_ref,
                   wuq_ref, wukv_ref, qg_ref, kg_ref, cos_ref, s1_ref, s2_ref,
                   daq_ref, dak_ref, dav_ref, sbq_ref, sbk_ref, sbv_ref,
                   mq_ref, mk_ref, mv_ref):
    x = x_ref[...]
    h = _rms(x, gmix_ref[...]).astype(_BF16)
    o = _IN_OFFS

    def proj(seg, width=None):
        hi = o[seg + 1] if width is None else o[seg] + width
        return jnp.dot(h, w_ref[:, o[seg]:hi], preferred_element_type=_F32)

    g64 = g64_ref[...]
    t = daq_ref.shape[2]

    def norm64(y, g):
        return y * lax.rsqrt(_group_mean_sq(y, g64, _DA_DIM) + _EPS) * g

    def store_t(ref, y, row0=0):
        for r in range(y.shape[0] // t):
            ref[r, row0:row0 + y.shape[1], :] = y[r * t:(r + 1) * t, :].T.astype(_BF16)

    cos, s1, s2 = cos_ref[...], s1_ref[...], s2_ref[...]

    def head_norm_rope(y, g):
        ms = jnp.sum(y * y, axis=-1, keepdims=True) * (1.0 / _MLA_QK)
        yn = y * lax.rsqrt(ms + _EPS) * g
        half = _MLA_ROPE // 2
        return (yn * cos + pltpu.roll(yn, half, 1) * s1
                + pltpu.roll(yn, _LANES - half, 1) * s2)

    cq = _rms(proj(6), cqg_ref[...]).astype(_BF16)
    q_all = jnp.dot(cq, wuq_ref[...], preferred_element_type=_F32)
    ckv = _rms(proj(7), ckvg_ref[...]).astype(_BF16)
    kv_all = jnp.dot(ckv, wukv_ref[...], preferred_element_type=_F32)
    k_rope = proj(8, _LANES)

    def latent_head(hd):
        sl = slice(hd * _LANES, (hd + 1) * _LANES)
        store_t(mq_ref, head_norm_rope(q_all[:, sl], qg_ref[:, sl]), hd * _LANES)
        mk_ref[:, sl] = head_norm_rope(kv_all[:, sl] + k_rope, kg_ref[:, sl]).astype(_BF16)

    store_t(daq_ref, norm64(proj(0), gq_ref[...]))
    latent_head(0)
    dak_ref[...] = norm64(proj(1), gk_ref[...]).astype(_BF16)
    latent_head(1)
    store_t(dav_ref, proj(2))
    latent_head(2)
    store_t(sbq_ref, proj(3) * (_SB_DIM ** -0.5 * _LOG2E))
    latent_head(3)
    sbk_ref[...] = proj(4).astype(_BF16)
    store_t(mv_ref, kv_all[:, _MLA_HEADS * _LANES:])
    store_t(sbv_ref, proj(5))


def _inproj(x2d, lp, seq):
    m, d = x2d.shape
    tm = _ROW_TILE
    t = _ATT_TILE
    pos_blocks = seq // tm
    tab = pl.BlockSpec((tm, _LANES), lambda i: (i % pos_blocks, 0))
    widths = (_DA_WIDTH, _DA_WIDTH, _DA_WIDTH, _SB_WIDTH, _SB_WIDTH, _SB_WIDTH,
              _MLA_QK_WIDTH, _MLA_QK_WIDTH, _MLA_V_WIDTH)
    transposed = (True, False, True, True, False, True, True, False, True)
    out_specs, out_shape = [], []
    for c, tr in zip(widths, transposed):
        if tr:
            out_specs.append(pl.BlockSpec((tm // t, c, t), lambda i: (i, 0, 0)))
            out_shape.append(jax.ShapeDtypeStruct((m // t, c, t), _BF16))
        else:
            out_specs.append(pl.BlockSpec((tm, c), lambda i: (i, 0)))
            out_shape.append(jax.ShapeDtypeStruct((m, c), _BF16))
    row = lambda c: pl.BlockSpec((tm, c), lambda i: (i, 0))
    return pl.pallas_call(
        _inproj_kernel,
        grid=(m // tm,),
        in_specs=[row(d), _const_spec((1, d)), _const_spec(lp["w_in"].shape),
                  _const_spec((_MXU, _MXU)), _const_spec((1, _DA_WIDTH)), _const_spec((1, _DA_WIDTH)),
                  _const_spec((1, _MLA_Q_RANK)), _const_spec((1, _MLA_KV_RANK)),
                  _const_spec(lp["w_uq"].shape), _const_spec(lp["w_ukv"].shape),
                  _const_spec((1, _MLA_QK_WIDTH)), _const_spec((1, _MLA_QK_WIDTH)), tab, tab, tab],
        out_specs=out_specs,
        out_shape=out_shape,
        compiler_params=_params("parallel"),
        name="inproj",
    )(x2d, lp["mix_g"], lp["w_in"], lp["g64"], lp["da_qg"], lp["da_kg"], lp["cq_g"],
      lp["ckv_g"], lp["w_uq"], lp["w_ukv"], lp["mla_qg"], lp["mla_kg"],
      lp["rope_cos"], lp["rope_s1"], lp["rope_s2"])


_SUM_ROWS = 16


def _pipelined_tiles(n_tiles, n_chains, stage, consume, run, s_even, s_odd, group):
    groups = [range(g, min(g + group, n_chains)) for g in range(0, n_chains, group)]

    def overlap(j_stage, buf_stage, j_done, buf_done, summary, run):
        new_summary, new_run = [], []
        new_summary += stage(j_stage, buf_stage, groups[0], False)
        for g, members in enumerate(groups):
            if g + 1 < len(groups):
                new_summary += stage(j_stage, buf_stage, groups[g + 1], False)
            new_run += consume(j_done, buf_done, members,
                               [summary[c] for c in members], [run[c] for c in members])
        return tuple(new_run), tuple(new_summary)

    def finish(j, buf, summary, run):
        return tuple(consume(j, buf, range(n_chains), list(summary), list(run)))

    def pair(i, carry):
        run, summary = carry
        run, summary = overlap(2 * i + 1, s_odd, 2 * i, s_even, summary, run)
        return overlap(2 * i + 2, s_even, 2 * i + 1, s_odd, summary, run)

    def tail_two(_, carry):
        run, summary = carry
        run, summary = overlap(n_tiles - 1, s_odd, n_tiles - 2, s_even, summary, run)
        return finish(n_tiles - 1, s_odd, summary, run), summary

    def tail_one(_, carry):
        run, summary = carry
        return finish(n_tiles - 1, s_even, summary, run), summary

    carry = (run, tuple(stage(0, s_even, range(n_chains), True)))
    last = n_tiles - 1
    odd = jnp.bitwise_and(last, 1)
    carry = lax.fori_loop(0, jnp.right_shift(last, 1), pair, carry)
    carry = lax.fori_loop(0, odd, tail_two, carry)
    lax.fori_loop(0, 1 - odd, tail_one, carry)


def _online_softmax(n_tiles, score, value, s_even, s_odd, acc_scr):
    n_chains, _, t = s_even.shape
    ones = jnp.ones((_SUM_ROWS, t), _BF16)

    def stage(ki, buf, members, first):
        col_max = []
        for c in members:
            s = score(c, ki)
            buf[c] = s
            col_max.append(jnp.max(s, axis=0, keepdims=True))
        return col_max

    def consume(ki, buf, members, col_max, m_run):
        out = []
        for c, cm, m_old in zip(members, col_max, m_run):
            m_new = jnp.maximum(m_old, cm)
            alpha = jnp.exp2(m_old - m_new)
            p = jnp.exp2(buf[c] - m_new).astype(_BF16)
            v_ones = jnp.concatenate([value(c, ki), ones], axis=0)
            acc_scr[c] = alpha * acc_scr[c] + jnp.dot(v_ones, p, preferred_element_type=_F32)
            out.append(m_new)
        return out

    for c in range(n_chains):
        acc_scr[c] = jnp.zeros(acc_scr.shape[1:], _F32)
    m_init = tuple(jnp.full((1, t), -jnp.inf, _F32) for _ in range(n_chains))
    _pipelined_tiles(n_tiles, n_chains, stage, consume, m_init, s_even, s_odd, group=1)


def _softmax_finish(acc, rows):
    width = acc.shape[0] - _SUM_ROWS
    return acc[rows] * (1.0 / acc[width:width + 1])


def _rows(i, t, base=0):
    return pl.ds(pl.multiple_of(base + i * t, t), t)


def _split_rows(x_t, half):
    zero = jnp.zeros((half, x_t.shape[1]), x_t.dtype)
    return (jnp.concatenate([x_t[:half], zero], axis=0),
            jnp.concatenate([zero, x_t[half:]], axis=0))


def _half_row_norm(o_t, half, g_t):
    sq = o_t * o_t
    lo = lax.rsqrt(jnp.mean(sq[:half], axis=0, keepdims=True) + _EPS)
    hi = lax.rsqrt(jnp.mean(sq[half:], axis=0, keepdims=True) + _EPS)
    return jnp.concatenate([o_t[:half] * lo, o_t[half:] * hi], axis=0) * g_t


def _da_kernel(lam_ref, q_ref, k_ref, v_ref, bias_ref, g_ref, o_ref, s_even, s_odd, acc_scr, *, t):
    nq = q_ref.shape[0]
    lam = lam_ref[0]

    heads = range(_DA_HEADS)
    head = lambda hd: slice(hd * _LANES, (hd + 1) * _LANES)
    all_rows = slice(0, _DA_VDIM)

    def q_tile(qi, carry):
        qm = [qh for hd in heads for qh in _split_rows(q_ref[qi, head(hd), :], _DA_DIM)]

        def score(c, ki):
            hd, mi = divmod(c, 2)
            d = jnp.minimum(qi - ki, 2)
            return (jnp.dot(k_ref[_rows(ki, t), head(hd)], qm[c], preferred_element_type=_F32)
                    + bias_ref[hd, d, mi])

        value = lambda c, ki: v_ref[ki, head(c // 2), :]
        _online_softmax(qi + 1, score, value, s_even, s_odd, acc_scr)
        for hd in heads:
            o_t = (_softmax_finish(acc_scr[2 * hd], all_rows)
                   - lam * _softmax_finish(acc_scr[2 * hd + 1], all_rows))
            o_t = o_t * lax.rsqrt(jnp.mean(o_t * o_t, axis=0, keepdims=True) + _EPS) * g_ref[...]
            o_ref[_rows(qi, t), head(hd)] = o_t.T.astype(o_ref.dtype)
        return carry

    lax.fori_loop(0, nq, q_tile, 0)


def _da_attention(q_t, k, v_t, bias, lam, g_t, batch, seq):
    t = _ATT_TILE
    nq = seq // t
    width = _DA_HEADS * _LANES
    tiles = pl.BlockSpec((nq, width, t), lambda b: (b, 0, 0))
    rows = pl.BlockSpec((seq, width), lambda b: (b, 0))
    return pl.pallas_call(
        functools.partial(_da_kernel, t=t),
        grid=(batch,),
        in_specs=[pl.BlockSpec(memory_space=pltpu.SMEM), tiles, rows, tiles,
                  _const_spec(bias.shape), _const_spec((_LANES, t))],
        out_specs=rows,
        out_shape=jax.ShapeDtypeStruct(k.shape, _BF16),
        scratch_shapes=[pltpu.VMEM((2 * _DA_HEADS, t, t), _F32), pltpu.VMEM((2 * _DA_HEADS, t, t), _F32),
                        pltpu.VMEM((2 * _DA_HEADS, _DA_VDIM + _SUM_ROWS, t), _F32)],
        compiler_params=_params("parallel"),
        name="diff_attention",
    )(lam, q_t, k, v_t, bias, g_t)


_PAIR_BATCH = 2
_SB_LOGIT_LOOKAHEAD = 2


def _mla_kernel(q_ref, k_ref, v_ref, mask_ref, g_ref, o_ref, s_even, s_odd, acc_scr, *, t, nb):
    nq = q_ref.shape[0] // nb
    seq = k_ref.shape[0] // nb

    head = lambda hd: slice(hd * _LANES, (hd + 1) * _LANES)
    pair = lambda hd: slice((hd // 2) * _LANES, (hd // 2 + 1) * _LANES)
    chains = [(bb, hd) for bb in range(nb) for hd in range(_MLA_HEADS)]

    def q_tile(qi, carry):
        qs = [q_ref[bb * nq + qi, head(hd), :] for bb, hd in chains]

        def score(c, ki):
            bb, hd = chains[c]
            d = jnp.minimum(qi - ki, 1)
            return (jnp.dot(k_ref[_rows(ki, t, bb * seq), head(hd)], qs[c], preferred_element_type=_F32)
                    + mask_ref[d])

        value = lambda c, ki: v_ref[chains[c][0] * nq + ki, pair(chains[c][1]), :]
        _online_softmax(qi + 1, score, value, s_even, s_odd, acc_scr)
        for bb in range(nb):
            for p in range(_MLA_HEADS // 2):
                c = bb * _MLA_HEADS + 2 * p
                o_t = jnp.concatenate([_softmax_finish(acc_scr[c], slice(0, _MLA_V)),
                                       _softmax_finish(acc_scr[c + 1], slice(_MLA_V, _LANES))], axis=0)
                o_ref[_rows(qi, t, bb * seq), head(p)] = (
                    _half_row_norm(o_t, _MLA_V, g_ref[...]).T.astype(o_ref.dtype))
        return carry

    lax.fori_loop(0, nq, q_tile, 0)


def _mla_attention(q_t, k, v_t, mask_t, g_t, batch, seq):
    t = _ATT_TILE
    nb = _PAIR_BATCH
    nq = seq // t
    qk_width = _MLA_HEADS * _LANES
    v_width = _MLA_HEADS * _MLA_V
    chains = nb * _MLA_HEADS
    return pl.pallas_call(
        functools.partial(_mla_kernel, t=t, nb=nb),
        grid=(batch // nb,),
        in_specs=[pl.BlockSpec((nb * nq, qk_width, t), lambda b: (b, 0, 0)),
                  pl.BlockSpec((nb * seq, qk_width), lambda b: (b, 0)),
                  pl.BlockSpec((nb * nq, v_width, t), lambda b: (b, 0, 0)),
                  _const_spec((2, t, t)), _const_spec((_LANES, t))],
        out_specs=pl.BlockSpec((nb * seq, v_width), lambda b: (b, 0)),
        out_shape=jax.ShapeDtypeStruct((batch * seq, v_width), _BF16),
        scratch_shapes=[pltpu.VMEM((chains, t, t), _F32), pltpu.VMEM((chains, t, t), _F32),
                        pltpu.VMEM((chains, _LANES + _SUM_ROWS, t), _F32)],
        compiler_params=_params("parallel"),
        name="latent_attention",
    )(q_t, k, v_t, mask_t, g_t)


def _sb_kernel(q_ref, k_ref, v_ref, g_ref, o_ref, s_even, s_odd, acc_scr, *, t, nb):
    nq = q_ref.shape[0] // nb
    seq = k_ref.shape[0] // nb
    key_idx = lax.broadcasted_iota(jnp.int32, (t, t), 0)
    query_idx = lax.broadcasted_iota(jnp.int32, (t, t), 1)
    earlier = key_idx < query_idx
    tri = jnp.where(earlier, 1.0, 0.0).astype(_BF16)

    pair = lambda hd: slice((hd // 2) * _LANES, (hd // 2 + 1) * _LANES)
    dot = functools.partial(jnp.dot, preferred_element_type=_F32)
    chains = [(bb, hd) for bb in range(nb) for hd in range(_SB_HEADS)]
    n_chains = range(len(chains))

    def q_tile(qi, carry):
        qh = [q for bb in range(nb) for p in range(_SB_HEADS // 2)
              for q in _split_rows(q_ref[bb * nq + qi, pair(2 * p), :], _SB_DIM)]

        def stage(j, buf, members, diag):
            members = list(members)
            logit = lambda c: dot(k_ref[_rows(qi - j, t, chains[c][0] * seq), pair(chains[c][1])], qh[c])
            z = [logit(c) for c in members[:_SB_LOGIT_LOOKAHEAD]]
            log_beta, first_row, later = [], [], []
            for n in range(len(members)):
                if n + _SB_LOGIT_LOOKAHEAD < len(members):
                    z.append(logit(members[n + _SB_LOGIT_LOOKAHEAD]))
                zi = z[n]
                lp = jnp.log2(1.0 + jnp.exp2(jnp.minimum(zi, -zi)))
                lb = jnp.minimum(zi, 0.0) - lp
                log_1m = lb - zi
                if diag:
                    log_1m = jnp.where(earlier, log_1m, 0.0)
                later.append(dot(tri, log_1m.astype(_BF16)))
                log_beta.append(lb)
                first_row.append(log_1m[0:1, :])
            for c, lb, la in zip(members, log_beta, later):
                log_w = lb + la
                buf[c] = jnp.where(earlier, log_w, -jnp.inf) if diag else log_w
            return [la[0:1, :] + fr for la, fr in zip(later, first_row)]

        def consume(j, buf, members, through, run):
            for c, r in zip(members, run):
                bb, hd = chains[c]
                a = jnp.exp2(buf[c] + r).astype(_BF16)
                acc_scr[c] = acc_scr[c] + dot(v_ref[bb * nq + qi - j, pair(hd), :], a)
            return [r + th for r, th in zip(run, through)]

        for c in n_chains:
            acc_scr[c] = jnp.zeros(acc_scr.shape[1:], _F32)
        run0 = tuple(jnp.zeros((1, t), _F32) for _ in n_chains)
        _pipelined_tiles(qi + 1, len(chains), stage, consume, run0, s_even, s_odd, group=len(chains))
        for bb in range(nb):
            for p in range(_SB_HEADS // 2):
                c = bb * _SB_HEADS + 2 * p
                o_t = jnp.concatenate([acc_scr[c, :_SB_DIM, :], acc_scr[c + 1, _SB_DIM:, :]], axis=0)
                o_ref[_rows(qi, t, bb * seq), pair(2 * p)] = (
                    _half_row_norm(o_t, _SB_DIM, g_ref[...]).T.astype(o_ref.dtype))
        return carry

    lax.fori_loop(0, nq, q_tile, 0)


def _sb_attention(q_t, k, v_t, g_t, batch, seq):
    t = _ATT_TILE
    nb = _PAIR_BATCH
    nq = seq // t
    width = _SB_HEADS * _SB_DIM
    chains = nb * _SB_HEADS
    tiles = pl.BlockSpec((nb * nq, width, t), lambda b: (b, 0, 0))
    rows = pl.BlockSpec((nb * seq, width), lambda b: (b, 0))
    return pl.pallas_call(
        functools.partial(_sb_kernel, t=t, nb=nb),
        grid=(batch // nb,),
        in_specs=[tiles, rows, tiles, _const_spec((_LANES, t))],
        out_specs=rows,
        out_shape=jax.ShapeDtypeStruct(k.shape, _BF16),
        scratch_shapes=[pltpu.VMEM((chains, t, t), _F32), pltpu.VMEM((chains, t, t), _F32),
                        pltpu.VMEM((chains, _LANES, t), _F32)],
        compiler_params=_params("parallel"),
        name="stick_breaking",
    )(q_t, k, v_t, g_t)


def _memkv_kernel(mem_ref, g_ref, w_ref, g64_ref, kg_ref, k_ref, v_ref):
    m = _rms(mem_ref[...], g_ref[...]).astype(_BF16)
    kv = jnp.dot(m, w_ref[...], preferred_element_type=_F32)
    width = _MEM_HEADS * _MEM_DIM
    k = kv[:, :width]
    k = k * lax.rsqrt(_group_mean_sq(k, g64_ref[...], _MEM_DIM) + _EPS) * kg_ref[...]
    k_ref[...] = k.astype(_BF16)
    v_ref[...] = kv[:, width:].astype(_BF16)


def _memkv(mem2d, g, w, g64, kg):
    depth = w.shape[0]
    rows, d = mem2d.shape
    tm = _ROW_TILE
    width = _MEM_HEADS * _MEM_DIM
    out = pl.BlockSpec((None, tm, width), lambda l, i: (l, i, 0))
    return pl.pallas_call(
        _memkv_kernel,
        grid=(depth, rows // tm),
        in_specs=[pl.BlockSpec((tm, d), lambda l, i: (i, 0)),
                  pl.BlockSpec((None, 1, d), lambda l, i: (l, 0, 0)),
                  pl.BlockSpec((None, d, 2 * width), lambda l, i: (l, 0, 0)),
                  _const_spec((_MXU, _MXU)),
                  pl.BlockSpec((None, 1, width), lambda l, i: (l, 0, 0))],
        out_specs=[out, out],
        out_shape=[jax.ShapeDtypeStruct((depth, rows, width), _BF16)] * 2,
        compiler_params=_params("parallel", "parallel"),
        name="memory_kv",
    )(mem2d, g, w, g64, kg)


def _mix_mem_kernel(x_ref, ya_ref, yb_ref, yc_ref, wo_ref, gx_ref, wq_ref, g64_ref, qg_ref,
                    km_ref, vm_ref, wmo_ref, o_ref):
    wa = ya_ref.shape[1]
    wb = wa + yb_ref.shape[1]
    x = (x_ref[...]
         + jnp.dot(ya_ref[...], wo_ref[:wa, :], preferred_element_type=_F32)
         + jnp.dot(yb_ref[...], wo_ref[wa:wb, :], preferred_element_type=_F32)
         + jnp.dot(yc_ref[...], wo_ref[wb:, :], preferred_element_type=_F32))
    h = _rms(x, gx_ref[...]).astype(_BF16)
    q = jnp.dot(h, wq_ref[...], preferred_element_type=_F32)
    q = (q * lax.rsqrt(_group_mean_sq(q, g64_ref[...], _MEM_DIM) + _EPS) * qg_ref[...]).astype(_BF16)
    km = km_ref[...]
    vm = vm_ref[...]
    head_of_lane = lax.broadcasted_iota(jnp.int32, (1, q.shape[1]), 1) // _MEM_DIM
    zero = jnp.zeros_like(q)
    o = jnp.zeros(q.shape, _F32)
    sels = [head_of_lane == hd for hd in range(_MEM_HEADS)]
    scores = [lax.dot_general(jnp.where(sel, q, zero), km, _NT, preferred_element_type=_F32) for sel in sels]
    for sel, s in zip(sels, scores):
        p = jnp.exp(s - jnp.max(s, axis=-1, keepdims=True))
        oh = jnp.dot(p.astype(_BF16), vm, preferred_element_type=_F32)
        o = jnp.where(sel, oh / jnp.sum(p, axis=-1, keepdims=True), o)
    o_ref[...] = x + jnp.dot(o.astype(_BF16), wmo_ref[...], preferred_element_type=_F32)


def _mix_mem(x2d, ya, yb, yc, lp, km, vm, seq):
    m, d = x2d.shape
    tm = _MIX_ROWS
    per_seq = seq // tm
    n_mem, width = km.shape[1], km.shape[2]
    row = lambda c: pl.BlockSpec((tm, c), lambda i: (i, 0))
    mem = pl.BlockSpec((None, n_mem, width), lambda i: (i // per_seq, 0, 0))
    return pl.pallas_call(
        _mix_mem_kernel,
        grid=(m // tm,),
        in_specs=[row(d), row(ya.shape[1]), row(yb.shape[1]), row(yc.shape[1]),
                  _const_spec(lp["w_out"].shape), _const_spec((1, d)),
                  _const_spec(lp["w_mem_q"].shape), _const_spec((_MXU, _MXU)),
                  _const_spec((1, width)), mem, mem, _const_spec(lp["w_mem_o"].shape)],
        out_specs=row(d),
        out_shape=jax.ShapeDtypeStruct((m, d), _F32),
        compiler_params=_params("parallel"),
        name="mix_and_memory",
    )(x2d, ya, yb, yc, lp["w_out"], lp["memx_g"], lp["w_mem_q"], lp["g64"], lp["mem_qg"],
      km, vm, lp["w_mem_o"])


_FF_CHUNK = 1024


def _ffn_kernel(x_ref, g_ref, w1_ref, w2_ref, o_ref):
    x = x_ref[...]
    h = _rms(x, g_ref[...]).astype(_BF16)
    acc = x
    for c in range(0, w1_ref.shape[1], _FF_CHUNK):
        u = jnp.dot(h, w1_ref[:, c:c + _FF_CHUNK], preferred_element_type=_F32)
        r = jnp.maximum(u, 0.0)
        acc = acc + jnp.dot((r * r).astype(_BF16), w2_ref[c:c + _FF_CHUNK, :],
                            preferred_element_type=_F32)
    o_ref[...] = acc


def _ffn(x2d, g, w1, w2):
    m, d = x2d.shape
    tm = _ROW_TILE
    row = pl.BlockSpec((tm, d), lambda i: (i, 0))
    return pl.pallas_call(
        _ffn_kernel,
        grid=(m // tm,),
        in_specs=[row, _const_spec((1, d)), _const_spec(w1.shape), _const_spec(w2.shape)],
        out_specs=row,
        out_shape=jax.ShapeDtypeStruct((m, d), _F32),
        compiler_params=_params("parallel"),
        name="ffn",
    )(x2d, g, w1, w2)


def _t5_bucket(rel):
    nb = _NUM_BUCKETS // 2
    bucket = (rel > 0).astype(jnp.int32) * nb
    n = jnp.abs(rel)
    max_exact = nb // 2
    is_small = n < max_exact
    large = max_exact + (jnp.log(jnp.maximum(n, 1).astype(jnp.float32) / max_exact)
                         / math.log(_MAX_DISTANCE / max_exact) * (nb - max_exact)).astype(jnp.int32)
    large = jnp.minimum(large, nb - 1)
    return bucket + jnp.where(is_small, n, large)


def _da_bias_tables(rel_bias, t):
    assert t + 1 >= _MAX_DISTANCE and t % _CHUNK == 0
    j = jnp.arange(t, dtype=jnp.int32)[:, None]
    i = jnp.arange(t, dtype=jnp.int32)[None, :]
    rb = rel_bias.astype(_F32) * _LOG2E

    def lookup(bucket):
        hit = bucket[:, :, None, None] == jnp.arange(_NUM_BUCKETS, dtype=jnp.int32)[:, None]
        return jnp.sum(jnp.where(hit, rb[None, None], 0.0), axis=2)

    b0 = lookup(_t5_bucket(j - i))
    b0 = jnp.where(((j // _CHUNK) <= (i // _CHUNK))[:, :, None], b0, -jnp.inf)
    b1 = lookup(_t5_bucket(j - i - t))
    far = jnp.broadcast_to(lookup(_t5_bucket(jnp.full((1, 1), -(t + 1), jnp.int32))), b1.shape)
    tab = jnp.stack([b0, b1, far]).reshape(3, t, t, _DA_HEADS, 2)
    return tab.transpose(3, 0, 4, 1, 2)


def _rope_tables(seq):
    half = _MLA_ROPE // 2
    freqs = _ROPE_THETA ** (-jnp.arange(half, dtype=jnp.float32) / half)
    ang = jnp.arange(seq, dtype=jnp.int32).astype(jnp.float32)[:, None] * freqs[None, :]
    cos, sin = jnp.cos(ang), jnp.sin(ang)
    ones = jnp.ones((seq, _MLA_NOPE), _F32)
    z = lambda w: jnp.zeros((seq, w), _F32)
    tail = _LANES - _MLA_QK
    c = jnp.concatenate([ones, cos, cos, z(tail)], axis=1)
    s1 = jnp.concatenate([z(_MLA_NOPE + half), sin, z(tail)], axis=1)
    s2 = jnp.concatenate([z(_MLA_NOPE), -sin, z(half + tail)], axis=1)
    return c, s1, s2


def _group_ones(group):
    idx = np.arange(_MXU) // group
    return jnp.asarray(idx[:, None] == idx[None, :], dtype=_BF16)


def _layer_params(l, p, rope, g64):
    d = p["w_in"].shape[1]
    o = _IN_OFFS
    w_in = p["w_in"][l]
    kr = jnp.zeros((d, _LANES), _F32).at[:, _MLA_NOPE:_MLA_QK].set(w_in[:, o[8]:o[9]])
    w_uq = jnp.pad(p["w_mla_uq"][l].reshape(_MLA_Q_RANK, _MLA_HEADS, _MLA_QK),
                   ((0, 0), (0, 0), (0, _LANES - _MLA_QK))).reshape(_MLA_Q_RANK, -1)
    w_ukv = p["w_mla_ukv"][l].reshape(_MLA_KV_RANK, _MLA_HEADS, _MLA_NOPE + _MLA_V)
    w_k = jnp.pad(w_ukv[:, :, :_MLA_NOPE], ((0, 0), (0, 0), (0, _LANES - _MLA_NOPE)))
    w_v = w_ukv[:, :, _MLA_NOPE:]
    pad_g = lambda g: jnp.tile(jnp.pad(g, (0, _LANES - _MLA_QK)), _MLA_HEADS)[None]
    col = lambda g: jnp.broadcast_to(g[:, None], (g.shape[0], _ATT_TILE))
    lam_init = 0.8 - 0.6 * math.exp(-0.3 * l)
    lp = p["da_lambda"][l].astype(_F32)
    lam = jnp.exp(jnp.sum(lp[0] * lp[1])) - jnp.exp(jnp.sum(lp[2] * lp[3])) + lam_init
    return {
        "mix_g": p["mix_norm_g"][l][None],
        "w_in": jnp.concatenate([w_in[:, :o[8]], kr], axis=1).astype(_BF16),
        "g64": g64,
        "da_qg": jnp.tile(p["da_q_norm_g"][l], 2 * _DA_HEADS)[None] * (_DA_DIM ** -0.5 * _LOG2E),
        "da_kg": jnp.tile(p["da_k_norm_g"][l], 2 * _DA_HEADS)[None],
        "cq_g": p["mla_cq_norm_g"][l][None],
        "ckv_g": p["mla_ckv_norm_g"][l][None],
        "w_uq": w_uq.astype(_BF16),
        "w_ukv": jnp.concatenate([w_k.reshape(_MLA_KV_RANK, -1), w_v.reshape(_MLA_KV_RANK, -1)],
                                 axis=1).astype(_BF16),
        "mla_qg": pad_g(p["mla_q_norm_g"][l]) * (_MLA_QK ** -0.5 * _LOG2E),
        "mla_kg": pad_g(p["mla_k_norm_g"][l]),
        "rope_cos": rope[0], "rope_s1": rope[1], "rope_s2": rope[2],
        "lam": jnp.reshape(lam, (1,)).astype(_F32),
        "da_og": col(p["da_subln_g"][l] * (1.0 - lam_init)),
        "sb_og": col(jnp.tile(p["sb_out_g"][l], 2)),
        "mla_og": col(jnp.tile(p["mla_out_g"][l], 2)),
        "w_out": p["w_out"][l].astype(_BF16),
        "memx_g": p["memx_norm_g"][l][None],
        "w_mem_q": p["w_mem_q"][l].astype(_BF16),
        "mem_qg": jnp.tile(p["mem_q_norm_g"][l], _MEM_HEADS)[None] * (_MEM_DIM ** -0.5),
        "w_mem_o": p["w_mem_o"][l].astype(_BF16),
        "ffn_g": p["ffn_norm_g"][l][None],
        "w_ff1": p["w_ff1"][l].astype(_BF16),
        "w_ff2": p["w_ff2"][l].astype(_BF16),
    }


def kernel(x, mem, rel_bias, mix_norm_g, w_in, da_q_norm_g, da_k_norm_g, da_lambda, da_subln_g,
           sb_out_g, mla_cq_norm_g, mla_ckv_norm_g, w_mla_uq, w_mla_ukv, mla_q_norm_g, mla_k_norm_g,
           mla_out_g, w_out, memx_norm_g, mem_norm_g, w_mem_q, w_mem_kv, mem_q_norm_g, mem_k_norm_g,
           w_mem_o, ffn_norm_g, w_ff1, w_ff2):
    p = dict(mix_norm_g=mix_norm_g, w_in=w_in, da_q_norm_g=da_q_norm_g, da_k_norm_g=da_k_norm_g,
             da_lambda=da_lambda, da_subln_g=da_subln_g, sb_out_g=sb_out_g,
             mla_cq_norm_g=mla_cq_norm_g, mla_ckv_norm_g=mla_ckv_norm_g, w_mla_uq=w_mla_uq,
             w_mla_ukv=w_mla_ukv, mla_q_norm_g=mla_q_norm_g, mla_k_norm_g=mla_k_norm_g,
             mla_out_g=mla_out_g, w_out=w_out, memx_norm_g=memx_norm_g, w_mem_q=w_mem_q,
             mem_q_norm_g=mem_q_norm_g, w_mem_o=w_mem_o, ffn_norm_g=ffn_norm_g, w_ff1=w_ff1,
             w_ff2=w_ff2)
    batch, seq, d = x.shape
    depth = w_in.shape[0]
    n_mem = mem.shape[1]
    t = _ATT_TILE
    assert seq % _ROW_TILE == 0 and seq % _MIX_ROWS == 0 and seq % t == 0
    assert (batch * n_mem) % _ROW_TILE == 0
    assert batch % _PAIR_BATCH == 0
    assert w_in.shape[2] == _IN_OFFS[-1]

    g64 = _group_ones(_MEM_DIM)
    rope = _rope_tables(seq)
    bias = _da_bias_tables(rel_bias, t)
    i = jnp.arange(t, dtype=jnp.int32)
    chunk_mask = jnp.where((i[:, None] // _CHUNK) <= (i[None, :] // _CHUNK), 0.0, -jnp.inf).astype(_F32)
    chunk_mask = jnp.stack([chunk_mask, jnp.zeros_like(chunk_mask)])

    width = _MEM_HEADS * _MEM_DIM
    km, vm = _memkv(mem.reshape(batch * n_mem, d), mem_norm_g[:, None, :], w_mem_kv.astype(_BF16), g64,
                    jnp.tile(mem_k_norm_g, (1, _MEM_HEADS))[:, None, :])
    km = km.reshape(depth, batch, n_mem, width)
    vm = vm.reshape(depth, batch, n_mem, width)

    x2d = x.reshape(batch * seq, d)
    for l in range(depth):
        lp = _layer_params(l, p, rope, g64)
        daq, dak, dav, sbq, sbk, sbv, mq, mk, mv = _inproj(x2d, lp, seq)
        ya = _da_attention(daq, dak, dav, bias, lp["lam"], lp["da_og"], batch, seq)
        yb = _sb_attention(sbq, sbk, sbv, lp["sb_og"], batch, seq)
        yc = _mla_attention(mq, mk, mv, chunk_mask, lp["mla_og"], batch, seq)
        x2d = _mix_mem(x2d, ya, yb, yc, lp, km[l], vm[l], seq)
        x2d = _ffn(x2d, lp["ffn_g"], lp["w_ff1"], lp["w_ff2"])
    return x2d.reshape(batch, seq, d)
```

```python
import functools
import math

import numpy as np
import jax
import jax.numpy as jnp
from jax import lax
from jax.experimental import pallas as pl
from jax.experimental.pallas import tpu as pltpu

_F32 = jnp.float32
_BF16 = jnp.bfloat16
_EPS = 1e-6

_CHUNK = 64
_DA_HEADS, _DA_DIM = 4, 64
_DA_VDIM = 2 * _DA_DIM
_SB_HEADS, _SB_DIM = 4, 64
_MLA_HEADS, _MLA_NOPE, _MLA_ROPE, _MLA_V = 4, 64, 32, 64
_MLA_QK = _MLA_NOPE + _MLA_ROPE
_MLA_Q_RANK, _MLA_KV_RANK = 256, 128
_ROPE_THETA = 10000.0
_NUM_BUCKETS, _MAX_DISTANCE = 32, 128
_MEM_HEADS, _MEM_DIM = 4, 64

_LANES = 128
_MXU = 256
_VMEM_BYTES = 64 * 1024 * 1024
_VMEM_LIMIT = _VMEM_BYTES - 12 * 1024 * 1024

_ATT_TILE = 256
_ROW_TILE = 512
_MIX_ROWS = 1024

_DA_WIDTH = _DA_HEADS * _DA_VDIM
_SB_WIDTH = _SB_HEADS * _SB_DIM
_MLA_QK_WIDTH = _MLA_HEADS * _LANES
_MLA_V_WIDTH = _MLA_HEADS * _MLA_V
_IN_SIZES = (_DA_WIDTH, _DA_WIDTH, _DA_WIDTH, _SB_WIDTH, _SB_WIDTH, _SB_WIDTH,
             _MLA_Q_RANK, _MLA_KV_RANK, _MLA_ROPE)
_IN_OFFS = tuple(int(v) for v in np.cumsum((0,) + _IN_SIZES))
_NT = (((1,), (1,)), ((), ()))
_LOG2E = math.log2(math.e)


def _const_spec(shape):
    zeros = (0,) * len(shape)
    return pl.BlockSpec(shape, lambda *_: zeros, pipeline_mode=pl.Buffered(1))


def _params(*sem):
    return pltpu.CompilerParams(dimension_semantics=sem, vmem_limit_bytes=_VMEM_LIMIT)


def _rms(x, g):
    return x * lax.rsqrt(jnp.mean(x * x, axis=-1, keepdims=True) + _EPS) * g


def _group_mean_sq(y, gmat, group):
    sq = (y * y).astype(_BF16)
    cols = y.shape[1]
    parts = [jnp.dot(sq[:, c:c + _MXU], gmat, preferred_element_type=_F32)
             for c in range(0, cols, _MXU)]
    ss = parts[0] if len(parts) == 1 else jnp.concatenate(parts, axis=1)
    return ss * (1.0 / group)


def _inproj_kernel(x_ref, gmix_ref, w_ref, g64_ref, gq_ref, gk_ref, cqg_ref, ckvg_ref,
                   wuq_ref, wukv_ref, qg_ref, kg_ref, cos_ref, s1_ref, s2_ref,
                   daq_ref, dak_ref, dav_ref, sbq_ref, sbk_ref, sbv_ref,
                   mq_ref, mk_ref, mv_ref):
    x = x_ref[...]
    h = _rms(x, gmix_ref[...]).astype(_BF16)
    o = _IN_OFFS

    def proj(seg, width=None):
        hi = o[seg + 1] if width is None else o[seg] + width
        return jnp.dot(h, w_ref[:, o[seg]:hi], preferred_element_type=_F32)

    g64 = g64_ref[...]
    t = daq_ref.shape[2]

    def norm64(y, g):
        return y * lax.rsqrt(_group_mean_sq(y, g64, _DA_DIM) + _EPS) * g

    def store_t(ref, y, row0=0):
        for r in range(y.shape[0] // t):
            ref[r, row0:row0 + y.shape[1], :] = y[r * t:(r + 1) * t, :].T.astype(_BF16)

    cos, s1, s2 = cos_ref[...], s1_ref[...], s2_ref[...]

    def head_norm_rope(y, g):
        ms = jnp.sum(y * y, axis=-1, keepdims=True) * (1.0 / _MLA_QK)
        yn = y * lax.rsqrt(ms + _EPS) * g
        half = _MLA_ROPE // 2
        return (yn * cos + pltpu.roll(yn, half, 1) * s1
                + pltpu.roll(yn, _LANES - half, 1) * s2)

    cq = _rms(proj(6), cqg_ref[...]).astype(_BF16)
    q_all = jnp.dot(cq, wuq_ref[...], preferred_element_type=_F32)
    ckv = _rms(proj(7), ckvg_ref[...]).astype(_BF16)
    kv_all = jnp.dot(ckv, wukv_ref[...], preferred_element_type=_F32)
    k_rope = proj(8, _LANES)

    def latent_head(hd):
        sl = slice(hd * _LANES, (hd + 1) * _LANES)
        store_t(mq_ref, head_norm_rope(q_all[:, sl], qg_ref[:, sl]), hd * _LANES)
        mk_ref[:, sl] = head_norm_rope(kv_all[:, sl] + k_rope, kg_ref[:, sl]).astype(_BF16)

    store_t(daq_ref, norm64(proj(0), gq_ref[...]))
    latent_head(0)
    dak_ref[...] = norm64(proj(1), gk_ref[...]).astype(_BF16)
    latent_head(1)
    store_t(dav_ref, proj(2))
    latent_head(2)
    store_t(sbq_ref, proj(3) * (_SB_DIM ** -0.5 * _LOG2E))
    latent_head(3)
    sbk_ref[...] = proj(4).astype(_BF16)
    store_t(mv_ref, kv_all[:, _MLA_HEADS * _LANES:])
    store_t(sbv_ref, proj(5))


def _inproj(x2d, lp, seq):
    m, d = x2d.shape
    tm = _MIX_ROWS
    t = _ATT_TILE
    pos_blocks = seq // tm
    tab = pl.BlockSpec((tm, _LANES), lambda i: (i % pos_blocks, 0))
    widths = (_DA_WIDTH, _DA_WIDTH, _DA_WIDTH, _SB_WIDTH, _SB_WIDTH, _SB_WIDTH,
              _MLA_QK_WIDTH, _MLA_QK_WIDTH, _MLA_V_WIDTH)
    transposed = (True, False, True, True, False, True, True, False, True)
    out_specs, out_shape = [], []
    for c, tr in zip(widths, transposed):
        if tr:
            out_specs.append(pl.BlockSpec((tm // t, c, t), lambda i: (i, 0, 0)))
            out_shape.append(jax.ShapeDtypeStruct((m // t, c, t), _BF16))
        else:
            out_specs.append(pl.BlockSpec((tm, c), lambda i: (i, 0)))
            out_shape.append(jax.ShapeDtypeStruct((m, c), _BF16))
    row = lambda c: pl.BlockSpec((tm, c), lambda i: (i, 0))
    return pl.pallas_call(
        _inproj_kernel,
        grid=(m // tm,),
        in_specs=[row(d), _const_spec((1, d)), _const_spec(lp["w_in"].shape),
                  _const_spec((_MXU, _MXU)), _const_spec((1, _DA_WIDTH)), _const_spec((1, _DA_WIDTH)),
                  _const_spec((1, _MLA_Q_RANK)), _const_spec((1, _MLA_KV_RANK)),
                  _const_spec(lp["w_uq"].shape), _const_spec(lp["w_ukv"].shape),
                  _const_spec((1, _MLA_QK_WIDTH)), _const_spec((1, _MLA_QK_WIDTH)), tab, tab, tab],
        out_specs=out_specs,
        out_shape=out_shape,
        compiler_params=_params("parallel"),
        name="inproj",
    )(x2d, lp["mix_g"], lp["w_in"], lp["g64"], lp["da_qg"], lp["da_kg"], lp["cq_g"],
      lp["ckv_g"], lp["w_uq"], lp["w_ukv"], lp["mla_qg"], lp["mla_kg"],
      lp["rope_cos"], lp["rope_s1"], lp["rope_s2"])


_SUM_ROWS = 16


def _pipelined_tiles(n_tiles, n_chains, stage, consume, run, s_even, s_odd, group):
    groups = [range(g, min(g + group, n_chains)) for g in range(0, n_chains, group)]

    def overlap(j_stage, buf_stage, j_done, buf_done, summary, run):
        new_summary, new_run = [], []
        new_summary += stage(j_stage, buf_stage, groups[0], False)
        for g, members in enumerate(groups):
            if g + 1 < len(groups):
                new_summary += stage(j_stage, buf_stage, groups[g + 1], False)
            new_run += consume(j_done, buf_done, members,
                               [summary[c] for c in members], [run[c] for c in members])
        return tuple(new_run), tuple(new_summary)

    def finish(j, buf, summary, run):
        return tuple(consume(j, buf, range(n_chains), list(summary), list(run)))

    def pair(i, carry):
        run, summary = carry
        run, summary = overlap(2 * i + 1, s_odd, 2 * i, s_even, summary, run)
        return overlap(2 * i + 2, s_even, 2 * i + 1, s_odd, summary, run)

    def tail_two(_, carry):
        run, summary = carry
        run, summary = overlap(n_tiles - 1, s_odd, n_tiles - 2, s_even, summary, run)
        return finish(n_tiles - 1, s_odd, summary, run), summary

    def tail_one(_, carry):
        run, summary = carry
        return finish(n_tiles - 1, s_even, summary, run), summary

    carry = (run, tuple(stage(0, s_even, range(n_chains), True)))
    last = n_tiles - 1
    odd = jnp.bitwise_and(last, 1)
    carry = lax.fori_loop(0, jnp.right_shift(last, 1), pair, carry)
    carry = lax.fori_loop(0, odd, tail_two, carry)
    lax.fori_loop(0, 1 - odd, tail_one, carry)


def _online_softmax(n_tiles, score, value, s_even, s_odd, acc_scr):
    n_chains, _, t = s_even.shape
    ones = jnp.ones((_SUM_ROWS, t), _BF16)

    def stage(ki, buf, members, first):
        col_max = []
        for c in members:
            s = score(c, ki)
            buf[c] = s
            col_max.append(jnp.max(s, axis=0, keepdims=True))
        return col_max

    def consume(ki, buf, members, col_max, m_run):
        out = []
        for c, cm, m_old in zip(members, col_max, m_run):
            m_new = jnp.maximum(m_old, cm)
            alpha = jnp.exp2(m_old - m_new)
            p = jnp.exp2(buf[c] - m_new).astype(_BF16)
            v_ones = jnp.concatenate([value(c, ki), ones], axis=0)
            acc_scr[c] = alpha * acc_scr[c] + jnp.dot(v_ones, p, preferred_element_type=_F32)
            out.append(m_new)
        return out

    for c in range(n_chains):
        acc_scr[c] = jnp.zeros(acc_scr.shape[1:], _F32)
    m_init = tuple(jnp.full((1, t), -jnp.inf, _F32) for _ in range(n_chains))
    _pipelined_tiles(n_tiles, n_chains, stage, consume, m_init, s_even, s_odd, group=1)


def _softmax_finish(acc, rows):
    width = acc.shape[0] - _SUM_ROWS
    return acc[rows] * (1.0 / acc[width:width + 1])


def _rows(i, t, base=0):
    return pl.ds(pl.multiple_of(base + i * t, t), t)


def _split_rows(x_t, half):
    zero = jnp.zeros((half, x_t.shape[1]), x_t.dtype)
    return (jnp.concatenate([x_t[:half], zero], axis=0),
            jnp.concatenate([zero, x_t[half:]], axis=0))


def _half_row_norm(o_t, half, g_t):
    sq = o_t * o_t
    lo = lax.rsqrt(jnp.mean(sq[:half], axis=0, keepdims=True) + _EPS)
    hi = lax.rsqrt(jnp.mean(sq[half:], axis=0, keepdims=True) + _EPS)
    return jnp.concatenate([o_t[:half] * lo, o_t[half:] * hi], axis=0) * g_t


def _da_kernel(lam_ref, q_ref, k_ref, v_ref, bias_ref, g_ref, o_ref, s_even, s_odd, acc_scr, *, t):
    nq = q_ref.shape[0]
    lam = lam_ref[0]

    heads = range(_DA_HEADS)
    head = lambda hd: slice(hd * _LANES, (hd + 1) * _LANES)
    all_rows = slice(0, _DA_VDIM)

    def q_tile(qi, carry):
        qm = [qh for hd in heads for qh in _split_rows(q_ref[qi, head(hd), :], _DA_DIM)]

        def score(c, ki):
            hd, mi = divmod(c, 2)
            d = jnp.minimum(qi - ki, 2)
            return (jnp.dot(k_ref[_rows(ki, t), head(hd)], qm[c], preferred_element_type=_F32)
                    + bias_ref[hd, d, mi])

        value = lambda c, ki: v_ref[ki, head(c // 2), :]
        _online_softmax(qi + 1, score, value, s_even, s_odd, acc_scr)
        for hd in heads:
            o_t = (_softmax_finish(acc_scr[2 * hd], all_rows)
                   - lam * _softmax_finish(acc_scr[2 * hd + 1], all_rows))
            o_t = o_t * lax.rsqrt(jnp.mean(o_t * o_t, axis=0, keepdims=True) + _EPS) * g_ref[...]
            o_ref[_rows(qi, t), head(hd)] = o_t.T.astype(o_ref.dtype)
        return carry

    lax.fori_loop(0, nq, q_tile, 0)


def _da_attention(q_t, k, v_t, bias, lam, g_t, batch, seq):
    t = _ATT_TILE
    nq = seq // t
    width = _DA_HEADS * _LANES
    tiles = pl.BlockSpec((nq, width, t), lambda b: (b, 0, 0))
    rows = pl.BlockSpec((seq, width), lambda b: (b, 0))
    return pl.pallas_call(
        functools.partial(_da_kernel, t=t),
        grid=(batch,),
        in_specs=[pl.BlockSpec(memory_space=pltpu.SMEM), tiles, rows, tiles,
                  _const_spec(bias.shape), _const_spec((_LANES, t))],
        out_specs=rows,
        out_shape=jax.ShapeDtypeStruct(k.shape, _BF16),
        scratch_shapes=[pltpu.VMEM((2 * _DA_HEADS, t, t), _F32), pltpu.VMEM((2 * _DA_HEADS, t, t), _F32),
                        pltpu.VMEM((2 * _DA_HEADS, _DA_VDIM + _SUM_ROWS, t), _F32)],
        compiler_params=_params("parallel"),
        name="diff_attention",
    )(lam, q_t, k, v_t, bias, g_t)


_PAIR_BATCH = 2
_SB_LOGIT_LOOKAHEAD = 2


def _mla_kernel(q_ref, k_ref, v_ref, mask_ref, g_ref, o_ref, s_even, s_odd, acc_scr, *, t, nb):
    nq = q_ref.shape[0] // nb
    seq = k_ref.shape[0] // nb

    head = lambda hd: slice(hd * _LANES, (hd + 1) * _LANES)
    pair = lambda hd: slice((hd // 2) * _LANES, (hd // 2 + 1) * _LANES)
    chains = [(bb, hd) for bb in range(nb) for hd in range(_MLA_HEADS)]

    def q_tile(qi, carry):
        qs = [q_ref[bb * nq + qi, head(hd), :] for bb, hd in chains]

        def score(c, ki):
            bb, hd = chains[c]
            d = jnp.minimum(qi - ki, 1)
            return (jnp.dot(k_ref[_rows(ki, t, bb * seq), head(hd)], qs[c], preferred_element_type=_F32)
                    + mask_ref[d])

        value = lambda c, ki: v_ref[chains[c][0] * nq + ki, pair(chains[c][1]), :]
        _online_softmax(qi + 1, score, value, s_even, s_odd, acc_scr)
        for bb in range(nb):
            for p in range(_MLA_HEADS // 2):
                c = bb * _MLA_HEADS + 2 * p
                o_t = jnp.concatenate([_softmax_finish(acc_scr[c], slice(0, _MLA_V)),
                                       _softmax_finish(acc_scr[c + 1], slice(_MLA_V, _LANES))], axis=0)
                o_ref[_rows(qi, t, bb * seq), head(p)] = (
                    _half_row_norm(o_t, _MLA_V, g_ref[...]).T.astype(o_ref.dtype))
        return carry

    lax.fori_loop(0, nq, q_tile, 0)


def _mla_attention(q_t, k, v_t, mask_t, g_t, batch, seq):
    t = _ATT_TILE
    nb = _PAIR_BATCH
    nq = seq // t
    qk_width = _MLA_HEADS * _LANES
    v_width = _MLA_HEADS * _MLA_V
    chains = nb * _MLA_HEADS
    return pl.pallas_call(
        functools.partial(_mla_kernel, t=t, nb=nb),
        grid=(batch // nb,),
        in_specs=[pl.BlockSpec((nb * nq, qk_width, t), lambda b: (b, 0, 0)),
                  pl.BlockSpec((nb * seq, qk_width), lambda b: (b, 0)),
                  pl.BlockSpec((nb * nq, v_width, t), lambda b: (b, 0, 0)),
                  _const_spec((2, t, t)), _const_spec((_LANES, t))],
        out_specs=pl.BlockSpec((nb * seq, v_width), lambda b: (b, 0)),
        out_shape=jax.ShapeDtypeStruct((batch * seq, v_width), _BF16),
        scratch_shapes=[pltpu.VMEM((chains, t, t), _F32), pltpu.VMEM((chains, t, t), _F32),
                        pltpu.VMEM((chains, _LANES + _SUM_ROWS, t), _F32)],
        compiler_params=_params("parallel"),
        name="latent_attention",
    )(q_t, k, v_t, mask_t, g_t)


def _sb_kernel(q_ref, k_ref, v_ref, g_ref, o_ref, s_even, s_odd, acc_scr, *, t, nb):
    nq = q_ref.shape[0] // nb
    seq = k_ref.shape[0] // nb
    key_idx = lax.broadcasted_iota(jnp.int32, (t, t), 0)
    query_idx = lax.broadcasted_iota(jnp.int32, (t, t), 1)
    earlier = key_idx < query_idx
    tri = jnp.where(earlier, 1.0, 0.0).astype(_BF16)

    pair = lambda hd: slice((hd // 2) * _LANES, (hd // 2 + 1) * _LANES)
    dot = functools.partial(jnp.dot, preferred_element_type=_F32)
    chains = [(bb, hd) for bb in range(nb) for hd in range(_SB_HEADS)]
    n_chains = range(len(chains))

    def q_tile(qi, carry):
        qh = [q for bb in range(nb) for p in range(_SB_HEADS // 2)
              for q in _split_rows(q_ref[bb * nq + qi, pair(2 * p), :], _SB_DIM)]

        def stage(j, buf, members, diag):
            members = list(members)
            logit = lambda c: dot(k_ref[_rows(qi - j, t, chains[c][0] * seq), pair(chains[c][1])], qh[c])
            z = [logit(c) for c in members[:_SB_LOGIT_LOOKAHEAD]]
            log_beta, first_row, later = [], [], []
            for n in range(len(members)):
                if n + _SB_LOGIT_LOOKAHEAD < len(members):
                    z.append(logit(members[n + _SB_LOGIT_LOOKAHEAD]))
                zi = z[n]
                lp = jnp.log2(1.0 + jnp.exp2(jnp.minimum(zi, -zi)))
                lb = jnp.minimum(zi, 0.0) - lp
                log_1m = lb - zi
                if diag:
                    log_1m = jnp.where(earlier, log_1m, 0.0)
                later.append(dot(tri, log_1m.astype(_BF16)))
                log_beta.append(lb)
                first_row.append(log_1m[0:1, :])
            for c, lb, la in zip(members, log_beta, later):
                log_w = lb + la
                buf[c] = jnp.where(earlier, log_w, -jnp.inf) if diag else log_w
            return [la[0:1, :] + fr for la, fr in zip(later, first_row)]

        def consume(j, buf, members, through, run):
            for c, r in zip(members, run):
                bb, hd = chains[c]
                a = jnp.exp2(buf[c] + r).astype(_BF16)
                acc_scr[c] = acc_scr[c] + dot(v_ref[bb * nq + qi - j, pair(hd), :], a)
            return [r + th for r, th in zip(run, through)]

        for c in n_chains:
            acc_scr[c] = jnp.zeros(acc_scr.shape[1:], _F32)
        run0 = tuple(jnp.zeros((1, t), _F32) for _ in n_chains)
        _pipelined_tiles(qi + 1, len(chains), stage, consume, run0, s_even, s_odd, group=len(chains))
        for bb in range(nb):
            for p in range(_SB_HEADS // 2):
                c = bb * _SB_HEADS + 2 * p
                o_t = jnp.concatenate([acc_scr[c, :_SB_DIM, :], acc_scr[c + 1, _SB_DIM:, :]], axis=0)
                o_ref[_rows(qi, t, bb * seq), pair(2 * p)] = (
                    _half_row_norm(o_t, _SB_DIM, g_ref[...]).T.astype(o_ref.dtype))
        return carry

    lax.fori_loop(0, nq, q_tile, 0)


def _sb_attention(q_t, k, v_t, g_t, batch, seq):
    t = _ATT_TILE
    nb = _PAIR_BATCH
    nq = seq // t
    width = _SB_HEADS * _SB_DIM
    chains = nb * _SB_HEADS
    tiles = pl.BlockSpec((nb * nq, width, t), lambda b: (b, 0, 0))
    rows = pl.BlockSpec((nb * seq, width), lambda b: (b, 0))
    return pl.pallas_call(
        functools.partial(_sb_kernel, t=t, nb=nb),
        grid=(batch // nb,),
        in_specs=[tiles, rows, tiles, _const_spec((_LANES, t))],
        out_specs=rows,
        out_shape=jax.ShapeDtypeStruct(k.shape, _BF16),
        scratch_shapes=[pltpu.VMEM((chains, t, t), _F32), pltpu.VMEM((chains, t, t), _F32),
                        pltpu.VMEM((chains, _LANES, t), _F32)],
        compiler_params=_params("parallel"),
        name="stick_breaking",
    )(q_t, k, v_t, g_t)


def _memkv_kernel(mem_ref, g_ref, w_ref, g64_ref, kg_ref, k_ref, v_ref):
    m = _rms(mem_ref[...], g_ref[...]).astype(_BF16)
    kv = jnp.dot(m, w_ref[...], preferred_element_type=_F32)
    width = _MEM_HEADS * _MEM_DIM
    k = kv[:, :width]
    k = k * lax.rsqrt(_group_mean_sq(k, g64_ref[...], _MEM_DIM) + _EPS) * kg_ref[...]
    k_ref[...] = k.astype(_BF16)
    v_ref[...] = kv[:, width:].astype(_BF16)


def _memkv(mem2d, g, w, g64, kg):
    depth = w.shape[0]
    rows, d = mem2d.shape
    tm = _ROW_TILE
    width = _MEM_HEADS * _MEM_DIM
    out = pl.BlockSpec((None, tm, width), lambda l, i: (l, i, 0))
    return pl.pallas_call(
        _memkv_kernel,
        grid=(depth, rows // tm),
        in_specs=[pl.BlockSpec((tm, d), lambda l, i: (i, 0)),
                  pl.BlockSpec((None, 1, d), lambda l, i: (l, 0, 0)),
                  pl.BlockSpec((None, d, 2 * width), lambda l, i: (l, 0, 0)),
                  _const_spec((_MXU, _MXU)),
                  pl.BlockSpec((None, 1, width), lambda l, i: (l, 0, 0))],
        out_specs=[out, out],
        out_shape=[jax.ShapeDtypeStruct((depth, rows, width), _BF16)] * 2,
        compiler_params=_params("parallel", "parallel"),
        name="memory_kv",
    )(mem2d, g, w, g64, kg)


def _mix_mem_kernel(x_ref, ya_ref, yb_ref, yc_ref, wo_ref, gx_ref, wq_ref, g64_ref, qg_ref,
                    km_ref, vm_ref, wmo_ref, o_ref):
    wa = ya_ref.shape[1]
    wb = wa + yb_ref.shape[1]
    x = (x_ref[...]
         + jnp.dot(ya_ref[...], wo_ref[:wa, :], preferred_element_type=_F32)
         + jnp.dot(yb_ref[...], wo_ref[wa:wb, :], preferred_element_type=_F32)
         + jnp.dot(yc_ref[...], wo_ref[wb:, :], preferred_element_type=_F32))
    h = _rms(x, gx_ref[...]).astype(_BF16)
    q = jnp.dot(h, wq_ref[...], preferred_element_type=_F32)
    q = (q * lax.rsqrt(_group_mean_sq(q, g64_ref[...], _MEM_DIM) + _EPS) * qg_ref[...]).astype(_BF16)
    km = km_ref[...]
    vm = vm_ref[...]
    head_of_lane = lax.broadcasted_iota(jnp.int32, (1, q.shape[1]), 1) // _MEM_DIM
    zero = jnp.zeros_like(q)
    o = jnp.zeros(q.shape, _F32)
    sels = [head_of_lane == hd for hd in range(_MEM_HEADS)]
    scores = [lax.dot_general(jnp.where(sel, q, zero), km, _NT, preferred_element_type=_F32) for sel in sels]
    for sel, s in zip(sels, scores):
        p = jnp.exp(s - jnp.max(s, axis=-1, keepdims=True))
        oh = jnp.dot(p.astype(_BF16), vm, preferred_element_type=_F32)
        o = jnp.where(sel, oh / jnp.sum(p, axis=-1, keepdims=True), o)
    o_ref[...] = x + jnp.dot(o.astype(_BF16), wmo_ref[...], preferred_element_type=_F32)


def _mix_mem(x2d, ya, yb, yc, lp, km, vm, seq):
    m, d = x2d.shape
    tm = _MIX_ROWS
    per_seq = seq // tm
    n_mem, width = km.shape[1], km.shape[2]
    row = lambda c: pl.BlockSpec((tm, c), lambda i: (i, 0))
    mem = pl.BlockSpec((None, n_mem, width), lambda i: (i // per_seq, 0, 0))
    return pl.pallas_call(
        _mix_mem_kernel,
        grid=(m // tm,),
        in_specs=[row(d), row(ya.shape[1]), row(yb.shape[1]), row(yc.shape[1]),
                  _const_spec(lp["w_out"].shape), _const_spec((1, d)),
                  _const_spec(lp["w_mem_q"].shape), _const_spec((_MXU, _MXU)),
                  _const_spec((1, width)), mem, mem, _const_spec(lp["w_mem_o"].shape)],
        out_specs=row(d),
        out_shape=jax.ShapeDtypeStruct((m, d), _F32),
        compiler_params=_params("parallel"),
        name="mix_and_memory",
    )(x2d, ya, yb, yc, lp["w_out"], lp["memx_g"], lp["w_mem_q"], lp["g64"], lp["mem_qg"],
      km, vm, lp["w_mem_o"])


_FF_CHUNK = 1024


def _ffn_kernel(x_ref, g_ref, w1_ref, w2_ref, o_ref):
    x = x_ref[...]
    h = _rms(x, g_ref[...]).astype(_BF16)
    acc = x
    for c in range(0, w1_ref.shape[1], _FF_CHUNK):
        u = jnp.dot(h, w1_ref[:, c:c + _FF_CHUNK], preferred_element_type=_F32)
        r = jnp.maximum(u, 0.0)
        acc = acc + jnp.dot((r * r).astype(_BF16), w2_ref[c:c + _FF_CHUNK, :],
                            preferred_element_type=_F32)
    o_ref[...] = acc


def _ffn(x2d, g, w1, w2):
    m, d = x2d.shape
    tm = _ROW_TILE
    row = pl.BlockSpec((tm, d), lambda i: (i, 0))
    return pl.pallas_call(
        _ffn_kernel,
        grid=(m // tm,),
        in_specs=[row, _const_spec((1, d)), _const_spec(w1.shape), _const_spec(w2.shape)],
        out_specs=row,
        out_shape=jax.ShapeDtypeStruct((m, d), _F32),
        compiler_params=_params("parallel"),
        name="ffn",
    )(x2d, g, w1, w2)


def _t5_bucket(rel):
    nb = _NUM_BUCKETS // 2
    bucket = (rel > 0).astype(jnp.int32) * nb
    n = jnp.abs(rel)
    max_exact = nb // 2
    is_small = n < max_exact
    large = max_exact + (jnp.log(jnp.maximum(n, 1).astype(jnp.float32) / max_exact)
                         / math.log(_MAX_DISTANCE / max_exact) * (nb - max_exact)).astype(jnp.int32)
    large = jnp.minimum(large, nb - 1)
    return bucket + jnp.where(is_small, n, large)


def _da_bias_tables(rel_bias, t):
    assert t + 1 >= _MAX_DISTANCE and t % _CHUNK == 0
    j = jnp.arange(t, dtype=jnp.int32)[:, None]
    i = jnp.arange(t, dtype=jnp.int32)[None, :]
    rb = rel_bias.astype(_F32) * _LOG2E

    def lookup(bucket):
        hit = bucket[:, :, None, None] == jnp.arange(_NUM_BUCKETS, dtype=jnp.int32)[:, None]
        return jnp.sum(jnp.where(hit, rb[None, None], 0.0), axis=2)

    b0 = lookup(_t5_bucket(j - i))
    b0 = jnp.where(((j // _CHUNK) <= (i // _CHUNK))[:, :, None], b0, -jnp.inf)
    b1 = lookup(_t5_bucket(j - i - t))
    far = jnp.broadcast_to(lookup(_t5_bucket(jnp.full((1, 1), -(t + 1), jnp.int32))), b1.shape)
    tab = jnp.stack([b0, b1, far]).reshape(3, t, t, _DA_HEADS, 2)
    return tab.transpose(3, 0, 4, 1, 2)


def _rope_tables(seq):
    half = _MLA_ROPE // 2
    freqs = _ROPE_THETA ** (-jnp.arange(half, dtype=jnp.float32) / half)
    ang = jnp.arange(seq, dtype=jnp.int32).astype(jnp.float32)[:, None] * freqs[None, :]
    cos, sin = jnp.cos(ang), jnp.sin(ang)
    ones = jnp.ones((seq, _MLA_NOPE), _F32)
    z = lambda w: jnp.zeros((seq, w), _F32)
    tail = _LANES - _MLA_QK
    c = jnp.concatenate([ones, cos, cos, z(tail)], axis=1)
    s1 = jnp.concatenate([z(_MLA_NOPE + half), sin, z(tail)], axis=1)
    s2 = jnp.concatenate([z(_MLA_NOPE), -sin, z(half + tail)], axis=1)
    return c, s1, s2


def _group_ones(group):
    idx = np.arange(_MXU) // group
    return jnp.asarray(idx[:, None] == idx[None, :], dtype=_BF16)


def _layer_params(l, p, rope, g64):
    d = p["w_in"].shape[1]
    o = _IN_OFFS
    w_in = p["w_in"][l]
    kr = jnp.zeros((d, _LANES), _F32).at[:, _MLA_NOPE:_MLA_QK].set(w_in[:, o[8]:o[9]])
    w_uq = jnp.pad(p["w_mla_uq"][l].reshape(_MLA_Q_RANK, _MLA_HEADS, _MLA_QK),
                   ((0, 0), (0, 0), (0, _LANES - _MLA_QK))).reshape(_MLA_Q_RANK, -1)
    w_ukv = p["w_mla_ukv"][l].reshape(_MLA_KV_RANK, _MLA_HEADS, _MLA_NOPE + _MLA_V)
    w_k = jnp.pad(w_ukv[:, :, :_MLA_NOPE], ((0, 0), (0, 0), (0, _LANES - _MLA_NOPE)))
    w_v = w_ukv[:, :, _MLA_NOPE:]
    pad_g = lambda g: jnp.tile(jnp.pad(g, (0, _LANES - _MLA_QK)), _MLA_HEADS)[None]
    col = lambda g: jnp.broadcast_to(g[:, None], (g.shape[0], _ATT_TILE))
    lam_init = 0.8 - 0.6 * math.exp(-0.3 * l)
    lp = p["da_lambda"][l].astype(_F32)
    lam = jnp.exp(jnp.sum(lp[0] * lp[1])) - jnp.exp(jnp.sum(lp[2] * lp[3])) + lam_init
    return {
        "mix_g": p["mix_norm_g"][l][None],
        "w_in": jnp.concatenate([w_in[:, :o[8]], kr], axis=1).astype(_BF16),
        "g64": g64,
        "da_qg": jnp.tile(p["da_q_norm_g"][l], 2 * _DA_HEADS)[None] * (_DA_DIM ** -0.5 * _LOG2E),
        "da_kg": jnp.tile(p["da_k_norm_g"][l], 2 * _DA_HEADS)[None],
        "cq_g": p["mla_cq_norm_g"][l][None],
        "ckv_g": p["mla_ckv_norm_g"][l][None],
        "w_uq": w_uq.astype(_BF16),
        "w_ukv": jnp.concatenate([w_k.reshape(_MLA_KV_RANK, -1), w_v.reshape(_MLA_KV_RANK, -1)],
                                 axis=1).astype(_BF16),
        "mla_qg": pad_g(p["mla_q_norm_g"][l]) * (_MLA_QK ** -0.5 * _LOG2E),
        "mla_kg": pad_g(p["mla_k_norm_g"][l]),
        "rope_cos": rope[0], "rope_s1": rope[1], "rope_s2": rope[2],
        "lam": jnp.reshape(lam, (1,)).astype(_F32),
        "da_og": col(p["da_subln_g"][l] * (1.0 - lam_init)),
        "sb_og": col(jnp.tile(p["sb_out_g"][l], 2)),
        "mla_og": col(jnp.tile(p["mla_out_g"][l], 2)),
        "w_out": p["w_out"][l].astype(_BF16),
        "memx_g": p["memx_norm_g"][l][None],
        "w_mem_q": p["w_mem_q"][l].astype(_BF16),
        "mem_qg": jnp.tile(p["mem_q_norm_g"][l], _MEM_HEADS)[None] * (_MEM_DIM ** -0.5),
        "w_mem_o": p["w_mem_o"][l].astype(_BF16),
        "ffn_g": p["ffn_norm_g"][l][None],
        "w_ff1": p["w_ff1"][l].astype(_BF16),
        "w_ff2": p["w_ff2"][l].astype(_BF16),
    }


def kernel(x, mem, rel_bias, mix_norm_g, w_in, da_q_norm_g, da_k_norm_g, da_lambda, da_subln_g,
           sb_out_g, mla_cq_norm_g, mla_ckv_norm_g, w_mla_uq, w_mla_ukv, mla_q_norm_g, mla_k_norm_g,
           mla_out_g, w_out, memx_norm_g, mem_norm_g, w_mem_q, w_mem_kv, mem_q_norm_g, mem_k_norm_g,
           w_mem_o, ffn_norm_g, w_ff1, w_ff2):
    p = dict(mix_norm_g=mix_norm_g, w_in=w_in, da_q_norm_g=da_q_norm_g, da_k_norm_g=da_k_norm_g,
             da_lambda=da_lambda, da_subln_g=da_subln_g, sb_out_g=sb_out_g,
             mla_cq_norm_g=mla_cq_norm_g, mla_ckv_norm_g=mla_ckv_norm_g, w_mla_uq=w_mla_uq,
             w_mla_ukv=w_mla_ukv, mla_q_norm_g=mla_q_norm_g, mla_k_norm_g=mla_k_norm_g,
             mla_out_g=mla_out_g, w_out=w_out, memx_norm_g=memx_norm_g, w_mem_q=w_mem_q,
             mem_q_norm_g=mem_q_norm_g, w_mem_o=w_mem_o, ffn_norm_g=ffn_norm_g, w_ff1=w_ff1,
             w_ff2=w_ff2)
    batch, seq, d = x.shape
    depth = w_in.shape[0]
    n_mem = mem.shape[1]
    t = _ATT_TILE
    assert seq % _ROW_TILE == 0 and seq % _MIX_ROWS == 0 and seq % t == 0
    assert (batch * n_mem) % _ROW_TILE == 0
    assert batch % _PAIR_BATCH == 0
    assert w_in.shape[2] == _IN_OFFS[-1]

    g64 = _group_ones(_MEM_DIM)
    rope = _rope_tables(seq)
    bias = _da_bias_tables(rel_bias, t)
    i = jnp.arange(t, dtype=jnp.int32)
    chunk_mask = jnp.where((i[:, None] // _CHUNK) <= (i[None, :] // _CHUNK), 0.0, -jnp.inf).astype(_F32)
    chunk_mask = jnp.stack([chunk_mask, jnp.zeros_like(chunk_mask)])

    width = _MEM_HEADS * _MEM_DIM
    km, vm = _memkv(mem.reshape(batch * n_mem, d), mem_norm_g[:, None, :], w_mem_kv.astype(_BF16), g64,
                    jnp.tile(mem_k_norm_g, (1, _MEM_HEADS))[:, None, :])
    km = km.reshape(depth, batch, n_mem, width)
    vm = vm.reshape(depth, batch, n_mem, width)

    x2d = x.reshape(batch * seq, d)
    for l in range(depth):
        lp = _layer_params(l, p, rope, g64)
        daq, dak, dav, sbq, sbk, sbv, mq, mk, mv = _inproj(x2d, lp, seq)
        ya = _da_attention(daq, dak, dav, bias, lp["lam"], lp["da_og"], batch, seq)
        yb = _sb_attention(sbq, sbk, sbv, lp["sb_og"], batch, seq)
        yc = _mla_attention(mq, mk, mv, chunk_mask, lp["mla_og"], batch, seq)
        x2d = _mix_mem(x2d, ya, yb, yc, lp, km[l], vm[l], seq)
        x2d = _ffn(x2d, lp["ffn_g"], lp["w_ff1"], lp["w_ff2"])
    return x2d.reshape(batch, seq, d)
```

```python
import functools
import math

import numpy as np
import jax
import jax.numpy as jnp
from jax import lax
from jax.experimental import pallas as pl
from jax.experimental.pallas import tpu as pltpu

_F32 = jnp.float32
_BF16 = jnp.bfloat16
_EPS = 1e-6

_CHUNK = 64
_DA_HEADS, _DA_DIM = 4, 64
_DA_VDIM = 2 * _DA_DIM
_SB_HEADS, _SB_DIM = 4, 64
_MLA_HEADS, _MLA_NOPE, _MLA_ROPE, _MLA_V = 4, 64, 32, 64
_MLA_QK = _MLA_NOPE + _MLA_ROPE
_MLA_Q_RANK, _MLA_KV_RANK = 256, 128
_ROPE_THETA = 10000.0
_NUM_BUCKETS, _MAX_DISTANCE = 32, 128
_MEM_HEADS, _MEM_DIM = 4, 64

_LANES = 128
_MXU = 256
_VMEM_BYTES = 64 * 1024 * 1024
_VMEM_LIMIT = _VMEM_BYTES - 12 * 1024 * 1024

_ATT_TILE = 256
_ROW_TILE = 1024

_DA_WIDTH = _DA_HEADS * _DA_VDIM
_SB_WIDTH = _SB_HEADS * _SB_DIM
_MLA_QK_WIDTH = _MLA_HEADS * _LANES
_MLA_V_WIDTH = _MLA_HEADS * _MLA_V
_IN_SIZES = (_DA_WIDTH, _DA_WIDTH, _DA_WIDTH, _SB_WIDTH, _SB_WIDTH, _SB_WIDTH,
             _MLA_Q_RANK, _MLA_KV_RANK, _MLA_ROPE)
_IN_OFFS = tuple(int(v) for v in np.cumsum((0,) + _IN_SIZES))
_NT = (((1,), (1,)), ((), ()))
_LOG2E = math.log2(math.e)


def _const_spec(shape):
    zeros = (0,) * len(shape)
    return pl.BlockSpec(shape, lambda *_: zeros, pipeline_mode=pl.Buffered(1))


def _params(*sem):
    return pltpu.CompilerParams(dimension_semantics=sem, vmem_limit_bytes=_VMEM_LIMIT)


def _rms(x, g):
    return x * lax.rsqrt(jnp.mean(x * x, axis=-1, keepdims=True) + _EPS) * g


def _group_mean_sq(y, gmat, group):
    sq = (y * y).astype(_BF16)
    cols = y.shape[1]
    parts = [jnp.dot(sq[:, c:c + _MXU], gmat, preferred_element_type=_F32)
             for c in range(0, cols, _MXU)]
    ss = parts[0] if len(parts) == 1 else jnp.concatenate(parts, axis=1)
    return ss * (1.0 / group)


def _inproj_kernel(x_ref, gmix_ref, w_ref, g64_ref, gq_ref, gk_ref, cqg_ref, ckvg_ref,
                   wuq_ref, wukv_ref, qg_ref, kg_ref, cos_ref, s1_ref, s2_ref,
                   daq_ref, dak_ref, dav_ref, sbq_ref, sbk_ref, sbv_ref,
                   mq_ref, mk_ref, mv_ref):
    x = x_ref[...]
    h = _rms(x, gmix_ref[...]).astype(_BF16)
    o = _IN_OFFS

    def proj(seg, width=None):
        hi = o[seg + 1] if width is None else o[seg] + width
        return jnp.dot(h, w_ref[:, o[seg]:hi], preferred_element_type=_F32)

    g64 = g64_ref[...]
    t = daq_ref.shape[2]

    def norm64(y, g):
        return y * lax.rsqrt(_group_mean_sq(y, g64, _DA_DIM) + _EPS) * g

    def store_t(ref, y, row0=0):
        for r in range(y.shape[0] // t):
            ref[r, row0:row0 + y.shape[1], :] = y[r * t:(r + 1) * t, :].T.astype(_BF16)

    cos, s1, s2 = cos_ref[...], s1_ref[...], s2_ref[...]

    def head_norm_rope(y, g):
        ms = jnp.sum(y * y, axis=-1, keepdims=True) * (1.0 / _MLA_QK)
        yn = y * lax.rsqrt(ms + _EPS) * g
        half = _MLA_ROPE // 2
        return (yn * cos + pltpu.roll(yn, half, 1) * s1
                + pltpu.roll(yn, _LANES - half, 1) * s2)

    cq = _rms(proj(6), cqg_ref[...]).astype(_BF16)
    q_all = jnp.dot(cq, wuq_ref[...], preferred_element_type=_F32)
    ckv = _rms(proj(7), ckvg_ref[...]).astype(_BF16)
    kv_all = jnp.dot(ckv, wukv_ref[...], preferred_element_type=_F32)
    k_rope = proj(8, _LANES)

    def latent_head(hd):
        sl = slice(hd * _LANES, (hd + 1) * _LANES)
        store_t(mq_ref, head_norm_rope(q_all[:, sl], qg_ref[:, sl]), hd * _LANES)
        mk_ref[:, sl] = head_norm_rope(kv_all[:, sl] + k_rope, kg_ref[:, sl]).astype(_BF16)

    store_t(daq_ref, norm64(proj(0), gq_ref[...]))
    latent_head(0)
    dak_ref[...] = norm64(proj(1), gk_ref[...]).astype(_BF16)
    latent_head(1)
    store_t(dav_ref, proj(2))
    latent_head(2)
    store_t(sbq_ref, proj(3) * (_SB_DIM ** -0.5 * _LOG2E))
    latent_head(3)
    sbk_ref[...] = proj(4).astype(_BF16)
    store_t(mv_ref, kv_all[:, _MLA_HEADS * _LANES:])
    store_t(sbv_ref, proj(5))


def _inproj(x2d, lp, seq):
    m, d = x2d.shape
    tm = _ROW_TILE
    t = _ATT_TILE
    pos_blocks = seq // tm
    tab = pl.BlockSpec((tm, _LANES), lambda i: (i % pos_blocks, 0))
    widths = (_DA_WIDTH, _DA_WIDTH, _DA_WIDTH, _SB_WIDTH, _SB_WIDTH, _SB_WIDTH,
              _MLA_QK_WIDTH, _MLA_QK_WIDTH, _MLA_V_WIDTH)
    transposed = (True, False, True, True, False, True, True, False, True)
    out_specs, out_shape = [], []
    for c, tr in zip(widths, transposed):
        if tr:
            out_specs.append(pl.BlockSpec((tm // t, c, t), lambda i: (i, 0, 0)))
            out_shape.append(jax.ShapeDtypeStruct((m // t, c, t), _BF16))
        else:
            out_specs.append(pl.BlockSpec((tm, c), lambda i: (i, 0)))
            out_shape.append(jax.ShapeDtypeStruct((m, c), _BF16))
    row = lambda c: pl.BlockSpec((tm, c), lambda i: (i, 0))
    return pl.pallas_call(
        _inproj_kernel,
        grid=(m // tm,),
        in_specs=[row(d), _const_spec((1, d)), _const_spec(lp["w_in"].shape),
                  _const_spec((_MXU, _MXU)), _const_spec((1, _DA_WIDTH)), _const_spec((1, _DA_WIDTH)),
                  _const_spec((1, _MLA_Q_RANK)), _const_spec((1, _MLA_KV_RANK)),
                  _const_spec(lp["w_uq"].shape), _const_spec(lp["w_ukv"].shape),
                  _const_spec((1, _MLA_QK_WIDTH)), _const_spec((1, _MLA_QK_WIDTH)), tab, tab, tab],
        out_specs=out_specs,
        out_shape=out_shape,
        compiler_params=_params("parallel"),
        name="inproj",
    )(x2d, lp["mix_g"], lp["w_in"], lp["g64"], lp["da_qg"], lp["da_kg"], lp["cq_g"],
      lp["ckv_g"], lp["w_uq"], lp["w_ukv"], lp["mla_qg"], lp["mla_kg"],
      lp["rope_cos"], lp["rope_s1"], lp["rope_s2"])


_SUM_ROWS = 16


def _pipelined_tiles(n_tiles, n_chains, stage, consume, run, s_even, s_odd, group):
    groups = [range(g, min(g + group, n_chains)) for g in range(0, n_chains, group)]

    def overlap(j_stage, buf_stage, j_done, buf_done, summary, run):
        new_summary, new_run = [], []
        new_summary += stage(j_stage, buf_stage, groups[0], False)
        for g, members in enumerate(groups):
            if g + 1 < len(groups):
                new_summary += stage(j_stage, buf_stage, groups[g + 1], False)
            new_run += consume(j_done, buf_done, members,
                               [summary[c] for c in members], [run[c] for c in members])
        return tuple(new_run), tuple(new_summary)

    def finish(j, buf, summary, run):
        return tuple(consume(j, buf, range(n_chains), list(summary), list(run)))

    def pair(i, carry):
        run, summary = carry
        run, summary = overlap(2 * i + 1, s_odd, 2 * i, s_even, summary, run)
        return overlap(2 * i + 2, s_even, 2 * i + 1, s_odd, summary, run)

    def tail_two(_, carry):
        run, summary = carry
        run, summary = overlap(n_tiles - 1, s_odd, n_tiles - 2, s_even, summary, run)
        return finish(n_tiles - 1, s_odd, summary, run), summary

    def tail_one(_, carry):
        run, summary = carry
        return finish(n_tiles - 1, s_even, summary, run), summary

    carry = (run, tuple(stage(0, s_even, range(n_chains), True)))
    last = n_tiles - 1
    odd = jnp.bitwise_and(last, 1)
    carry = lax.fori_loop(0, jnp.right_shift(last, 1), pair, carry)
    carry = lax.fori_loop(0, odd, tail_two, carry)
    lax.fori_loop(0, 1 - odd, tail_one, carry)


def _online_softmax(n_tiles, score, value, s_even, s_odd, acc_scr):
    n_chains, _, t = s_even.shape
    ones = jnp.ones((_SUM_ROWS, t), _BF16)

    def stage(ki, buf, members, first):
        col_max = []
        for c in members:
            s = score(c, ki)
            buf[c] = s
            col_max.append(jnp.max(s, axis=0, keepdims=True))
        return col_max

    def consume(ki, buf, members, col_max, m_run):
        out = []
        for c, cm, m_old in zip(members, col_max, m_run):
            m_new = jnp.maximum(m_old, cm)
            alpha = jnp.exp2(m_old - m_new)
            p = jnp.exp2(buf[c] - m_new).astype(_BF16)
            v_ones = jnp.concatenate([value(c, ki), ones], axis=0)
            acc_scr[c] = alpha * acc_scr[c] + jnp.dot(v_ones, p, preferred_element_type=_F32)
            out.append(m_new)
        return out

    for c in range(n_chains):
        acc_scr[c] = jnp.zeros(acc_scr.shape[1:], _F32)
    m_init = tuple(jnp.full((1, t), -jnp.inf, _F32) for _ in range(n_chains))
    _pipelined_tiles(n_tiles, n_chains, stage, consume, m_init, s_even, s_odd, group=1)


def _softmax_finish(acc, rows):
    width = acc.shape[0] - _SUM_ROWS
    return acc[rows] * (1.0 / acc[width:width + 1])


def _rows(i, t, base=0):
    return pl.ds(pl.multiple_of(base + i * t, t), t)


def _split_rows(x_t, half):
    zero = jnp.zeros((half, x_t.shape[1]), x_t.dtype)
    return (jnp.concatenate([x_t[:half], zero], axis=0),
            jnp.concatenate([zero, x_t[half:]], axis=0))


def _half_row_norm(o_t, half, g_t):
    sq = o_t * o_t
    lo = lax.rsqrt(jnp.mean(sq[:half], axis=0, keepdims=True) + _EPS)
    hi = lax.rsqrt(jnp.mean(sq[half:], axis=0, keepdims=True) + _EPS)
    return jnp.concatenate([o_t[:half] * lo, o_t[half:] * hi], axis=0) * g_t


def _da_kernel(lam_ref, q_ref, k_ref, v_ref, bias_ref, g_ref, o_ref, s_even, s_odd, acc_scr, *, t):
    nq = q_ref.shape[0]
    lam = lam_ref[0]

    heads = range(_DA_HEADS)
    head = lambda hd: slice(hd * _LANES, (hd + 1) * _LANES)
    all_rows = slice(0, _DA_VDIM)

    def q_tile(qi, carry):
        qm = [qh for hd in heads for qh in _split_rows(q_ref[qi, head(hd), :], _DA_DIM)]

        def score(c, ki):
            hd, mi = divmod(c, 2)
            d = jnp.minimum(qi - ki, 2)
            return (jnp.dot(k_ref[_rows(ki, t), head(hd)], qm[c], preferred_element_type=_F32)
                    + bias_ref[hd, d, mi])

        value = lambda c, ki: v_ref[ki, head(c // 2), :]
        _online_softmax(qi + 1, score, value, s_even, s_odd, acc_scr)
        for hd in heads:
            o_t = (_softmax_finish(acc_scr[2 * hd], all_rows)
                   - lam * _softmax_finish(acc_scr[2 * hd + 1], all_rows))
            o_t = o_t * lax.rsqrt(jnp.mean(o_t * o_t, axis=0, keepdims=True) + _EPS) * g_ref[...]
            o_ref[_rows(qi, t), head(hd)] = o_t.T.astype(o_ref.dtype)
        return carry

    lax.fori_loop(0, nq, q_tile, 0)


def _da_attention(q_t, k, v_t, bias, lam, g_t, batch, seq):
    t = _ATT_TILE
    nq = seq // t
    width = _DA_HEADS * _LANES
    tiles = pl.BlockSpec((nq, width, t), lambda b: (b, 0, 0))
    rows = pl.BlockSpec((seq, width), lambda b: (b, 0))
    return pl.pallas_call(
        functools.partial(_da_kernel, t=t),
        grid=(batch,),
        in_specs=[pl.BlockSpec(memory_space=pltpu.SMEM), tiles, rows, tiles,
                  _const_spec(bias.shape), _const_spec((_LANES, t))],
        out_specs=rows,
        out_shape=jax.ShapeDtypeStruct(k.shape, _BF16),
        scratch_shapes=[pltpu.VMEM((2 * _DA_HEADS, t, t), _F32), pltpu.VMEM((2 * _DA_HEADS, t, t), _F32),
                        pltpu.VMEM((2 * _DA_HEADS, _DA_VDIM + _SUM_ROWS, t), _F32)],
        compiler_params=_params("parallel"),
        name="diff_attention",
    )(lam, q_t, k, v_t, bias, g_t)


_PAIR_BATCH = 2
_SB_LOGIT_LOOKAHEAD = 2


def _mla_kernel(q_ref, k_ref, v_ref, mask_ref, g_ref, o_ref, s_even, s_odd, acc_scr, *, t, nb):
    nq = q_ref.shape[0] // nb
    seq = k_ref.shape[0] // nb

    head = lambda hd: slice(hd * _LANES, (hd + 1) * _LANES)
    pair = lambda hd: slice((hd // 2) * _LANES, (hd // 2 + 1) * _LANES)
    chains = [(bb, hd) for bb in range(nb) for hd in range(_MLA_HEADS)]

    def q_tile(qi, carry):
        qs = [q_ref[bb * nq + qi, head(hd), :] for bb, hd in chains]

        def score(c, ki):
            bb, hd = chains[c]
            d = jnp.minimum(qi - ki, 1)
            return (jnp.dot(k_ref[_rows(ki, t, bb * seq), head(hd)], qs[c], preferred_element_type=_F32)
                    + mask_ref[d])

        value = lambda c, ki: v_ref[chains[c][0] * nq + ki, pair(chains[c][1]), :]
        _online_softmax(qi + 1, score, value, s_even, s_odd, acc_scr)
        for bb in range(nb):
            for p in range(_MLA_HEADS // 2):
                c = bb * _MLA_HEADS + 2 * p
                o_t = jnp.concatenate([_softmax_finish(acc_scr[c], slice(0, _MLA_V)),
                                       _softmax_finish(acc_scr[c + 1], slice(_MLA_V, _LANES))], axis=0)
                o_ref[_rows(qi, t, bb * seq), head(p)] = (
                    _half_row_norm(o_t, _MLA_V, g_ref[...]).T.astype(o_ref.dtype))
        return carry

    lax.fori_loop(0, nq, q_tile, 0)


def _mla_attention(q_t, k, v_t, mask_t, g_t, batch, seq):
    t = _ATT_TILE
    nb = _PAIR_BATCH
    nq = seq // t
    qk_width = _MLA_HEADS * _LANES
    v_width = _MLA_HEADS * _MLA_V
    chains = nb * _MLA_HEADS
    return pl.pallas_call(
        functools.partial(_mla_kernel, t=t, nb=nb),
        grid=(batch // nb,),
        in_specs=[pl.BlockSpec((nb * nq, qk_width, t), lambda b: (b, 0, 0)),
                  pl.BlockSpec((nb * seq, qk_width), lambda b: (b, 0)),
                  pl.BlockSpec((nb * nq, v_width, t), lambda b: (b, 0, 0)),
                  _const_spec((2, t, t)), _const_spec((_LANES, t))],
        out_specs=pl.BlockSpec((nb * seq, v_width), lambda b: (b, 0)),
        out_shape=jax.ShapeDtypeStruct((batch * seq, v_width), _BF16),
        scratch_shapes=[pltpu.VMEM((chains, t, t), _F32), pltpu.VMEM((chains, t, t), _F32),
                        pltpu.VMEM((chains, _LANES + _SUM_ROWS, t), _F32)],
        compiler_params=_params("parallel"),
        name="latent_attention",
    )(q_t, k, v_t, mask_t, g_t)


def _sb_kernel(q_ref, k_ref, v_ref, g_ref, o_ref, s_even, s_odd, acc_scr, *, t, nb):
    nq = q_ref.shape[0] // nb
    seq = k_ref.shape[0] // nb
    key_idx = lax.broadcasted_iota(jnp.int32, (t, t), 0)
    query_idx = lax.broadcasted_iota(jnp.int32, (t, t), 1)
    earlier = key_idx < query_idx
    tri = jnp.where(earlier, 1.0, 0.0).astype(_BF16)

    pair = lambda hd: slice((hd // 2) * _LANES, (hd // 2 + 1) * _LANES)
    dot = functools.partial(jnp.dot, preferred_element_type=_F32)
    chains = [(bb, hd) for bb in range(nb) for hd in range(_SB_HEADS)]
    n_chains = range(len(chains))

    def q_tile(qi, carry):
        qh = [q for bb in range(nb) for p in range(_SB_HEADS // 2)
              for q in _split_rows(q_ref[bb * nq + qi, pair(2 * p), :], _SB_DIM)]

        def stage(j, buf, members, diag):
            members = list(members)
            logit = lambda c: dot(k_ref[_rows(qi - j, t, chains[c][0] * seq), pair(chains[c][1])], qh[c])
            z = [logit(c) for c in members[:_SB_LOGIT_LOOKAHEAD]]
            log_beta, first_row, later = [], [], []
            for n in range(len(members)):
                if n + _SB_LOGIT_LOOKAHEAD < len(members):
                    z.append(logit(members[n + _SB_LOGIT_LOOKAHEAD]))
                zi = z[n]
                lp = jnp.log2(1.0 + jnp.exp2(jnp.minimum(zi, -zi)))
                lb = jnp.minimum(zi, 0.0) - lp
                log_1m = lb - zi
                if diag:
                    log_1m = jnp.where(earlier, log_1m, 0.0)
                later.append(dot(tri, log_1m.astype(_BF16)))
                log_beta.append(lb)
                first_row.append(log_1m[0:1, :])
            for c, lb, la in zip(members, log_beta, later):
                log_w = lb + la
                buf[c] = jnp.where(earlier, log_w, -jnp.inf) if diag else log_w
            return [la[0:1, :] + fr for la, fr in zip(later, first_row)]

        def consume(j, buf, members, through, run):
            for c, r in zip(members, run):
                bb, hd = chains[c]
                a = jnp.exp2(buf[c] + r).astype(_BF16)
                acc_scr[c] = acc_scr[c] + dot(v_ref[bb * nq + qi - j, pair(hd), :], a)
            return [r + th for r, th in zip(run, through)]

        for c in n_chains:
            acc_scr[c] = jnp.zeros(acc_scr.shape[1:], _F32)
        run0 = tuple(jnp.zeros((1, t), _F32) for _ in n_chains)
        _pipelined_tiles(qi + 1, len(chains), stage, consume, run0, s_even, s_odd, group=len(chains))
        for bb in range(nb):
            for p in range(_SB_HEADS // 2):
                c = bb * _SB_HEADS + 2 * p
                o_t = jnp.concatenate([acc_scr[c, :_SB_DIM, :], acc_scr[c + 1, _SB_DIM:, :]], axis=0)
                o_ref[_rows(qi, t, bb * seq), pair(2 * p)] = (
                    _half_row_norm(o_t, _SB_DIM, g_ref[...]).T.astype(o_ref.dtype))
        return carry

    lax.fori_loop(0, nq, q_tile, 0)


def _sb_attention(q_t, k, v_t, g_t, batch, seq):
    t = _ATT_TILE
    nb = _PAIR_BATCH
    nq = seq // t
    width = _SB_HEADS * _SB_DIM
    chains = nb * _SB_HEADS
    tiles = pl.BlockSpec((nb * nq, width, t), lambda b: (b, 0, 0))
    rows = pl.BlockSpec((nb * seq, width), lambda b: (b, 0))
    return pl.pallas_call(
        functools.partial(_sb_kernel, t=t, nb=nb),
        grid=(batch // nb,),
        in_specs=[tiles, rows, tiles, _const_spec((_LANES, t))],
        out_specs=rows,
        out_shape=jax.ShapeDtypeStruct(k.shape, _BF16),
        scratch_shapes=[pltpu.VMEM((chains, t, t), _F32), pltpu.VMEM((chains, t, t), _F32),
                        pltpu.VMEM((chains, _LANES, t), _F32)],
        compiler_params=_params("parallel"),
        name="stick_breaking",
    )(q_t, k, v_t, g_t)


def _memkv_kernel(mem_ref, g_ref, w_ref, g64_ref, kg_ref, k_ref, v_ref):
    m = _rms(mem_ref[...], g_ref[...]).astype(_BF16)
    kv = jnp.dot(m, w_ref[...], preferred_element_type=_F32)
    width = _MEM_HEADS * _MEM_DIM
    k = kv[:, :width]
    k = k * lax.rsqrt(_group_mean_sq(k, g64_ref[...], _MEM_DIM) + _EPS) * kg_ref[...]
    k_ref[...] = k.astype(_BF16)
    v_ref[...] = kv[:, width:].astype(_BF16)


def _memkv(mem2d, g, w, g64, kg):
    depth = w.shape[0]
    rows, d = mem2d.shape
    tm = _ROW_TILE
    width = _MEM_HEADS * _MEM_DIM
    out = pl.BlockSpec((None, tm, width), lambda l, i: (l, i, 0))
    return pl.pallas_call(
        _memkv_kernel,
        grid=(depth, rows // tm),
        in_specs=[pl.BlockSpec((tm, d), lambda l, i: (i, 0)),
                  pl.BlockSpec((None, 1, d), lambda l, i: (l, 0, 0)),
                  pl.BlockSpec((None, d, 2 * width), lambda l, i: (l, 0, 0)),
                  _const_spec((_MXU, _MXU)),
                  pl.BlockSpec((None, 1, width), lambda l, i: (l, 0, 0))],
        out_specs=[out, out],
        out_shape=[jax.ShapeDtypeStruct((depth, rows, width), _BF16)] * 2,
        compiler_params=_params("parallel", "parallel"),
        name="memory_kv",
    )(mem2d, g, w, g64, kg)


def _mix_mem_kernel(x_ref, ya_ref, yb_ref, yc_ref, wo_ref, gx_ref, wq_ref, g64_ref, qg_ref,
                    km_ref, vm_ref, wmo_ref, o_ref):
    wa = ya_ref.shape[1]
    wb = wa + yb_ref.shape[1]
    x = (x_ref[...]
         + jnp.dot(ya_ref[...], wo_ref[:wa, :], preferred_element_type=_F32)
         + jnp.dot(yb_ref[...], wo_ref[wa:wb, :], preferred_element_type=_F32)
         + jnp.dot(yc_ref[...], wo_ref[wb:, :], preferred_element_type=_F32))
    h = _rms(x, gx_ref[...]).astype(_BF16)
    q = jnp.dot(h, wq_ref[...], preferred_element_type=_F32)
    q = (q * lax.rsqrt(_group_mean_sq(q, g64_ref[...], _MEM_DIM) + _EPS) * qg_ref[...]).astype(_BF16)
    km = km_ref[...]
    vm = vm_ref[...]
    head_of_lane = lax.broadcasted_iota(jnp.int32, (1, q.shape[1]), 1) // _MEM_DIM
    zero = jnp.zeros_like(q)
    o = jnp.zeros(q.shape, _F32)
    sels = [head_of_lane == hd for hd in range(_MEM_HEADS)]
    scores = [lax.dot_general(jnp.where(sel, q, zero), km, _NT, preferred_element_type=_F32) for sel in sels]
    for sel, s in zip(sels, scores):
        p = jnp.exp(s - jnp.max(s, axis=-1, keepdims=True))
        oh = jnp.dot(p.astype(_BF16), vm, preferred_element_type=_F32)
        o = jnp.where(sel, oh / jnp.sum(p, axis=-1, keepdims=True), o)
    o_ref[...] = x + jnp.dot(o.astype(_BF16), wmo_ref[...], preferred_element_type=_F32)


def _mix_mem(x2d, ya, yb, yc, lp, km, vm, seq):
    m, d = x2d.shape
    tm = _ROW_TILE
    per_seq = seq // tm
    n_mem, width = km.shape[1], km.shape[2]
    row = lambda c: pl.BlockSpec((tm, c), lambda i: (i, 0))
    mem = pl.BlockSpec((None, n_mem, width), lambda i: (i // per_seq, 0, 0))
    return pl.pallas_call(
        _mix_mem_kernel,
        grid=(m // tm,),
        in_specs=[row(d), row(ya.shape[1]), row(yb.shape[1]), row(yc.shape[1]),
                  _const_spec(lp["w_out"].shape), _const_spec((1, d)),
                  _const_spec(lp["w_mem_q"].shape), _const_spec((_MXU, _MXU)),
                  _const_spec((1, width)), mem, mem, _const_spec(lp["w_mem_o"].shape)],
        out_specs=row(d),
        out_shape=jax.ShapeDtypeStruct((m, d), _F32),
        compiler_params=_params("parallel"),
        name="mix_and_memory",
    )(x2d, ya, yb, yc, lp["w_out"], lp["memx_g"], lp["w_mem_q"], lp["g64"], lp["mem_qg"],
      km, vm, lp["w_mem_o"])


_FF_CHUNK = 1024


def _ffn_kernel(x_ref, g_ref, w1_ref, w2_ref, o_ref):
    x = x_ref[...]
    h = _rms(x, g_ref[...]).astype(_BF16)
    acc = x
    for c in range(0, w1_ref.shape[1], _FF_CHUNK):
        u = jnp.dot(h, w1_ref[:, c:c + _FF_CHUNK], preferred_element_type=_F32)
        r = jnp.maximum(u, 0.0)
        acc = acc + jnp.dot((r * r).astype(_BF16), w2_ref[c:c + _FF_CHUNK, :],
                            preferred_element_type=_F32)
    o_ref[...] = acc


def _ffn(x2d, g, w1, w2):
    m, d = x2d.shape
    tm = _ROW_TILE
    row = pl.BlockSpec((tm, d), lambda i: (i, 0))
    return pl.pallas_call(
        _ffn_kernel,
        grid=(m // tm,),
        in_specs=[row, _const_spec((1, d)), _const_spec(w1.shape), _const_spec(w2.shape)],
        out_specs=row,
        out_shape=jax.ShapeDtypeStruct((m, d), _F32),
        compiler_params=_params("parallel"),
        name="ffn",
    )(x2d, g, w1, w2)


def _t5_bucket(rel):
    nb = _NUM_BUCKETS // 2
    bucket = (rel > 0).astype(jnp.int32) * nb
    n = jnp.abs(rel)
    max_exact = nb // 2
    is_small = n < max_exact
    large = max_exact + (jnp.log(jnp.maximum(n, 1).astype(jnp.float32) / max_exact)
                         / math.log(_MAX_DISTANCE / max_exact) * (nb - max_exact)).astype(jnp.int32)
    large = jnp.minimum(large, nb - 1)
    return bucket + jnp.where(is_small, n, large)


def _da_bias_tables(rel_bias, t):
    assert t + 1 >= _MAX_DISTANCE and t % _CHUNK == 0
    j = jnp.arange(t, dtype=jnp.int32)[:, None]
    i = jnp.arange(t, dtype=jnp.int32)[None, :]
    rb = rel_bias.astype(_F32) * _LOG2E

    def lookup(bucket):
        hit = bucket[:, :, None, None] == jnp.arange(_NUM_BUCKETS, dtype=jnp.int32)[:, None]
        return jnp.sum(jnp.where(hit, rb[None, None], 0.0), axis=2)

    b0 = lookup(_t5_bucket(j - i))
    b0 = jnp.where(((j // _CHUNK) <= (i // _CHUNK))[:, :, None], b0, -jnp.inf)
    b1 = lookup(_t5_bucket(j - i - t))
    far = jnp.broadcast_to(lookup(_t5_bucket(jnp.full((1, 1), -(t + 1), jnp.int32))), b1.shape)
    tab = jnp.stack([b0, b1, far]).reshape(3, t, t, _DA_HEADS, 2)
    return tab.transpose(3, 0, 4, 1, 2)


def _rope_tables(seq):
    half = _MLA_ROPE // 2
    freqs = _ROPE_THETA ** (-jnp.arange(half, dtype=jnp.float32) / half)
    ang = jnp.arange(seq, dtype=jnp.int32).astype(jnp.float32)[:, None] * freqs[None, :]
    cos, sin = jnp.cos(ang), jnp.sin(ang)
    ones = jnp.ones((seq, _MLA_NOPE), _F32)
    z = lambda w: jnp.zeros((seq, w), _F32)
    tail = _LANES - _MLA_QK
    c = jnp.concatenate([ones, cos, cos, z(tail)], axis=1)
    s1 = jnp.concatenate([z(_MLA_NOPE + half), sin, z(tail)], axis=1)
    s2 = jnp.concatenate([z(_MLA_NOPE), -sin, z(half + tail)], axis=1)
    return c, s1, s2


def _group_ones(group):
    idx = np.arange(_MXU) // group
    return jnp.asarray(idx[:, None] == idx[None, :], dtype=_BF16)


def _layer_params(l, p, rope, g64):
    d = p["w_in"].shape[1]
    o = _IN_OFFS
    w_in = p["w_in"][l]
    kr = jnp.zeros((d, _LANES), _F32).at[:, _MLA_NOPE:_MLA_QK].set(w_in[:, o[8]:o[9]])
    w_uq = jnp.pad(p["w_mla_uq"][l].reshape(_MLA_Q_RANK, _MLA_HEADS, _MLA_QK),
                   ((0, 0), (0, 0), (0, _LANES - _MLA_QK))).reshape(_MLA_Q_RANK, -1)
    w_ukv = p["w_mla_ukv"][l].reshape(_MLA_KV_RANK, _MLA_HEADS, _MLA_NOPE + _MLA_V)
    w_k = jnp.pad(w_ukv[:, :, :_MLA_NOPE], ((0, 0), (0, 0), (0, _LANES - _MLA_NOPE)))
    w_v = w_ukv[:, :, _MLA_NOPE:]
    pad_g = lambda g: jnp.tile(jnp.pad(g, (0, _LANES - _MLA_QK)), _MLA_HEADS)[None]
    col = lambda g: jnp.broadcast_to(g[:, None], (g.shape[0], _ATT_TILE))
    lam_init = 0.8 - 0.6 * math.exp(-0.3 * l)
    lp = p["da_lambda"][l].astype(_F32)
    lam = jnp.exp(jnp.sum(lp[0] * lp[1])) - jnp.exp(jnp.sum(lp[2] * lp[3])) + lam_init
    return {
        "mix_g": p["mix_norm_g"][l][None],
        "w_in": jnp.concatenate([w_in[:, :o[8]], kr], axis=1).astype(_BF16),
        "g64": g64,
        "da_qg": jnp.tile(p["da_q_norm_g"][l], 2 * _DA_HEADS)[None] * (_DA_DIM ** -0.5 * _LOG2E),
        "da_kg": jnp.tile(p["da_k_norm_g"][l], 2 * _DA_HEADS)[None],
        "cq_g": p["mla_cq_norm_g"][l][None],
        "ckv_g": p["mla_ckv_norm_g"][l][None],
        "w_uq": w_uq.astype(_BF16),
        "w_ukv": jnp.concatenate([w_k.reshape(_MLA_KV_RANK, -1), w_v.reshape(_MLA_KV_RANK, -1)],
                                 axis=1).astype(_BF16),
        "mla_qg": pad_g(p["mla_q_norm_g"][l]) * (_MLA_QK ** -0.5 * _LOG2E),
        "mla_kg": pad_g(p["mla_k_norm_g"][l]),
        "rope_cos": rope[0], "rope_s1": rope[1], "rope_s2": rope[2],
        "lam": jnp.reshape(lam, (1,)).astype(_F32),
        "da_og": col(p["da_subln_g"][l] * (1.0 - lam_init)),
        "sb_og": col(jnp.tile(p["sb_out_g"][l], 2)),
        "mla_og": col(jnp.tile(p["mla_out_g"][l], 2)),
        "w_out": p["w_out"][l].astype(_BF16),
        "memx_g": p["memx_norm_g"][l][None],
        "w_mem_q": p["w_mem_q"][l].astype(_BF16),
        "mem_qg": jnp.tile(p["mem_q_norm_g"][l], _MEM_HEADS)[None] * (_MEM_DIM ** -0.5),
        "w_mem_o": p["w_mem_o"][l].astype(_BF16),
        "ffn_g": p["ffn_norm_g"][l][None],
        "w_ff1": p["w_ff1"][l].astype(_BF16),
        "w_ff2": p["w_ff2"][l].astype(_BF16),
    }


def kernel(x, mem, rel_bias, mix_norm_g, w_in, da_q_norm_g, da_k_norm_g, da_lambda, da_subln_g,
           sb_out_g, mla_cq_norm_g, mla_ckv_norm_g, w_mla_uq, w_mla_ukv, mla_q_norm_g, mla_k_norm_g,
           mla_out_g, w_out, memx_norm_g, mem_norm_g, w_mem_q, w_mem_kv, mem_q_norm_g, mem_k_norm_g,
           w_mem_o, ffn_norm_g, w_ff1, w_ff2):
    p = dict(mix_norm_g=mix_norm_g, w_in=w_in, da_q_norm_g=da_q_norm_g, da_k_norm_g=da_k_norm_g,
             da_lambda=da_lambda, da_subln_g=da_subln_g, sb_out_g=sb_out_g,
             mla_cq_norm_g=mla_cq_norm_g, mla_ckv_norm_g=mla_ckv_norm_g, w_mla_uq=w_mla_uq,
             w_mla_ukv=w_mla_ukv, mla_q_norm_g=mla_q_norm_g, mla_k_norm_g=mla_k_norm_g,
             mla_out_g=mla_out_g, w_out=w_out, memx_norm_g=memx_norm_g, w_mem_q=w_mem_q,
             mem_q_norm_g=mem_q_norm_g, w_mem_o=w_mem_o, ffn_norm_g=ffn_norm_g, w_ff1=w_ff1,
             w_ff2=w_ff2)
    batch, seq, d = x.shape
    depth = w_in.shape[0]
    n_mem = mem.shape[1]
    t = _ATT_TILE
    assert seq % _ROW_TILE == 0 and seq % t == 0
    assert (batch * n_mem) % _ROW_TILE == 0
    assert batch % _PAIR_BATCH == 0
    assert w_in.shape[2] == _IN_OFFS[-1]

    g64 = _group_ones(_MEM_DIM)
    rope = _rope_tables(seq)
    bias = _da_bias_tables(rel_bias, t)
    i = jnp.arange(t, dtype=jnp.int32)
    chunk_mask = jnp.where((i[:, None] // _CHUNK) <= (i[None, :] // _CHUNK), 0.0, -jnp.inf).astype(_F32)
    chunk_mask = jnp.stack([chunk_mask, jnp.zeros_like(chunk_mask)])

    width = _MEM_HEADS * _MEM_DIM
    km, vm = _memkv(mem.reshape(batch * n_mem, d), mem_norm_g[:, None, :], w_mem_kv.astype(_BF16), g64,
                    jnp.tile(mem_k_norm_g, (1, _MEM_HEADS))[:, None, :])
    km = km.reshape(depth, batch, n_mem, width)
    vm = vm.reshape(depth, batch, n_mem, width)

    x2d = x.reshape(batch * seq, d)
    for l in range(depth):
        lp = _layer_params(l, p, rope, g64)
        daq, dak, dav, sbq, sbk, sbv, mq, mk, mv = _inproj(x2d, lp, seq)
        ya = _da_attention(daq, dak, dav, bias, lp["lam"], lp["da_og"], batch, seq)
        yb = _sb_attention(sbq, sbk, sbv, lp["sb_og"], batch, seq)
        yc = _mla_attention(mq, mk, mv, chunk_mask, lp["mla_og"], batch, seq)
        x2d = _mix_mem(x2d, ya, yb, yc, lp, km[l], vm[l], seq)
        x2d = _ffn(x2d, lp["ffn_g"], lp["w_ff1"], lp["w_ff2"])
    return x2d.reshape(batch, seq, d)
```

```python
import functools
import math

import numpy as np
import jax
import jax.numpy as jnp
from jax import lax
from jax.experimental import pallas as pl
from jax.experimental.pallas import tpu as pltpu

_F32 = jnp.float32
_BF16 = jnp.bfloat16
_EPS = 1e-6

_CHUNK = 64
_DA_HEADS, _DA_DIM = 4, 64
_DA_VDIM = 2 * _DA_DIM
_SB_HEADS, _SB_DIM = 4, 64
_MLA_HEADS, _MLA_NOPE, _MLA_ROPE, _MLA_V = 4, 64, 32, 64
_MLA_QK = _MLA_NOPE + _MLA_ROPE
_MLA_Q_RANK, _MLA_KV_RANK = 256, 128
_ROPE_THETA = 10000.0
_NUM_BUCKETS, _MAX_DISTANCE = 32, 128
_MEM_HEADS, _MEM_DIM = 4, 64

_LANES = 128
_MXU = 256
_VMEM_BYTES = 64 * 1024 * 1024
_VMEM_LIMIT = _VMEM_BYTES - 12 * 1024 * 1024

_ATT_TILE = 256
_ROW_TILE = 1024

_DA_WIDTH = _DA_HEADS * _DA_VDIM
_SB_WIDTH = _SB_HEADS * _SB_DIM
_MLA_QK_WIDTH = _MLA_HEADS * _LANES
_MLA_V_WIDTH = _MLA_HEADS * _MLA_V
_IN_SIZES = (_DA_WIDTH, _DA_WIDTH, _DA_WIDTH, _SB_WIDTH, _SB_WIDTH, _SB_WIDTH,
             _MLA_Q_RANK, _MLA_KV_RANK, _MLA_ROPE)
_IN_OFFS = tuple(int(v) for v in np.cumsum((0,) + _IN_SIZES))
_NT = (((1,), (1,)), ((), ()))
_LOG2E = math.log2(math.e)


def _const_spec(shape):
    zeros = (0,) * len(shape)
    return pl.BlockSpec(shape, lambda *_: zeros, pipeline_mode=pl.Buffered(1))


def _params(*sem):
    return pltpu.CompilerParams(dimension_semantics=sem, vmem_limit_bytes=_VMEM_LIMIT)


def _rms(x, g):
    return x * lax.rsqrt(jnp.mean(x * x, axis=-1, keepdims=True) + _EPS) * g


def _group_mean_sq(y, gmat, group):
    sq = (y * y).astype(_BF16)
    cols = y.shape[1]
    parts = [jnp.dot(sq[:, c:c + _MXU], gmat, preferred_element_type=_F32)
             for c in range(0, cols, _MXU)]
    ss = parts[0] if len(parts) == 1 else jnp.concatenate(parts, axis=1)
    return ss * (1.0 / group)


def _inproj_kernel(x_ref, gmix_ref, w_ref, g64_ref, gq_ref, gk_ref, cqg_ref, ckvg_ref,
                   wuq_ref, wukv_ref, qg_ref, kg_ref, cos_ref, s1_ref, s2_ref,
                   daq_ref, dak_ref, dav_ref, sbq_ref, sbk_ref, sbv_ref,
                   mq_ref, mk_ref, mv_ref):
    x = x_ref[...]
    h = _rms(x, gmix_ref[...]).astype(_BF16)
    o = _IN_OFFS

    def proj(seg, width=None):
        hi = o[seg + 1] if width is None else o[seg] + width
        return jnp.dot(h, w_ref[:, o[seg]:hi], preferred_element_type=_F32)

    g64 = g64_ref[...]
    t = daq_ref.shape[2]

    def norm64(y, g):
        return y * lax.rsqrt(_group_mean_sq(y, g64, _DA_DIM) + _EPS) * g

    def store_t(ref, y, row0=0):
        for r in range(y.shape[0] // t):
            ref[r, row0:row0 + y.shape[1], :] = y[r * t:(r + 1) * t, :].T.astype(_BF16)

    cos, s1, s2 = cos_ref[...], s1_ref[...], s2_ref[...]

    def head_norm_rope(y, g):
        ms = jnp.sum(y * y, axis=-1, keepdims=True) * (1.0 / _MLA_QK)
        yn = y * lax.rsqrt(ms + _EPS) * g
        half = _MLA_ROPE // 2
        return (yn * cos + pltpu.roll(yn, half, 1) * s1
                + pltpu.roll(yn, _LANES - half, 1) * s2)

    cq = _rms(proj(6), cqg_ref[...]).astype(_BF16)
    q_all = jnp.dot(cq, wuq_ref[...], preferred_element_type=_F32)
    ckv = _rms(proj(7), ckvg_ref[...]).astype(_BF16)
    kv_all = jnp.dot(ckv, wukv_ref[...], preferred_element_type=_F32)
    k_rope = proj(8, _LANES)

    def latent_head(hd):
        sl = slice(hd * _LANES, (hd + 1) * _LANES)
        store_t(mq_ref, head_norm_rope(q_all[:, sl], qg_ref[:, sl]), hd * _LANES)
        mk_ref[:, sl] = head_norm_rope(kv_all[:, sl] + k_rope, kg_ref[:, sl]).astype(_BF16)

    store_t(daq_ref, norm64(proj(0), gq_ref[...]))
    latent_head(0)
    dak_ref[...] = norm64(proj(1), gk_ref[...]).astype(_BF16)
    latent_head(1)
    store_t(dav_ref, proj(2))
    latent_head(2)
    store_t(sbq_ref, proj(3) * (_SB_DIM ** -0.5 * _LOG2E))
    latent_head(3)
    sbk_ref[...] = proj(4).astype(_BF16)
    store_t(mv_ref, kv_all[:, _MLA_HEADS * _LANES:])
    store_t(sbv_ref, proj(5))


def _inproj(x2d, lp, seq):
    m, d = x2d.shape
    tm = _ROW_TILE
    t = _ATT_TILE
    pos_blocks = seq // tm
    tab = pl.BlockSpec((tm, _LANES), lambda i: (i % pos_blocks, 0))
    widths = (_DA_WIDTH, _DA_WIDTH, _DA_WIDTH, _SB_WIDTH, _SB_WIDTH, _SB_WIDTH,
              _MLA_QK_WIDTH, _MLA_QK_WIDTH, _MLA_V_WIDTH)
    transposed = (True, False, True, True, False, True, True, False, True)
    out_specs, out_shape = [], []
    for c, tr in zip(widths, transposed):
        if tr:
            out_specs.append(pl.BlockSpec((tm // t, c, t), lambda i: (i, 0, 0)))
            out_shape.append(jax.ShapeDtypeStruct((m // t, c, t), _BF16))
        else:
            out_specs.append(pl.BlockSpec((tm, c), lambda i: (i, 0)))
            out_shape.append(jax.ShapeDtypeStruct((m, c), _BF16))
    row = lambda c: pl.BlockSpec((tm, c), lambda i: (i, 0))
    return pl.pallas_call(
        _inproj_kernel,
        grid=(m // tm,),
        in_specs=[row(d), _const_spec((1, d)), _const_spec(lp["w_in"].shape),
                  _const_spec((_MXU, _MXU)), _const_spec((1, _DA_WIDTH)), _const_spec((1, _DA_WIDTH)),
                  _const_spec((1, _MLA_Q_RANK)), _const_spec((1, _MLA_KV_RANK)),
                  _const_spec(lp["w_uq"].shape), _const_spec(lp["w_ukv"].shape),
                  _const_spec((1, _MLA_QK_WIDTH)), _const_spec((1, _MLA_QK_WIDTH)), tab, tab, tab],
        out_specs=out_specs,
        out_shape=out_shape,
        compiler_params=_params("parallel"),
        name="inproj",
    )(x2d, lp["mix_g"], lp["w_in"], lp["g64"], lp["da_qg"], lp["da_kg"], lp["cq_g"],
      lp["ckv_g"], lp["w_uq"], lp["w_ukv"], lp["mla_qg"], lp["mla_kg"],
      lp["rope_cos"], lp["rope_s1"], lp["rope_s2"])


_SUM_ROWS = 16


def _pipelined_tiles(n_tiles, n_chains, stage, consume, run, s_even, s_odd, group):
    groups = [range(g, min(g + group, n_chains)) for g in range(0, n_chains, group)]

    def overlap(j_stage, buf_stage, j_done, buf_done, summary, run):
        new_summary, new_run = [], []
        new_summary += stage(j_stage, buf_stage, groups[0], False)
        for g, members in enumerate(groups):
            if g + 1 < len(groups):
                new_summary += stage(j_stage, buf_stage, groups[g + 1], False)
            new_run += consume(j_done, buf_done, members,
                               [summary[c] for c in members], [run[c] for c in members])
        return tuple(new_run), tuple(new_summary)

    def finish(j, buf, summary, run):
        return tuple(consume(j, buf, range(n_chains), list(summary), list(run)))

    def pair(i, carry):
        run, summary = carry
        run, summary = overlap(2 * i + 1, s_odd, 2 * i, s_even, summary, run)
        return overlap(2 * i + 2, s_even, 2 * i + 1, s_odd, summary, run)

    def tail_two(_, carry):
        run, summary = carry
        run, summary = overlap(n_tiles - 1, s_odd, n_tiles - 2, s_even, summary, run)
        return finish(n_tiles - 1, s_odd, summary, run), summary

    def tail_one(_, carry):
        run, summary = carry
        return finish(n_tiles - 1, s_even, summary, run), summary

    carry = (run, tuple(stage(0, s_even, range(n_chains), True)))
    last = n_tiles - 1
    odd = jnp.bitwise_and(last, 1)
    carry = lax.fori_loop(0, jnp.right_shift(last, 1), pair, carry)
    carry = lax.fori_loop(0, odd, tail_two, carry)
    lax.fori_loop(0, 1 - odd, tail_one, carry)


def _online_softmax(n_tiles, score, value, s_even, s_odd, acc_scr):
    n_chains, _, t = s_even.shape
    ones = jnp.ones((_SUM_ROWS, t), _BF16)

    def stage(ki, buf, members, first):
        col_max = []
        for c in members:
            s = score(c, ki)
            buf[c] = s
            col_max.append(jnp.max(s, axis=0, keepdims=True))
        return col_max

    def consume(ki, buf, members, col_max, m_run):
        out = []
        for c, cm, m_old in zip(members, col_max, m_run):
            m_new = jnp.maximum(m_old, cm)
            alpha = jnp.exp2(m_old - m_new)
            p = jnp.exp2(buf[c] - m_new).astype(_BF16)
            v_ones = jnp.concatenate([value(c, ki), ones], axis=0)
            acc_scr[c] = alpha * acc_scr[c] + jnp.dot(v_ones, p, preferred_element_type=_F32)
            out.append(m_new)
        return out

    for c in range(n_chains):
        acc_scr[c] = jnp.zeros(acc_scr.shape[1:], _F32)
    m_init = tuple(jnp.full((1, t), -jnp.inf, _F32) for _ in range(n_chains))
    _pipelined_tiles(n_tiles, n_chains, stage, consume, m_init, s_even, s_odd, group=1)


def _softmax_finish(acc, rows):
    width = acc.shape[0] - _SUM_ROWS
    return acc[rows] * (1.0 / acc[width:width + 1])


def _rows(i, t, base=0):
    return pl.ds(pl.multiple_of(base + i * t, t), t)


def _split_rows(x_t, half):
    zero = jnp.zeros((half, x_t.shape[1]), x_t.dtype)
    return (jnp.concatenate([x_t[:half], zero], axis=0),
            jnp.concatenate([zero, x_t[half:]], axis=0))


def _half_row_norm(o_t, half, g_t):
    sq = o_t * o_t
    lo = lax.rsqrt(jnp.mean(sq[:half], axis=0, keepdims=True) + _EPS)
    hi = lax.rsqrt(jnp.mean(sq[half:], axis=0, keepdims=True) + _EPS)
    return jnp.concatenate([o_t[:half] * lo, o_t[half:] * hi], axis=0) * g_t


def _da_kernel(lam_ref, q_ref, k_ref, v_ref, bias_ref, g_ref, o_ref, s_even, s_odd, acc_scr, *, t):
    nq = q_ref.shape[0]
    lam = lam_ref[0]

    heads = range(_DA_HEADS)
    head = lambda hd: slice(hd * _LANES, (hd + 1) * _LANES)
    all_rows = slice(0, _DA_VDIM)

    def q_tile(qi, carry):
        qm = [qh for hd in heads for qh in _split_rows(q_ref[qi, head(hd), :], _DA_DIM)]

        def score(c, ki):
            hd, mi = divmod(c, 2)
            d = jnp.minimum(qi - ki, 2)
            return (jnp.dot(k_ref[_rows(ki, t), head(hd)], qm[c], preferred_element_type=_F32)
                    + bias_ref[hd, d, mi])

        value = lambda c, ki: v_ref[ki, head(c // 2), :]
        _online_softmax(qi + 1, score, value, s_even, s_odd, acc_scr)
        for hd in heads:
            o_t = (_softmax_finish(acc_scr[2 * hd], all_rows)
                   - lam * _softmax_finish(acc_scr[2 * hd + 1], all_rows))
            o_t = o_t * lax.rsqrt(jnp.mean(o_t * o_t, axis=0, keepdims=True) + _EPS) * g_ref[...]
            o_ref[_rows(qi, t), head(hd)] = o_t.T.astype(o_ref.dtype)
        return carry

    lax.fori_loop(0, nq, q_tile, 0)


def _da_attention(q_t, k, v_t, bias, lam, g_t, batch, seq):
    t = _ATT_TILE
    nq = seq // t
    width = _DA_HEADS * _LANES
    tiles = pl.BlockSpec((nq, width, t), lambda b: (b, 0, 0))
    rows = pl.BlockSpec((seq, width), lambda b: (b, 0))
    return pl.pallas_call(
        functools.partial(_da_kernel, t=t),
        grid=(batch,),
        in_specs=[pl.BlockSpec(memory_space=pltpu.SMEM), tiles, rows, tiles,
                  _const_spec(bias.shape), _const_spec((_LANES, t))],
        out_specs=rows,
        out_shape=jax.ShapeDtypeStruct(k.shape, _BF16),
        scratch_shapes=[pltpu.VMEM((2 * _DA_HEADS, t, t), _F32), pltpu.VMEM((2 * _DA_HEADS, t, t), _F32),
                        pltpu.VMEM((2 * _DA_HEADS, _DA_VDIM + _SUM_ROWS, t), _F32)],
        compiler_params=_params("parallel"),
        name="diff_attention",
    )(lam, q_t, k, v_t, bias, g_t)


_PAIR_BATCH = 2
_SB_LOGIT_LOOKAHEAD = 2


def _mla_kernel(q_ref, k_ref, v_ref, mask_ref, g_ref, o_ref, s_even, s_odd, acc_scr, *, t, nb):
    nq = q_ref.shape[0] // nb
    seq = k_ref.shape[0] // nb

    head = lambda hd: slice(hd * _LANES, (hd + 1) * _LANES)
    pair = lambda hd: slice((hd // 2) * _LANES, (hd // 2 + 1) * _LANES)
    chains = [(bb, hd) for bb in range(nb) for hd in range(_MLA_HEADS)]

    def q_tile(qi, carry):
        qs = [q_ref[bb * nq + qi, head(hd), :] for bb, hd in chains]

        def score(c, ki):
            bb, hd = chains[c]
            d = jnp.minimum(qi - ki, 1)
            return (jnp.dot(k_ref[_rows(ki, t, bb * seq), head(hd)], qs[c], preferred_element_type=_F32)
                    + mask_ref[d])

        value = lambda c, ki: v_ref[chains[c][0] * nq + ki, pair(chains[c][1]), :]
        _online_softmax(qi + 1, score, value, s_even, s_odd, acc_scr)
        for bb in range(nb):
            for p in range(_MLA_HEADS // 2):
                c = bb * _MLA_HEADS + 2 * p
                o_t = jnp.concatenate([_softmax_finish(acc_scr[c], slice(0, _MLA_V)),
                                       _softmax_finish(acc_scr[c + 1], slice(_MLA_V, _LANES))], axis=0)
                o_ref[_rows(qi, t, bb * seq), head(p)] = (
                    _half_row_norm(o_t, _MLA_V, g_ref[...]).T.astype(o_ref.dtype))
        return carry

    lax.fori_loop(0, nq, q_tile, 0)


def _mla_attention(q_t, k, v_t, mask_t, g_t, batch, seq):
    t = _ATT_TILE
    nb = _PAIR_BATCH
    nq = seq // t
    qk_width = _MLA_HEADS * _LANES
    v_width = _MLA_HEADS * _MLA_V
    chains = nb * _MLA_HEADS
    return pl.pallas_call(
        functools.partial(_mla_kernel, t=t, nb=nb),
        grid=(batch // nb,),
        in_specs=[pl.BlockSpec((nb * nq, qk_width, t), lambda b: (b, 0, 0)),
                  pl.BlockSpec((nb * seq, qk_width), lambda b: (b, 0)),
                  pl.BlockSpec((nb * nq, v_width, t), lambda b: (b, 0, 0)),
                  _const_spec((2, t, t)), _const_spec((_LANES, t))],
        out_specs=pl.BlockSpec((nb * seq, v_width), lambda b: (b, 0)),
        out_shape=jax.ShapeDtypeStruct((batch * seq, v_width), _BF16),
        scratch_shapes=[pltpu.VMEM((chains, t, t), _F32), pltpu.VMEM((chains, t, t), _F32),
                        pltpu.VMEM((chains, _LANES + _SUM_ROWS, t), _F32)],
        compiler_params=_params("parallel"),
        name="latent_attention",
    )(q_t, k, v_t, mask_t, g_t)


def _sb_kernel(q_ref, k_ref, v_ref, g_ref, o_ref, s_even, s_odd, acc_scr, *, t, nb):
    nq = q_ref.shape[0] // nb
    seq = k_ref.shape[0] // nb
    key_idx = lax.broadcasted_iota(jnp.int32, (t, t), 0)
    query_idx = lax.broadcasted_iota(jnp.int32, (t, t), 1)
    earlier = key_idx < query_idx
    tri = jnp.where(earlier, 1.0, 0.0).astype(_BF16)

    pair = lambda hd: slice((hd // 2) * _LANES, (hd // 2 + 1) * _LANES)
    dot = functools.partial(jnp.dot, preferred_element_type=_F32)
    chains = [(bb, hd) for bb in range(nb) for hd in range(_SB_HEADS)]
    n_chains = range(len(chains))

    def q_tile(qi, carry):
        qh = [q for bb in range(nb) for p in range(_SB_HEADS // 2)
              for q in _split_rows(q_ref[bb * nq + qi, pair(2 * p), :], _SB_DIM)]

        def stage(j, buf, members, diag):
            members = list(members)
            logit = lambda c: dot(k_ref[_rows(qi - j, t, chains[c][0] * seq), pair(chains[c][1])], qh[c])
            z = [logit(c) for c in members[:_SB_LOGIT_LOOKAHEAD]]
            log_beta, first_row, later = [], [], []
            for n in range(len(members)):
                if n + _SB_LOGIT_LOOKAHEAD < len(members):
                    z.append(logit(members[n + _SB_LOGIT_LOOKAHEAD]))
                zi = z[n]
                lp = jnp.log2(1.0 + jnp.exp2(jnp.minimum(zi, -zi)))
                lb = jnp.minimum(zi, 0.0) - lp
                log_1m = lb - zi
                if diag:
                    log_1m = jnp.where(earlier, log_1m, 0.0)
                later.append(dot(tri, log_1m.astype(_BF16)))
                log_beta.append(lb)
                first_row.append(log_1m[0:1, :])
            for c, lb, la in zip(members, log_beta, later):
                log_w = lb + la
                buf[c] = jnp.where(earlier, log_w, -jnp.inf) if diag else log_w
            return [la[0:1, :] + fr for la, fr in zip(later, first_row)]

        def consume(j, buf, members, through, run):
            for c, r in zip(members, run):
                bb, hd = chains[c]
                a = jnp.exp2(buf[c] + r).astype(_BF16)
                acc_scr[c] = acc_scr[c] + dot(v_ref[bb * nq + qi - j, pair(hd), :], a)
            return [r + th for r, th in zip(run, through)]

        for c in n_chains:
            acc_scr[c] = jnp.zeros(acc_scr.shape[1:], _F32)
        run0 = tuple(jnp.zeros((1, t), _F32) for _ in n_chains)
        _pipelined_tiles(qi + 1, len(chains), stage, consume, run0, s_even, s_odd, group=len(chains))
        for bb in range(nb):
            for p in range(_SB_HEADS // 2):
                c = bb * _SB_HEADS + 2 * p
                o_t = jnp.concatenate([acc_scr[c, :_SB_DIM, :], acc_scr[c + 1, _SB_DIM:, :]], axis=0)
                o_ref[_rows(qi, t, bb * seq), pair(2 * p)] = (
                    _half_row_norm(o_t, _SB_DIM, g_ref[...]).T.astype(o_ref.dtype))
        return carry

    lax.fori_loop(0, nq, q_tile, 0)


def _sb_attention(q_t, k, v_t, g_t, batch, seq):
    t = _ATT_TILE
    nb = _PAIR_BATCH
    nq = seq // t
    width = _SB_HEADS * _SB_DIM
    chains = nb * _SB_HEADS
    tiles = pl.BlockSpec((nb * nq, width, t), lambda b: (b, 0, 0))
    rows = pl.BlockSpec((nb * seq, width), lambda b: (b, 0))
    return pl.pallas_call(
        functools.partial(_sb_kernel, t=t, nb=nb),
        grid=(batch // nb,),
        in_specs=[tiles, rows, tiles, _const_spec((_LANES, t))],
        out_specs=rows,
        out_shape=jax.ShapeDtypeStruct(k.shape, _BF16),
        scratch_shapes=[pltpu.VMEM((chains, t, t), _F32), pltpu.VMEM((chains, t, t), _F32),
                        pltpu.VMEM((chains, _LANES, t), _F32)],
        compiler_params=_params("parallel"),
        name="stick_breaking",
    )(q_t, k, v_t, g_t)


def _memkv_kernel(mem_ref, g_ref, w_ref, g64_ref, kg_ref, k_ref, v_ref):
    m = _rms(mem_ref[...], g_ref[...]).astype(_BF16)
    kv = jnp.dot(m, w_ref[...], preferred_element_type=_F32)
    width = _MEM_HEADS * _MEM_DIM
    k = kv[:, :width]
    k = k * lax.rsqrt(_group_mean_sq(k, g64_ref[...], _MEM_DIM) + _EPS) * kg_ref[...]
    k_ref[...] = k.astype(_BF16)
    v_ref[...] = kv[:, width:].astype(_BF16)


def _memkv(mem2d, g, w, g64, kg):
    depth = w.shape[0]
    rows, d = mem2d.shape
    tm = _ROW_TILE
    width = _MEM_HEADS * _MEM_DIM
    out = pl.BlockSpec((None, tm, width), lambda l, i: (l, i, 0))
    return pl.pallas_call(
        _memkv_kernel,
        grid=(depth, rows // tm),
        in_specs=[pl.BlockSpec((tm, d), lambda l, i: (i, 0)),
                  pl.BlockSpec((None, 1, d), lambda l, i: (l, 0, 0)),
                  pl.BlockSpec((None, d, 2 * width), lambda l, i: (l, 0, 0)),
                  _const_spec((_MXU, _MXU)),
                  pl.BlockSpec((None, 1, width), lambda l, i: (l, 0, 0))],
        out_specs=[out, out],
        out_shape=[jax.ShapeDtypeStruct((depth, rows, width), _BF16)] * 2,
        compiler_params=_params("parallel", "parallel"),
        name="memory_kv",
    )(mem2d, g, w, g64, kg)


_FF_CHUNK = 1024
_FUSED_ROWS = 512


def _mix_mem_ffn_kernel(x_ref, ya_ref, yb_ref, yc_ref, wo_ref, gx_ref, wq_ref, g64_ref, qg_ref,
                        km_ref, vm_ref, wmo_ref, gf_ref, w1_ref, w2_ref, o_ref):
    wa = ya_ref.shape[1]
    wb = wa + yb_ref.shape[1]
    x = (x_ref[...]
         + jnp.dot(ya_ref[...], wo_ref[:wa, :], preferred_element_type=_F32)
         + jnp.dot(yb_ref[...], wo_ref[wa:wb, :], preferred_element_type=_F32)
         + jnp.dot(yc_ref[...], wo_ref[wb:, :], preferred_element_type=_F32))
    h = _rms(x, gx_ref[...]).astype(_BF16)
    q = jnp.dot(h, wq_ref[...], preferred_element_type=_F32)
    q = (q * lax.rsqrt(_group_mean_sq(q, g64_ref[...], _MEM_DIM) + _EPS) * qg_ref[...]).astype(_BF16)
    km = km_ref[...]
    vm = vm_ref[...]
    head_of_lane = lax.broadcasted_iota(jnp.int32, (1, q.shape[1]), 1) // _MEM_DIM
    zero = jnp.zeros_like(q)
    o = jnp.zeros(q.shape, _F32)
    sels = [head_of_lane == hd for hd in range(_MEM_HEADS)]
    scores = [lax.dot_general(jnp.where(sel, q, zero), km, _NT, preferred_element_type=_F32) for sel in sels]
    for sel, s in zip(sels, scores):
        p = jnp.exp(s - jnp.max(s, axis=-1, keepdims=True))
        oh = jnp.dot(p.astype(_BF16), vm, preferred_element_type=_F32)
        o = jnp.where(sel, oh / jnp.sum(p, axis=-1, keepdims=True), o)
    x = x + jnp.dot(o.astype(_BF16), wmo_ref[...], preferred_element_type=_F32)
    h = _rms(x, gf_ref[...]).astype(_BF16)
    acc = x
    for c in range(0, w1_ref.shape[1], _FF_CHUNK):
        u = jnp.dot(h, w1_ref[:, c:c + _FF_CHUNK], preferred_element_type=_F32)
        r = jnp.maximum(u, 0.0)
        acc = acc + jnp.dot((r * r).astype(_BF16), w2_ref[c:c + _FF_CHUNK, :],
                            preferred_element_type=_F32)
    o_ref[...] = acc


def _mix_mem_ffn(x2d, ya, yb, yc, lp, km, vm, seq):
    m, d = x2d.shape
    tm = _FUSED_ROWS
    per_seq = seq // tm
    n_mem, width = km.shape[1], km.shape[2]
    row = lambda c: pl.BlockSpec((tm, c), lambda i: (i, 0))
    mem = pl.BlockSpec((None, n_mem, width), lambda i: (i // per_seq, 0, 0))
    return pl.pallas_call(
        _mix_mem_ffn_kernel,
        grid=(m // tm,),
        in_specs=[row(d), row(ya.shape[1]), row(yb.shape[1]), row(yc.shape[1]),
                  _const_spec(lp["w_out"].shape), _const_spec((1, d)),
                  _const_spec(lp["w_mem_q"].shape), _const_spec((_MXU, _MXU)),
                  _const_spec((1, width)), mem, mem, _const_spec(lp["w_mem_o"].shape),
                  _const_spec((1, d)), _const_spec(lp["w_ff1"].shape), _const_spec(lp["w_ff2"].shape)],
        out_specs=row(d),
        out_shape=jax.ShapeDtypeStruct((m, d), _F32),
        compiler_params=_params("parallel"),
        name="mix_memory_ffn",
    )(x2d, ya, yb, yc, lp["w_out"], lp["memx_g"], lp["w_mem_q"], lp["g64"], lp["mem_qg"],
      km, vm, lp["w_mem_o"], lp["ffn_g"], lp["w_ff1"], lp["w_ff2"])


def _t5_bucket(rel):
    nb = _NUM_BUCKETS // 2
    bucket = (rel > 0).astype(jnp.int32) * nb
    n = jnp.abs(rel)
    max_exact = nb // 2
    is_small = n < max_exact
    large = max_exact + (jnp.log(jnp.maximum(n, 1).astype(jnp.float32) / max_exact)
                         / math.log(_MAX_DISTANCE / max_exact) * (nb - max_exact)).astype(jnp.int32)
    large = jnp.minimum(large, nb - 1)
    return bucket + jnp.where(is_small, n, large)


def _da_bias_tables(rel_bias, t):
    assert t + 1 >= _MAX_DISTANCE and t % _CHUNK == 0
    j = jnp.arange(t, dtype=jnp.int32)[:, None]
    i = jnp.arange(t, dtype=jnp.int32)[None, :]
    rb = rel_bias.astype(_F32) * _LOG2E

    def lookup(bucket):
        hit = bucket[:, :, None, None] == jnp.arange(_NUM_BUCKETS, dtype=jnp.int32)[:, None]
        return jnp.sum(jnp.where(hit, rb[None, None], 0.0), axis=2)

    b0 = lookup(_t5_bucket(j - i))
    b0 = jnp.where(((j // _CHUNK) <= (i // _CHUNK))[:, :, None], b0, -jnp.inf)
    b1 = lookup(_t5_bucket(j - i - t))
    far = jnp.broadcast_to(lookup(_t5_bucket(jnp.full((1, 1), -(t + 1), jnp.int32))), b1.shape)
    tab = jnp.stack([b0, b1, far]).reshape(3, t, t, _DA_HEADS, 2)
    return tab.transpose(3, 0, 4, 1, 2)


def _rope_tables(seq):
    half = _MLA_ROPE // 2
    freqs = _ROPE_THETA ** (-jnp.arange(half, dtype=jnp.float32) / half)
    ang = jnp.arange(seq, dtype=jnp.int32).astype(jnp.float32)[:, None] * freqs[None, :]
    cos, sin = jnp.cos(ang), jnp.sin(ang)
    ones = jnp.ones((seq, _MLA_NOPE), _F32)
    z = lambda w: jnp.zeros((seq, w), _F32)
    tail = _LANES - _MLA_QK
    c = jnp.concatenate([ones, cos, cos, z(tail)], axis=1)
    s1 = jnp.concatenate([z(_MLA_NOPE + half), sin, z(tail)], axis=1)
    s2 = jnp.concatenate([z(_MLA_NOPE), -sin, z(half + tail)], axis=1)
    return c, s1, s2


def _group_ones(group):
    idx = np.arange(_MXU) // group
    return jnp.asarray(idx[:, None] == idx[None, :], dtype=_BF16)


def _layer_params(l, p, rope, g64):
    d = p["w_in"].shape[1]
    o = _IN_OFFS
    w_in = p["w_in"][l]
    kr = jnp.zeros((d, _LANES), _F32).at[:, _MLA_NOPE:_MLA_QK].set(w_in[:, o[8]:o[9]])
    w_uq = jnp.pad(p["w_mla_uq"][l].reshape(_MLA_Q_RANK, _MLA_HEADS, _MLA_QK),
                   ((0, 0), (0, 0), (0, _LANES - _MLA_QK))).reshape(_MLA_Q_RANK, -1)
    w_ukv = p["w_mla_ukv"][l].reshape(_MLA_KV_RANK, _MLA_HEADS, _MLA_NOPE + _MLA_V)
    w_k = jnp.pad(w_ukv[:, :, :_MLA_NOPE], ((0, 0), (0, 0), (0, _LANES - _MLA_NOPE)))
    w_v = w_ukv[:, :, _MLA_NOPE:]
    pad_g = lambda g: jnp.tile(jnp.pad(g, (0, _LANES - _MLA_QK)), _MLA_HEADS)[None]
    col = lambda g: jnp.broadcast_to(g[:, None], (g.shape[0], _ATT_TILE))
    lam_init = 0.8 - 0.6 * math.exp(-0.3 * l)
    lp = p["da_lambda"][l].astype(_F32)
    lam = jnp.exp(jnp.sum(lp[0] * lp[1])) - jnp.exp(jnp.sum(lp[2] * lp[3])) + lam_init
    return {
        "mix_g": p["mix_norm_g"][l][None],
        "w_in": jnp.concatenate([w_in[:, :o[8]], kr], axis=1).astype(_BF16),
        "g64": g64,
        "da_qg": jnp.tile(p["da_q_norm_g"][l], 2 * _DA_HEADS)[None] * (_DA_DIM ** -0.5 * _LOG2E),
        "da_kg": jnp.tile(p["da_k_norm_g"][l], 2 * _DA_HEADS)[None],
        "cq_g": p["mla_cq_norm_g"][l][None],
        "ckv_g": p["mla_ckv_norm_g"][l][None],
        "w_uq": w_uq.astype(_BF16),
        "w_ukv": jnp.concatenate([w_k.reshape(_MLA_KV_RANK, -1), w_v.reshape(_MLA_KV_RANK, -1)],
                                 axis=1).astype(_BF16),
        "mla_qg": pad_g(p["mla_q_norm_g"][l]) * (_MLA_QK ** -0.5 * _LOG2E),
        "mla_kg": pad_g(p["mla_k_norm_g"][l]),
        "rope_cos": rope[0], "rope_s1": rope[1], "rope_s2": rope[2],
        "lam": jnp.reshape(lam, (1,)).astype(_F32),
        "da_og": col(p["da_subln_g"][l] * (1.0 - lam_init)),
        "sb_og": col(jnp.tile(p["sb_out_g"][l], 2)),
        "mla_og": col(jnp.tile(p["mla_out_g"][l], 2)),
        "w_out": p["w_out"][l].astype(_BF16),
        "memx_g": p["memx_norm_g"][l][None],
        "w_mem_q": p["w_mem_q"][l].astype(_BF16),
        "mem_qg": jnp.tile(p["mem_q_norm_g"][l], _MEM_HEADS)[None] * (_MEM_DIM ** -0.5),
        "w_mem_o": p["w_mem_o"][l].astype(_BF16),
        "ffn_g": p["ffn_norm_g"][l][None],
        "w_ff1": p["w_ff1"][l].astype(_BF16),
        "w_ff2": p["w_ff2"][l].astype(_BF16),
    }


def kernel(x, mem, rel_bias, mix_norm_g, w_in, da_q_norm_g, da_k_norm_g, da_lambda, da_subln_g,
           sb_out_g, mla_cq_norm_g, mla_ckv_norm_g, w_mla_uq, w_mla_ukv, mla_q_norm_g, mla_k_norm_g,
           mla_out_g, w_out, memx_norm_g, mem_norm_g, w_mem_q, w_mem_kv, mem_q_norm_g, mem_k_norm_g,
           w_mem_o, ffn_norm_g, w_ff1, w_ff2):
    p = dict(mix_norm_g=mix_norm_g, w_in=w_in, da_q_norm_g=da_q_norm_g, da_k_norm_g=da_k_norm_g,
             da_lambda=da_lambda, da_subln_g=da_subln_g, sb_out_g=sb_out_g,
             mla_cq_norm_g=mla_cq_norm_g, mla_ckv_norm_g=mla_ckv_norm_g, w_mla_uq=w_mla_uq,
             w_mla_ukv=w_mla_ukv, mla_q_norm_g=mla_q_norm_g, mla_k_norm_g=mla_k_norm_g,
             mla_out_g=mla_out_g, w_out=w_out, memx_norm_g=memx_norm_g, w_mem_q=w_mem_q,
             mem_q_norm_g=mem_q_norm_g, w_mem_o=w_mem_o, ffn_norm_g=ffn_norm_g, w_ff1=w_ff1,
             w_ff2=w_ff2)
    batch, seq, d = x.shape
    depth = w_in.shape[0]
    n_mem = mem.shape[1]
    t = _ATT_TILE
    assert seq % _ROW_TILE == 0 and seq % t == 0
    assert (batch * n_mem) % _ROW_TILE == 0
    assert batch % _PAIR_BATCH == 0
    assert w_in.shape[2] == _IN_OFFS[-1]

    g64 = _group_ones(_MEM_DIM)
    rope = _rope_tables(seq)
    bias = _da_bias_tables(rel_bias, t)
    i = jnp.arange(t, dtype=jnp.int32)
    chunk_mask = jnp.where((i[:, None] // _CHUNK) <= (i[None, :] // _CHUNK), 0.0, -jnp.inf).astype(_F32)
    chunk_mask = jnp.stack([chunk_mask, jnp.zeros_like(chunk_mask)])

    width = _MEM_HEADS * _MEM_DIM
    km, vm = _memkv(mem.reshape(batch * n_mem, d), mem_norm_g[:, None, :], w_mem_kv.astype(_BF16), g64,
                    jnp.tile(mem_k_norm_g, (1, _MEM_HEADS))[:, None, :])
    km = km.reshape(depth, batch, n_mem, width)
    vm = vm.reshape(depth, batch, n_mem, width)

    x2d = x.reshape(batch * seq, d)
    for l in range(depth):
        lp = _layer_params(l, p, rope, g64)
        daq, dak, dav, sbq, sbk, sbv, mq, mk, mv = _inproj(x2d, lp, seq)
        ya = _da_attention(daq, dak, dav, bias, lp["lam"], lp["da_og"], batch, seq)
        yb = _sb_attention(sbq, sbk, sbv, lp["sb_og"], batch, seq)
        yc = _mla_attention(mq, mk, mv, chunk_mask, lp["mla_og"], batch, seq)
        x2d = _mix_mem_ffn(x2d, ya, yb, yc, lp, km[l], vm[l], seq)
    return x2d.reshape(batch, seq, d)
```
